```python
import jax, jax.numpy as jnp
from jax import lax
import numpy as np

D_MODEL = 2048
BATCH = 4
SEQ = 2048
DEPTH = 1

N_HEADS = 16
QK_NOPE_DIM = 128
QK_ROPE_DIM = 64
V_HEAD_DIM = 128
Q_LORA_RANK = 512
KV_LORA_RANK = 512
ROPE_THETA = 10000.0
Q_BLOCK = 128

SGU_GROUPS = 8
SGU_GROUP_DIM = 128
SGU_WIDTH = SGU_GROUPS * SGU_GROUP_DIM
CHUNK = 128

D_FF = -(-8 * D_MODEL // (3 * 256)) * 256

N_BRANCH = 2
RMS_EPS = 1e-6

D_IN = Q_LORA_RANK + KV_LORA_RANK + QK_ROPE_DIM + 2 * SGU_WIDTH + N_BRANCH * D_MODEL

kernel_name = "hybrid_mla_sgu_gated_block"


def rms_norm(x, g):
    xf = x.astype(jnp.float32)
    y = xf * lax.rsqrt(jnp.mean(xf * xf, axis=-1, keepdims=True) + RMS_EPS)
    return (y * g.astype(jnp.float32)).astype(x.dtype)


def rope_tables(positions):
    inv_freq = ROPE_THETA ** (-jnp.arange(0, QK_ROPE_DIM, 2, dtype=jnp.float32) / QK_ROPE_DIM)
    ang = positions.astype(jnp.float32)[..., None] * inv_freq
    return jnp.cos(ang), jnp.sin(ang)


def apply_rope(x, cos, sin):
    xf = x.astype(jnp.float32)
    x1, x2 = jnp.split(xf, 2, axis=-1)
    return jnp.concatenate([x1 * cos - x2 * sin, x2 * cos + x1 * sin], axis=-1).astype(x.dtype)


def mla(q_lat, kv_lat, k_pe, cos, sin, q_norm_g, w_uq, kv_norm_g, w_ukv):
    B, S, _ = q_lat.shape
    q = (rms_norm(q_lat, q_norm_g) @ w_uq).reshape(B, S, N_HEADS, QK_NOPE_DIM + QK_ROPE_DIM)
    q_nope = q[..., :QK_NOPE_DIM]
    q_pe = apply_rope(q[..., QK_NOPE_DIM:], cos[:, :, None, :], sin[:, :, None, :])
    kv = (rms_norm(kv_lat, kv_norm_g) @ w_ukv).reshape(B, S, N_HEADS, QK_NOPE_DIM + V_HEAD_DIM)
    k_nope = kv[..., :QK_NOPE_DIM]
    v = kv[..., QK_NOPE_DIM:]
    k_pe = apply_rope(k_pe, cos, sin)
    scale = (QK_NOPE_DIM + QK_ROPE_DIM) ** -0.5
    outs = []
    for i in range(S // Q_BLOCK):
        q0 = i * Q_BLOCK
        k_end = q0 + Q_BLOCK
        s = (jnp.einsum('bqhd,bkhd->bhqk', q_nope[:, q0:k_end], k_nope[:, :k_end])
             + jnp.einsum('bqhd,bkd->bhqk', q_pe[:, q0:k_end], k_pe[:, :k_end]))
        s = s.astype(jnp.float32) * scale
        causal = (q0 + jnp.arange(Q_BLOCK))[:, None] >= jnp.arange(k_end)[None, :]
        s = jnp.where(causal, s, jnp.finfo(jnp.float32).min)
        p = jax.nn.softmax(s, axis=-1).astype(v.dtype)
        outs.append(jnp.einsum('bhqk,bkhd->bqhd', p, v[:, :k_end]))
    return jnp.concatenate(outs, axis=1).reshape(B, S, N_HEADS * V_HEAD_DIM)


def sgu(uv, norm_g, w_s, b_s):
    B, S, _ = uv.shape
    uv = jax.nn.gelu(uv)
    u, v = uv[..., :SGU_WIDTH], uv[..., SGU_WIDTH:]
    v = rms_norm(v, norm_g).reshape(B, S // CHUNK, CHUNK, SGU_GROUPS, SGU_GROUP_DIM)
    tril = jnp.tril(jnp.ones((CHUNK, CHUNK), dtype=bool))
    ws = jnp.where(tril[None], w_s, jnp.zeros_like(w_s))
    mixed = jnp.einsum('gts,bnsgd->bntgd', ws, v) + b_s.T[None, None, :, :, None]
    return u * mixed.reshape(B, S, SGU_WIDTH)


def setup_inputs(seed: int = 0) -> dict:
    key = jax.random.key(seed)
    ks = jax.random.split(key, 24)
    f32 = jnp.float32

    def w(k, shape, fan_in):
        return jax.random.normal(k, shape, f32) * fan_in ** -0.5

    def gain(k, shape):
        return 1.0 + 0.01 * jax.random.normal(k, shape, f32)

    L = DEPTH
    x = jax.random.normal(ks[0], (BATCH, SEQ, D_MODEL), f32)
    offsets = jax.random.randint(ks[1], (BATCH, 1), 0, 1024, dtype=jnp.int32)
    positions = (jnp.arange(SEQ, dtype=jnp.int32)[None, :] + offsets).astype(jnp.int32)
    return {
        "x": x,
        "positions": positions,
        "norm_mix_g": gain(ks[2], (L, D_MODEL)),
        "w_in": w(ks[3], (L, D_MODEL, D_IN), D_MODEL),
        "b_gate": 0.01 * jax.random.normal(ks[4], (L, N_BRANCH * D_MODEL), f32),
        "q_norm_g": gain(ks[5], (L, Q_LORA_RANK)),
        "w_uq": w(ks[6], (L, Q_LORA_RANK, N_HEADS * (QK_NOPE_DIM + QK_ROPE_DIM)), Q_LORA_RANK),
        "kv_norm_g": gain(ks[7], (L, KV_LORA_RANK)),
        "w_ukv": w(ks[8], (L, KV_LORA_RANK, N_HEADS * (QK_NOPE_DIM + V_HEAD_DIM)), KV_LORA_RANK),
        "w_o_attn": w(ks[9], (L, N_HEADS * V_HEAD_DIM, D_MODEL), N_HEADS * V_HEAD_DIM),
        "sgu_norm_g": gain(ks[10], (L, SGU_WIDTH)),
        "w_sgu": w(ks[11], (L, SGU_GROUPS, CHUNK, CHUNK), CHUNK),
        "b_sgu": gain(ks[12], (L, SGU_GROUPS, CHUNK)),
        "w_o_sgu": w(ks[13], (L, SGU_WIDTH, D_MODEL), SGU_WIDTH),
        "w_out": w(ks[14], (L, D_MODEL, D_MODEL), D_MODEL),
        "norm_ffn_g": gain(ks[15], (L, D_MODEL)),
        "w_gate_ffn": w(ks[16], (L, D_MODEL, D_FF), D_MODEL),
        "w_up_ffn": w(ks[17], (L, D_MODEL, D_FF), D_MODEL),
        "w_down_ffn": w(ks[18], (L, D_FF, D_MODEL), D_FF),
        "norm_final_g": gain(ks[19], (D_MODEL,)),
    }


def reference(x, positions, norm_mix_g, w_in, b_gate, q_norm_g, w_uq, kv_norm_g, w_ukv, w_o_attn,
              sgu_norm_g, w_sgu, b_sgu, w_o_sgu, w_out, norm_ffn_g, w_gate_ffn, w_up_ffn,
              w_down_ffn, norm_final_g):
    B, S, D = x.shape
    cos, sin = rope_tables(positions)
    o1 = Q_LORA_RANK
    o2 = o1 + KV_LORA_RANK
    o3 = o2 + QK_ROPE_DIM
    o4 = o3 + 2 * SGU_WIDTH
    h = x
    for l in range(DEPTH):
        a = rms_norm(h, norm_mix_g[l])
        z = a @ w_in[l]
        q_lat, kv_lat, k_pe = z[..., :o1], z[..., o1:o2], z[..., o2:o3]
        uv, gate_logits = z[..., o3:o4], z[..., o4:]
        y_attn = mla(q_lat, kv_lat, k_pe, cos, sin, q_norm_g[l], w_uq[l], kv_norm_g[l], w_ukv[l]) @ w_o_attn[l]
        y_sgu = sgu(uv, sgu_norm_g[l], w_sgu[l], b_sgu[l]) @ w_o_sgu[l]
        gates = jax.nn.sigmoid(gate_logits + b_gate[l]).reshape(B, S, N_BRANCH, D)
        merged = gates[:, :, 0, :] * y_attn + gates[:, :, 1, :] * y_sgu
        h = h + merged @ w_out[l]
        f = rms_norm(h, norm_ffn_g[l])
        h = h + (jax.nn.silu(f @ w_gate_ffn[l]) * (f @ w_up_ffn[l])) @ w_down_ffn[l]
    return rms_norm(h, norm_final_g)
```

```python
import functools

import jax
import jax.numpy as jnp
from jax import lax
from jax.experimental import pallas as pl
from jax.experimental.pallas import tpu as pltpu

D_MODEL = 2048
N_HEADS = 16
QK_NOPE_DIM = 128
QK_ROPE_DIM = 64
V_HEAD_DIM = 128
Q_LORA_RANK = 512
KV_LORA_RANK = 512
ROPE_THETA = 10000.0
SGU_GROUPS = 8
SGU_GROUP_DIM = 128
SGU_WIDTH = SGU_GROUPS * SGU_GROUP_DIM
CHUNK = 128
RMS_EPS = 1e-6
LANES = 128
HALF_ROPE = QK_ROPE_DIM // 2

VMEM_LIMIT_BYTES = 56 * 1024 * 1024

F32 = jnp.float32
BF16 = jnp.bfloat16


def _rms(x, g):
    return x * lax.rsqrt(jnp.mean(x * x, axis=-1, keepdims=True) + RMS_EPS) * g


def _dot(a, b):
    return jnp.dot(a, b, preferred_element_type=F32)


def _resident(shape):
    return pl.BlockSpec(shape, lambda *_: (0,) * len(shape), pipeline_mode=pl.Buffered(1))


def _params(n_axes):
    return pltpu.CompilerParams(
        dimension_semantics=("arbitrary",) * n_axes, vmem_limit_bytes=VMEM_LIMIT_BYTES)


def _rope_table_kernel(pos_ref, freq_ref, cos_ref, sin_ref):
    ang = pos_ref[...] * freq_ref[...]
    cos_ref[...] = jnp.cos(ang)
    sin_ref[...] = jnp.sin(ang)


def _rope_tables(positions):
    n_tok = positions.size
    per_row = LANES // HALF_ROPE
    inv_freq = ROPE_THETA ** (-jnp.arange(0, QK_ROPE_DIM, 2, dtype=F32) / QK_ROPE_DIM)
    pos_rep = jnp.repeat(positions.astype(F32).reshape(n_tok // per_row, per_row), HALF_ROPE, axis=1)
    freq = jnp.tile(inv_freq, per_row).reshape(1, LANES)
    shape = jax.ShapeDtypeStruct((n_tok // per_row, LANES), F32)
    cos, sin = pl.pallas_call(
        _rope_table_kernel, out_shape=(shape, shape), name="rope_tables")(pos_rep, freq)
    cos = cos.reshape(n_tok, HALF_ROPE)
    sin = sin.reshape(n_tok, HALF_ROPE)
    return jnp.concatenate([cos, cos, -sin, sin], axis=-1)


def _rope_dup(x, cs):
    y = x * cs
    return y + pltpu.roll(y, LANES // 2, 1)


def _inproj_kernel(x_ref, g_ref, cs_ref, wlat_ref, qg_ref, kvg_ref, wuq_ref, wukv_ref,
                   a_ref, q_ref, kn_ref, v_ref, kpe_ref):
    a = _rms(x_ref[0], g_ref[...]).astype(BF16)
    a_ref[0] = a
    z = _dot(a, wlat_ref[...])
    qn = _rms(z[:, :Q_LORA_RANK], qg_ref[...]).astype(BF16)
    kvn = _rms(z[:, Q_LORA_RANK:Q_LORA_RANK + KV_LORA_RANK], kvg_ref[...]).astype(BF16)
    cs = cs_ref[0]
    kpe = _rope_dup(z[:, Q_LORA_RANK + KV_LORA_RANK:], cs)
    lane = lax.broadcasted_iota(jnp.int32, kpe.shape, 1)
    kpe_ref[0] = jnp.where(lane < QK_ROPE_DIM, kpe, 0.0).astype(BF16)

    heads_per_dot = 4
    width = heads_per_dot * LANES
    nope_cols = N_HEADS * QK_NOPE_DIM
    for hg in range(N_HEADS // heads_per_dot):
        c0 = hg * width
        q_nope = _dot(qn, wuq_ref[:, c0:c0 + width])
        q_pe = _dot(qn, wuq_ref[:, nope_cols + c0:nope_cols + c0 + width])
        k_nope = _dot(kvn, wukv_ref[:, c0:c0 + width])
        val = _dot(kvn, wukv_ref[:, nope_cols + c0:nope_cols + c0 + width])
        for hh in range(heads_per_dot):
            h = hg * heads_per_dot + hh
            sl = slice(hh * LANES, (hh + 1) * LANES)
            q_ref[0, h, :, 0:LANES] = q_nope[:, sl].astype(BF16)
            q_ref[0, h, :, LANES:2 * LANES] = _rope_dup(q_pe[:, sl], cs).astype(BF16)
            kn_ref[0, h] = k_nope[:, sl].astype(BF16)
            v_ref[0, h] = val[:, sl].astype(BF16)


def _inproj(x, norm_g, cs, w_lat, q_g, kv_g, w_uq, w_ukv, tm):
    B, S, D = x.shape
    lat = w_lat.shape[1]
    row = lambda b, i: (b, i, 0)
    head = lambda b, i: (b, 0, i, 0)
    return pl.pallas_call(
        _inproj_kernel,
        grid=(B, S // tm),
        in_specs=[
            pl.BlockSpec((1, tm, D), row),
            _resident((1, D)),
            pl.BlockSpec((1, tm, LANES), row),
            _resident((D, lat)),
            _resident((1, Q_LORA_RANK)),
            _resident((1, KV_LORA_RANK)),
            _resident(w_uq.shape),
            _resident(w_ukv.shape),
        ],
        out_specs=[
            pl.BlockSpec((1, tm, D), row),
            pl.BlockSpec((1, N_HEADS, tm, 2 * LANES), head),
            pl.BlockSpec((1, N_HEADS, tm, LANES), head),
            pl.BlockSpec((1, N_HEADS, tm, LANES), head),
            pl.BlockSpec((1, tm, LANES), row),
        ],
        out_shape=[
            jax.ShapeDtypeStruct((B, S, D), BF16),
            jax.ShapeDtypeStruct((B, N_HEADS, S, 2 * LANES), BF16),
            jax.ShapeDtypeStruct((B, N_HEADS, S, LANES), BF16),
            jax.ShapeDtypeStruct((B, N_HEADS, S, LANES), BF16),
            jax.ShapeDtypeStruct((B, S, LANES), BF16),
        ],
        compiler_params=_params(2),
        name="inproj",
    )(x, norm_g, cs, w_lat, q_g, kv_g, w_uq, w_ukv)


def _sgu_kernel(a_ref, wuv_ref, sg_ref, ws_ref, bfull_ref, wos_ref, wg1_ref, bg1_ref, m_ref):
    a = a_ref[...]
    tm = a.shape[0]
    n_chunks = tm // CHUNK
    uv = jax.nn.gelu(_dot(a, wuv_ref[...]))
    u = uv[:, :SGU_WIDTH]
    vn = _rms(uv[:, SGU_WIDTH:], sg_ref[...]).astype(BF16)
    t_idx = lax.broadcasted_iota(jnp.int32, (CHUNK, CHUNK), 0)
    s_idx = lax.broadcasted_iota(jnp.int32, (CHUNK, CHUNK), 1)
    causal = t_idx >= s_idx
    mixed_cols = []
    for g in range(SGU_GROUPS):
        ws = jnp.where(causal, ws_ref[g], 0.0).astype(BF16)
        gs = slice(g * SGU_GROUP_DIM, (g + 1) * SGU_GROUP_DIM)
        rhs = jnp.concatenate([vn[c * CHUNK:(c + 1) * CHUNK, gs] for c in range(n_chunks)], axis=1)
        mixed_cols.append(_dot(ws, rhs))
    bfull = bfull_ref[...]
    rows = []
    for c in range(n_chunks):
        cs = slice(c * SGU_GROUP_DIM, (c + 1) * SGU_GROUP_DIM)
        mixed = jnp.concatenate([mixed_cols[g][:, cs] for g in range(SGU_GROUPS)], axis=1)
        rows.append(u[c * CHUNK:(c + 1) * CHUNK] * (mixed + bfull))
    sgu_out = jnp.concatenate(rows, axis=0).astype(BF16)
    y_sgu = _dot(sgu_out, wos_ref[...])
    gate = jax.nn.sigmoid(_dot(a, wg1_ref[...]) + bg1_ref[...])
    m_ref[...] = gate * y_sgu


def _sgu_branch(a, w_uv, sgu_g, w_s, b_full, w_o_sgu, w_g1, b_g1, tm):
    T, D = a.shape
    row = lambda i: (i, 0)
    return pl.pallas_call(
        _sgu_kernel,
        grid=(T // tm,),
        in_specs=[
            pl.BlockSpec((tm, D), row),
            _resident(w_uv.shape),
            _resident(sgu_g.shape),
            _resident(w_s.shape),
            _resident(b_full.shape),
            _resident(w_o_sgu.shape),
            _resident(w_g1.shape),
            _resident(b_g1.shape),
        ],
        out_specs=pl.BlockSpec((tm, D), row),
        out_shape=jax.ShapeDtypeStruct((T, D), F32),
        compiler_params=_params(1),
        name="sgu_branch",
    )(a, w_uv, sgu_g, w_s, b_full, w_o_sgu, w_g1, b_g1)


def _attn_kernel(q_ref, kn_ref, kpe_ref, v_ref, o_ref, *, tq, scale):
    qi = pl.program_id(2)
    heads = q_ref.shape[1]
    row = lax.broadcasted_iota(jnp.int32, (tq, tq), 0)
    col = lax.broadcasted_iota(jnp.int32, (tq, tq), 1)
    causal = row >= col
    neg = jnp.finfo(F32).min

    for hh in range(heads):
        q = q_ref[0, hh]

        def step(j, carry, masked, hh=hh, q=q):
            m, l, acc = carry
            ks = pl.ds(pl.multiple_of(j * tq, tq), tq)
            k = jnp.concatenate([kn_ref[0, hh, ks, :], kpe_ref[0, ks, :]], axis=1)
            s = lax.dot_general(q, k, (((1,), (1,)), ((), ())), preferred_element_type=F32) * scale
            if masked:
                s = jnp.where(causal, s, neg)
            m_new = jnp.maximum(m, jnp.max(s, axis=-1, keepdims=True))
            alpha = jnp.exp(m - m_new)
            p = jnp.exp(s - m_new)
            l = alpha * l + jnp.sum(p, axis=-1, keepdims=True)
            acc = alpha * acc + _dot(p.astype(BF16), v_ref[0, hh, ks, :])
            return m_new, l, acc

        init = (jnp.full((tq, 1), neg, F32), jnp.zeros((tq, 1), F32), jnp.zeros((tq, LANES), F32))
        carry = lax.fori_loop(0, qi, functools.partial(step, masked=False), init)
        m, l, acc = step(qi, carry, masked=True)
        o_ref[0, :, hh * LANES:(hh + 1) * LANES] = (acc / l).astype(o_ref.dtype)


def _attention(q, k_nope, k_pe, v, tq, heads_per_step):
    B, H, S, _ = q.shape
    scale = (QK_NOPE_DIM + QK_ROPE_DIM) ** -0.5
    hb = heads_per_step
    return pl.pallas_call(
        functools.partial(_attn_kernel, tq=tq, scale=scale),
        grid=(B, H // hb, S // tq),
        in_specs=[
            pl.BlockSpec((1, hb, tq, 2 * LANES), lambda b, g, i: (b, g, i, 0)),
            pl.BlockSpec((1, hb, S, LANES), lambda b, g, i: (b, g, 0, 0)),
            pl.BlockSpec((1, S, LANES), lambda b, g, i: (b, 0, 0)),
            pl.BlockSpec((1, hb, S, LANES), lambda b, g, i: (b, g, 0, 0)),
        ],
        out_specs=pl.BlockSpec((1, tq, hb * LANES), lambda b, g, i: (b, i, g)),
        out_shape=jax.ShapeDtypeStruct((B, S, H * V_HEAD_DIM), BF16),
        compiler_params=_params(3),
        name="mla_attention",
    )(q, k_nope, k_pe, v)


def _merge_kernel(attn_ref, a_ref, m_ref, x_ref, woa_ref, wg0_ref, bg0_ref, wout_ref, fg_ref,
                  h_ref, f_ref):
    y_attn = _dot(attn_ref[...], woa_ref[...])
    gate = jax.nn.sigmoid(_dot(a_ref[...], wg0_ref[...]) + bg0_ref[...])
    merged = (gate * y_attn + m_ref[...]).astype(BF16)
    h = x_ref[...] + _dot(merged, wout_ref[...])
    h_ref[...] = h
    f_ref[...] = _rms(h, fg_ref[...]).astype(BF16)


def _merge(attn, a, m_sgu, x, w_o_attn, w_g0, b_g0, w_out, ffn_g, tm):
    T, D = x.shape
    row = lambda i: (i, 0)
    tile = pl.BlockSpec((tm, D), row)
    return pl.pallas_call(
        _merge_kernel,
        grid=(T // tm,),
        in_specs=[tile, tile, tile, tile,
                  _resident(w_o_attn.shape), _resident(w_g0.shape), _resident(b_g0.shape),
                  _resident(w_out.shape), _resident(ffn_g.shape)],
        out_specs=[tile, tile],
        out_shape=[jax.ShapeDtypeStruct((T, D), F32), jax.ShapeDtypeStruct((T, D), BF16)],
        compiler_params=_params(1),
        name="merge_outproj",
    )(attn, a, m_sgu, x, w_o_attn, w_g0, b_g0, w_out, ffn_g)


def _ffn_kernel(f_ref, h_ref, wg_ref, wu_ref, wd_ref, ng_ref, o_ref, acc_ref):
    j = pl.program_id(1)

    @pl.when(j == 0)
    def _():
        acc_ref[...] = jnp.zeros_like(acc_ref)

    f = f_ref[...]
    gate = _dot(f, wg_ref[...])
    up = _dot(f, wu_ref[...])
    act = (jax.nn.silu(gate) * up).astype(BF16)
    acc_ref[...] += _dot(act, wd_ref[...])

    @pl.when(j == pl.num_programs(1) - 1)
    def _():
        o_ref[...] = _rms(h_ref[...] + acc_ref[...], ng_ref[...])


def _ffn(f, h, w_gate, w_up, w_down, final_g, tm, tf):
    T, D = h.shape
    d_ff = w_gate.shape[1]
    row = lambda i, j: (i, 0)
    return pl.pallas_call(
        _ffn_kernel,
        grid=(T // tm, d_ff // tf),
        in_specs=[
            pl.BlockSpec((tm, D), row),
            pl.BlockSpec((tm, D), row),
            pl.BlockSpec((D, tf), lambda i, j: (0, j)),
            pl.BlockSpec((D, tf), lambda i, j: (0, j)),
            pl.BlockSpec((tf, D), lambda i, j: (j, 0)),
            pl.BlockSpec((1, D), lambda i, j: (0, 0)),
        ],
        out_specs=pl.BlockSpec((tm, D), row),
        out_shape=jax.ShapeDtypeStruct((T, D), F32),
        scratch_shapes=[pltpu.VMEM((tm, D), F32)],
        compiler_params=_params(2),
        name="swiglu_ffn",
    )(f, h, w_gate, w_up, w_down, final_g)


def _dup_rope_cols(w):
    x1, x2 = w[..., :HALF_ROPE], w[..., HALF_ROPE:]
    return jnp.concatenate([x1, x2, x2, x1], axis=-1)


def kernel(x, positions, norm_mix_g, w_in, b_gate, q_norm_g, w_uq, kv_norm_g, w_ukv, w_o_attn,
           sgu_norm_g, w_sgu, b_sgu, w_o_sgu, w_out, norm_ffn_g, w_gate_ffn, w_up_ffn,
           w_down_ffn, norm_final_g):
    B, S, D = x.shape
    T = B * S
    depth = w_in.shape[0]
    assert depth == 1, "the final norm is fused into the FFN epilogue of a single layer"
    o1 = Q_LORA_RANK
    o2 = o1 + KV_LORA_RANK
    o3 = o2 + QK_ROPE_DIM
    o4 = o3 + 2 * SGU_WIDTH

    cs = _rope_tables(positions).reshape(B, S, LANES)
    row_vec = lambda v: v.reshape(1, -1).astype(F32)

    h = x
    out = None
    for l in range(depth):
        wi = w_in[l]
        w_lat = jnp.concatenate([wi[:, :o2], _dup_rope_cols(wi[:, o2:o3])], axis=1).astype(BF16)
        w_uv = wi[:, o3:o4].astype(BF16)
        w_g0 = wi[:, o4:o4 + D].astype(BF16)
        w_g1 = wi[:, o4 + D:].astype(BF16)
        uq = w_uq[l].reshape(Q_LORA_RANK, N_HEADS, QK_NOPE_DIM + QK_ROPE_DIM)
        w_uq_p = jnp.concatenate(
            [uq[..., :QK_NOPE_DIM].reshape(Q_LORA_RANK, -1),
             _dup_rope_cols(uq[..., QK_NOPE_DIM:]).reshape(Q_LORA_RANK, -1)], axis=1).astype(BF16)
        ukv = w_ukv[l].reshape(KV_LORA_RANK, N_HEADS, QK_NOPE_DIM + V_HEAD_DIM)
        w_ukv_p = jnp.concatenate(
            [ukv[..., :QK_NOPE_DIM].reshape(KV_LORA_RANK, -1),
             ukv[..., QK_NOPE_DIM:].reshape(KV_LORA_RANK, -1)], axis=1).astype(BF16)
        b_full = jnp.repeat(b_sgu[l].T, SGU_GROUP_DIM, axis=1).astype(F32)

        a, q, k_nope, v, k_pe = _inproj(
            h, row_vec(norm_mix_g[l]), cs, w_lat, row_vec(q_norm_g[l]), row_vec(kv_norm_g[l]),
            w_uq_p, w_ukv_p, tm=256)
        a2 = a.reshape(T, D)
        m_sgu = _sgu_branch(
            a2, w_uv, row_vec(sgu_norm_g[l]), w_sgu[l], b_full, w_o_sgu[l].astype(BF16),
            w_g1, row_vec(b_gate[l, D:]), tm=256)
        attn = _attention(q, k_nope, k_pe, v, tq=256, heads_per_step=4)
        h_mid, f = _merge(
            attn.reshape(T, D), a2, m_sgu, h.reshape(T, D), w_o_attn[l].astype(BF16), w_g0,
            row_vec(b_gate[l, :D]), w_out[l].astype(BF16), row_vec(norm_ffn_g[l]), tm=256)
        out = _ffn(f, h_mid, w_gate_ffn[l].astype(BF16), w_up_ffn[l].astype(BF16),
                   w_down_ffn[l].astype(BF16), row_vec(norm_final_g), tm=512, tf=512)
        h = out.reshape(B, S, D)
    return h
```

```python
import functools

import jax
import jax.numpy as jnp
from jax import lax
from jax.experimental import pallas as pl
from jax.experimental.pallas import tpu as pltpu

D_MODEL = 2048
N_HEADS = 16
QK_NOPE_DIM = 128
QK_ROPE_DIM = 64
V_HEAD_DIM = 128
Q_LORA_RANK = 512
KV_LORA_RANK = 512
ROPE_THETA = 10000.0
SGU_GROUPS = 8
SGU_GROUP_DIM = 128
SGU_WIDTH = SGU_GROUPS * SGU_GROUP_DIM
CHUNK = 128
RMS_EPS = 1e-6
LANES = 128
HALF_ROPE = QK_ROPE_DIM // 2

VMEM_LIMIT_BYTES = 56 * 1024 * 1024

F32 = jnp.float32
BF16 = jnp.bfloat16


def _rms(x, g):
    return x * lax.rsqrt(jnp.mean(x * x, axis=-1, keepdims=True) + RMS_EPS) * g


def _dot(a, b):
    return jnp.dot(a, b, preferred_element_type=F32)


def _resident(shape):
    return pl.BlockSpec(shape, lambda *_: (0,) * len(shape), pipeline_mode=pl.Buffered(1))


def _params(n_axes):
    return pltpu.CompilerParams(
        dimension_semantics=("arbitrary",) * n_axes, vmem_limit_bytes=VMEM_LIMIT_BYTES)


def _rope_table_kernel(pos_ref, freq_ref, cos_ref, sin_ref):
    ang = pos_ref[...] * freq_ref[...]
    cos_ref[...] = jnp.cos(ang)
    sin_ref[...] = jnp.sin(ang)


def _rope_tables(positions):
    n_tok = positions.size
    per_row = LANES // HALF_ROPE
    inv_freq = ROPE_THETA ** (-jnp.arange(0, QK_ROPE_DIM, 2, dtype=F32) / QK_ROPE_DIM)
    pos_rep = jnp.repeat(positions.astype(F32).reshape(n_tok // per_row, per_row), HALF_ROPE, axis=1)
    freq = jnp.tile(inv_freq, per_row).reshape(1, LANES)
    shape = jax.ShapeDtypeStruct((n_tok // per_row, LANES), F32)
    cos, sin = pl.pallas_call(
        _rope_table_kernel, out_shape=(shape, shape), name="rope_tables")(pos_rep, freq)
    cos = cos.reshape(n_tok, HALF_ROPE)
    sin = sin.reshape(n_tok, HALF_ROPE)
    return jnp.concatenate([cos, cos, -sin, sin], axis=-1)


def _rope_dup(x, cs):
    y = x * cs
    return y + pltpu.roll(y, LANES // 2, 1)


def _inproj_kernel(x_ref, g_ref, cs_ref, wlat_ref, qg_ref, kvg_ref, wuq_ref, wukv_ref,
                   a_ref, q_ref, kn_ref, v_ref, kpe_ref):
    a = _rms(x_ref[0], g_ref[...]).astype(BF16)
    a_ref[0] = a
    z = _dot(a, wlat_ref[...])
    qn = _rms(z[:, :Q_LORA_RANK], qg_ref[...]).astype(BF16)
    kvn = _rms(z[:, Q_LORA_RANK:Q_LORA_RANK + KV_LORA_RANK], kvg_ref[...]).astype(BF16)
    cs = cs_ref[0]
    kpe = _rope_dup(z[:, Q_LORA_RANK + KV_LORA_RANK:], cs)
    lane = lax.broadcasted_iota(jnp.int32, kpe.shape, 1)
    kpe_ref[0] = jnp.where(lane < QK_ROPE_DIM, kpe, 0.0).astype(BF16)

    heads_per_dot = 4
    width = heads_per_dot * LANES
    nope_cols = N_HEADS * QK_NOPE_DIM
    for hg in range(N_HEADS // heads_per_dot):
        c0 = hg * width
        q_nope = _dot(qn, wuq_ref[:, c0:c0 + width])
        q_pe = _dot(qn, wuq_ref[:, nope_cols + c0:nope_cols + c0 + width])
        k_nope = _dot(kvn, wukv_ref[:, c0:c0 + width])
        val = _dot(kvn, wukv_ref[:, nope_cols + c0:nope_cols + c0 + width])
        for hh in range(heads_per_dot):
            h = hg * heads_per_dot + hh
            sl = slice(hh * LANES, (hh + 1) * LANES)
            q_ref[0, h, :, 0:LANES] = q_nope[:, sl].astype(BF16)
            q_ref[0, h, :, LANES:2 * LANES] = _rope_dup(q_pe[:, sl], cs).astype(BF16)
            kn_ref[0, h] = k_nope[:, sl].astype(BF16)
            v_ref[0, h] = val[:, sl].astype(BF16)


def _inproj(x, norm_g, cs, w_lat, q_g, kv_g, w_uq, w_ukv, tm):
    B, S, D = x.shape
    lat = w_lat.shape[1]
    row = lambda b, i: (b, i, 0)
    head = lambda b, i: (b, 0, i, 0)
    return pl.pallas_call(
        _inproj_kernel,
        grid=(B, S // tm),
        in_specs=[
            pl.BlockSpec((1, tm, D), row),
            _resident((1, D)),
            pl.BlockSpec((1, tm, LANES), row),
            _resident((D, lat)),
            _resident((1, Q_LORA_RANK)),
            _resident((1, KV_LORA_RANK)),
            _resident(w_uq.shape),
            _resident(w_ukv.shape),
        ],
        out_specs=[
            pl.BlockSpec((1, tm, D), row),
            pl.BlockSpec((1, N_HEADS, tm, 2 * LANES), head),
            pl.BlockSpec((1, N_HEADS, tm, LANES), head),
            pl.BlockSpec((1, N_HEADS, tm, LANES), head),
            pl.BlockSpec((1, tm, LANES), row),
        ],
        out_shape=[
            jax.ShapeDtypeStruct((B, S, D), BF16),
            jax.ShapeDtypeStruct((B, N_HEADS, S, 2 * LANES), BF16),
            jax.ShapeDtypeStruct((B, N_HEADS, S, LANES), BF16),
            jax.ShapeDtypeStruct((B, N_HEADS, S, LANES), BF16),
            jax.ShapeDtypeStruct((B, S, LANES), BF16),
        ],
        compiler_params=_params(2),
        name="inproj",
    )(x, norm_g, cs, w_lat, q_g, kv_g, w_uq, w_ukv)


def _sgu_kernel(a_ref, wuv_ref, sg_ref, ws_ref, bfull_ref, wos_ref, wg1_ref, bg1_ref, m_ref):
    a = a_ref[...]
    tm = a.shape[0]
    n_chunks = tm // CHUNK
    uv = jax.nn.gelu(_dot(a, wuv_ref[...]))
    u = uv[:, :SGU_WIDTH]
    vn = _rms(uv[:, SGU_WIDTH:], sg_ref[...]).astype(BF16)
    t_idx = lax.broadcasted_iota(jnp.int32, (CHUNK, CHUNK), 0)
    s_idx = lax.broadcasted_iota(jnp.int32, (CHUNK, CHUNK), 1)
    causal = t_idx >= s_idx
    mixed_cols = []
    for g in range(SGU_GROUPS):
        ws = jnp.where(causal, ws_ref[g], 0.0).astype(BF16)
        gs = slice(g * SGU_GROUP_DIM, (g + 1) * SGU_GROUP_DIM)
        rhs = jnp.concatenate([vn[c * CHUNK:(c + 1) * CHUNK, gs] for c in range(n_chunks)], axis=1)
        mixed_cols.append(_dot(ws, rhs))
    bfull = bfull_ref[...]
    rows = []
    for c in range(n_chunks):
        cs = slice(c * SGU_GROUP_DIM, (c + 1) * SGU_GROUP_DIM)
        mixed = jnp.concatenate([mixed_cols[g][:, cs] for g in range(SGU_GROUPS)], axis=1)
        rows.append(u[c * CHUNK:(c + 1) * CHUNK] * (mixed + bfull))
    sgu_out = jnp.concatenate(rows, axis=0).astype(BF16)
    y_sgu = _dot(sgu_out, wos_ref[...])
    gate = jax.nn.sigmoid(_dot(a, wg1_ref[...]) + bg1_ref[...])
    m_ref[...] = gate * y_sgu


def _sgu_branch(a, w_uv, sgu_g, w_s, b_full, w_o_sgu, w_g1, b_g1, tm):
    T, D = a.shape
    row = lambda i: (i, 0)
    return pl.pallas_call(
        _sgu_kernel,
        grid=(T // tm,),
        in_specs=[
            pl.BlockSpec((tm, D), row),
            _resident(w_uv.shape),
            _resident(sgu_g.shape),
            _resident(w_s.shape),
            _resident(b_full.shape),
            _resident(w_o_sgu.shape),
            _resident(w_g1.shape),
            _resident(b_g1.shape),
        ],
        out_specs=pl.BlockSpec((tm, D), row),
        out_shape=jax.ShapeDtypeStruct((T, D), F32),
        compiler_params=_params(1),
        name="sgu_branch",
    )(a, w_uv, sgu_g, w_s, b_full, w_o_sgu, w_g1, b_g1)


def _attn_kernel(q_ref, kn_ref, kpe_ref, v_ref, o_ref, kf_ref, *, tq, scale):
    seq = q_ref.shape[2]
    row = lax.broadcasted_iota(jnp.int32, (tq, tq), 0)
    col = lax.broadcasted_iota(jnp.int32, (tq, tq), 1)
    causal = row >= col
    neg = jnp.finfo(F32).min
    kf_ref[:, 0:LANES] = kn_ref[0, 0]
    kf_ref[:, LANES:2 * LANES] = kpe_ref[0]

    for qi in range(seq // tq):
        q0 = qi * tq
        k_len = q0 + tq
        q = q_ref[0, 0, q0:k_len, :]
        s = lax.dot_general(q, kf_ref[0:k_len, :], (((1,), (1,)), ((), ())),
                            preferred_element_type=F32) * scale
        s_diag = jnp.where(causal, s[:, q0:k_len], neg)
        s = jnp.concatenate([s[:, 0:q0], s_diag], axis=1) if qi else s_diag
        m = jnp.max(s, axis=-1, keepdims=True)
        p = jnp.exp(s - m)
        l = jnp.sum(p, axis=-1, keepdims=True)
        acc = _dot(p.astype(BF16), v_ref[0, 0, 0:k_len, :])
        o_ref[0, q0:k_len, :] = (acc / l).astype(o_ref.dtype)


def _attention(q, k_nope, k_pe, v, tq):
    B, H, S, _ = q.shape
    scale = (QK_NOPE_DIM + QK_ROPE_DIM) ** -0.5
    return pl.pallas_call(
        functools.partial(_attn_kernel, tq=tq, scale=scale),
        grid=(B, H),
        in_specs=[
            pl.BlockSpec((1, 1, S, 2 * LANES), lambda b, h: (b, h, 0, 0)),
            pl.BlockSpec((1, 1, S, LANES), lambda b, h: (b, h, 0, 0)),
            pl.BlockSpec((1, S, LANES), lambda b, h: (b, 0, 0)),
            pl.BlockSpec((1, 1, S, LANES), lambda b, h: (b, h, 0, 0)),
        ],
        out_specs=pl.BlockSpec((1, S, LANES), lambda b, h: (b, 0, h)),
        out_shape=jax.ShapeDtypeStruct((B, S, H * V_HEAD_DIM), BF16),
        scratch_shapes=[pltpu.VMEM((S, 2 * LANES), BF16)],
        compiler_params=_params(2),
        name="mla_attention",
    )(q, k_nope, k_pe, v)


def _merge_kernel(attn_ref, a_ref, m_ref, x_ref, woa_ref, wg0_ref, bg0_ref, wout_ref, fg_ref,
                  h_ref, f_ref):
    y_attn = _dot(attn_ref[...], woa_ref[...])
    gate = jax.nn.sigmoid(_dot(a_ref[...], wg0_ref[...]) + bg0_ref[...])
    merged = (gate * y_attn + m_ref[...]).astype(BF16)
    h = x_ref[...] + _dot(merged, wout_ref[...])
    h_ref[...] = h
    f_ref[...] = _rms(h, fg_ref[...]).astype(BF16)


def _merge(attn, a, m_sgu, x, w_o_attn, w_g0, b_g0, w_out, ffn_g, tm):
    T, D = x.shape
    row = lambda i: (i, 0)
    tile = pl.BlockSpec((tm, D), row)
    return pl.pallas_call(
        _merge_kernel,
        grid=(T // tm,),
        in_specs=[tile, tile, tile, tile,
                  _resident(w_o_attn.shape), _resident(w_g0.shape), _resident(b_g0.shape),
                  _resident(w_out.shape), _resident(ffn_g.shape)],
        out_specs=[tile, tile],
        out_shape=[jax.ShapeDtypeStruct((T, D), F32), jax.ShapeDtypeStruct((T, D), BF16)],
        compiler_params=_params(1),
        name="merge_outproj",
    )(attn, a, m_sgu, x, w_o_attn, w_g0, b_g0, w_out, ffn_g)


def _ffn_kernel(f_ref, h_ref, wg_ref, wu_ref, wd_ref, ng_ref, o_ref, acc_ref):
    j = pl.program_id(1)

    @pl.when(j == 0)
    def _():
        acc_ref[...] = jnp.zeros_like(acc_ref)

    f = f_ref[...]
    gate = _dot(f, wg_ref[...])
    up = _dot(f, wu_ref[...])
    act = (jax.nn.silu(gate) * up).astype(BF16)
    acc_ref[...] += _dot(act, wd_ref[...])

    @pl.when(j == pl.num_programs(1) - 1)
    def _():
        o_ref[...] = _rms(h_ref[...] + acc_ref[...], ng_ref[...])


def _ffn(f, h, w_gate, w_up, w_down, final_g, tm, tf):
    T, D = h.shape
    d_ff = w_gate.shape[1]
    row = lambda i, j: (i, 0)
    return pl.pallas_call(
        _ffn_kernel,
        grid=(T // tm, d_ff // tf),
        in_specs=[
            pl.BlockSpec((tm, D), row),
            pl.BlockSpec((tm, D), row),
            pl.BlockSpec((D, tf), lambda i, j: (0, j)),
            pl.BlockSpec((D, tf), lambda i, j: (0, j)),
            pl.BlockSpec((tf, D), lambda i, j: (j, 0)),
            pl.BlockSpec((1, D), lambda i, j: (0, 0)),
        ],
        out_specs=pl.BlockSpec((tm, D), row),
        out_shape=jax.ShapeDtypeStruct((T, D), F32),
        scratch_shapes=[pltpu.VMEM((tm, D), F32)],
        compiler_params=_params(2),
        name="swiglu_ffn",
    )(f, h, w_gate, w_up, w_down, final_g)


def _dup_rope_cols(w):
    x1, x2 = w[..., :HALF_ROPE], w[..., HALF_ROPE:]
    return jnp.concatenate([x1, x2, x2, x1], axis=-1)


def kernel(x, positions, norm_mix_g, w_in, b_gate, q_norm_g, w_uq, kv_norm_g, w_ukv, w_o_attn,
           sgu_norm_g, w_sgu, b_sgu, w_o_sgu, w_out, norm_ffn_g, w_gate_ffn, w_up_ffn,
           w_down_ffn, norm_final_g):
    B, S, D = x.shape
    T = B * S
    depth = w_in.shape[0]
    assert depth == 1, "the final norm is fused into the FFN epilogue of a single layer"
    o1 = Q_LORA_RANK
    o2 = o1 + KV_LORA_RANK
    o3 = o2 + QK_ROPE_DIM
    o4 = o3 + 2 * SGU_WIDTH

    cs = _rope_tables(positions).reshape(B, S, LANES)
    row_vec = lambda v: v.reshape(1, -1).astype(F32)

    h = x
    out = None
    for l in range(depth):
        wi = w_in[l]
        w_lat = jnp.concatenate([wi[:, :o2], _dup_rope_cols(wi[:, o2:o3])], axis=1).astype(BF16)
        w_uv = wi[:, o3:o4].astype(BF16)
        w_g0 = wi[:, o4:o4 + D].astype(BF16)
        w_g1 = wi[:, o4 + D:].astype(BF16)
        uq = w_uq[l].reshape(Q_LORA_RANK, N_HEADS, QK_NOPE_DIM + QK_ROPE_DIM)
        w_uq_p = jnp.concatenate(
            [uq[..., :QK_NOPE_DIM].reshape(Q_LORA_RANK, -1),
             _dup_rope_cols(uq[..., QK_NOPE_DIM:]).reshape(Q_LORA_RANK, -1)], axis=1).astype(BF16)
        ukv = w_ukv[l].reshape(KV_LORA_RANK, N_HEADS, QK_NOPE_DIM + V_HEAD_DIM)
        w_ukv_p = jnp.concatenate(
            [ukv[..., :QK_NOPE_DIM].reshape(KV_LORA_RANK, -1),
             ukv[..., QK_NOPE_DIM:].reshape(KV_LORA_RANK, -1)], axis=1).astype(BF16)
        b_full = jnp.repeat(b_sgu[l].T, SGU_GROUP_DIM, axis=1).astype(F32)

        a, q, k_nope, v, k_pe = _inproj(
            h, row_vec(norm_mix_g[l]), cs, w_lat, row_vec(q_norm_g[l]), row_vec(kv_norm_g[l]),
            w_uq_p, w_ukv_p, tm=256)
        a2 = a.reshape(T, D)
        m_sgu = _sgu_branch(
            a2, w_uv, row_vec(sgu_norm_g[l]), w_sgu[l], b_full, w_o_sgu[l].astype(BF16),
            w_g1, row_vec(b_gate[l, D:]), tm=256)
        attn = _attention(q, k_nope, k_pe, v, tq=256)
        h_mid, f = _merge(
            attn.reshape(T, D), a2, m_sgu, h.reshape(T, D), w_o_attn[l].astype(BF16), w_g0,
            row_vec(b_gate[l, :D]), w_out[l].astype(BF16), row_vec(norm_ffn_g[l]), tm=256)
        out = _ffn(f, h_mid, w_gate_ffn[l].astype(BF16), w_up_ffn[l].astype(BF16),
                   w_down_ffn[l].astype(BF16), row_vec(norm_final_g), tm=512, tf=512)
        h = out.reshape(B, S, D)
    return h
```

```python
import functools

import jax
import jax.numpy as jnp
from jax import lax
from jax.experimental import pallas as pl
from jax.experimental.pallas import tpu as pltpu

D_MODEL = 2048
N_HEADS = 16
QK_NOPE_DIM = 128
QK_ROPE_DIM = 64
V_HEAD_DIM = 128
Q_LORA_RANK = 512
KV_LORA_RANK = 512
ROPE_THETA = 10000.0
SGU_GROUPS = 8
SGU_GROUP_DIM = 128
SGU_WIDTH = SGU_GROUPS * SGU_GROUP_DIM
CHUNK = 128
RMS_EPS = 1e-6
LANES = 128
HALF_ROPE = QK_ROPE_DIM // 2
LOG2_E = 1.4426950408889634

VMEM_LIMIT_BYTES = 56 * 1024 * 1024

F32 = jnp.float32
BF16 = jnp.bfloat16


def _rms(x, g):
    return x * lax.rsqrt(jnp.mean(x * x, axis=-1, keepdims=True) + RMS_EPS) * g


def _dot(a, b):
    return jnp.dot(a, b, preferred_element_type=F32)


def _resident(shape):
    return pl.BlockSpec(shape, lambda *_: (0,) * len(shape), pipeline_mode=pl.Buffered(1))


def _params(n_axes):
    return pltpu.CompilerParams(
        dimension_semantics=("arbitrary",) * n_axes, vmem_limit_bytes=VMEM_LIMIT_BYTES)


def _rope_table_kernel(pos_ref, freq_ref, cos_ref, sin_ref):
    ang = pos_ref[...] * freq_ref[...]
    cos_ref[...] = jnp.cos(ang)
    sin_ref[...] = jnp.sin(ang)


def _rope_tables(positions):
    n_tok = positions.size
    per_row = LANES // HALF_ROPE
    inv_freq = ROPE_THETA ** (-jnp.arange(0, QK_ROPE_DIM, 2, dtype=F32) / QK_ROPE_DIM)
    pos_rep = jnp.repeat(positions.astype(F32).reshape(n_tok // per_row, per_row), HALF_ROPE, axis=1)
    freq = jnp.tile(inv_freq, per_row).reshape(1, LANES)
    shape = jax.ShapeDtypeStruct((n_tok // per_row, LANES), F32)
    cos, sin = pl.pallas_call(
        _rope_table_kernel, out_shape=(shape, shape), name="rope_tables")(pos_rep, freq)
    cos = cos.reshape(n_tok, HALF_ROPE)
    sin = sin.reshape(n_tok, HALF_ROPE)
    return jnp.concatenate([cos, cos, -sin, sin], axis=-1)


def _rope_dup(x, cs):
    y = x * cs
    return y + pltpu.roll(y, LANES // 2, 1)


def _inproj_kernel(x_ref, g_ref, cs_ref, wlat_ref, qg_ref, kvg_ref, wuq_ref, wukv_ref,
                   a_ref, q_ref, kn_ref, v_ref, kpe_ref):
    a = _rms(x_ref[0], g_ref[...]).astype(BF16)
    a_ref[0] = a
    z = _dot(a, wlat_ref[...])
    qn = _rms(z[:, :Q_LORA_RANK], qg_ref[...]).astype(BF16)
    kvn = _rms(z[:, Q_LORA_RANK:Q_LORA_RANK + KV_LORA_RANK], kvg_ref[...]).astype(BF16)
    cs = cs_ref[0]
    kpe = _rope_dup(z[:, Q_LORA_RANK + KV_LORA_RANK:], cs)
    lane = lax.broadcasted_iota(jnp.int32, kpe.shape, 1)
    kpe_ref[0] = jnp.where(lane < QK_ROPE_DIM, kpe, 0.0).astype(BF16)

    heads_per_dot = 4
    width = heads_per_dot * LANES
    nope_cols = N_HEADS * QK_NOPE_DIM
    for hg in range(N_HEADS // heads_per_dot):
        c0 = hg * width
        q_nope = _dot(qn, wuq_ref[:, c0:c0 + width])
        q_pe = _dot(qn, wuq_ref[:, nope_cols + c0:nope_cols + c0 + width])
        k_nope = _dot(kvn, wukv_ref[:, c0:c0 + width])
        val = _dot(kvn, wukv_ref[:, nope_cols + c0:nope_cols + c0 + width])
        for hh in range(heads_per_dot):
            h = hg * heads_per_dot + hh
            sl = slice(hh * LANES, (hh + 1) * LANES)
            q_ref[0, h, :, 0:LANES] = q_nope[:, sl].astype(BF16)
            q_ref[0, h, :, LANES:2 * LANES] = _rope_dup(q_pe[:, sl], cs).astype(BF16)
            kn_ref[0, h] = k_nope[:, sl].astype(BF16)
            v_ref[0, h] = val[:, sl].astype(BF16)


def _inproj(x, norm_g, cs, w_lat, q_g, kv_g, w_uq, w_ukv, tm):
    B, S, D = x.shape
    lat = w_lat.shape[1]
    row = lambda b, i: (b, i, 0)
    head = lambda b, i: (b, 0, i, 0)
    return pl.pallas_call(
        _inproj_kernel,
        grid=(B, S // tm),
        in_specs=[
            pl.BlockSpec((1, tm, D), row),
            _resident((1, D)),
            pl.BlockSpec((1, tm, LANES), row),
            _resident((D, lat)),
            _resident((1, Q_LORA_RANK)),
            _resident((1, KV_LORA_RANK)),
            _resident(w_uq.shape),
            _resident(w_ukv.shape),
        ],
        out_specs=[
            pl.BlockSpec((1, tm, D), row),
            pl.BlockSpec((1, N_HEADS, tm, 2 * LANES), head),
            pl.BlockSpec((1, N_HEADS, tm, LANES), head),
            pl.BlockSpec((1, N_HEADS, tm, LANES), head),
            pl.BlockSpec((1, tm, LANES), row),
        ],
        out_shape=[
            jax.ShapeDtypeStruct((B, S, D), BF16),
            jax.ShapeDtypeStruct((B, N_HEADS, S, 2 * LANES), BF16),
            jax.ShapeDtypeStruct((B, N_HEADS, S, LANES), BF16),
            jax.ShapeDtypeStruct((B, N_HEADS, S, LANES), BF16),
            jax.ShapeDtypeStruct((B, S, LANES), BF16),
        ],
        compiler_params=_params(2),
        name="inproj",
    )(x, norm_g, cs, w_lat, q_g, kv_g, w_uq, w_ukv)


def _sgu_kernel(a_ref, wuv_ref, sg_ref, ws_ref, bfull_ref, wos_ref, wg1_ref, bg1_ref, m_ref):
    a = a_ref[...]
    tm = a.shape[0]
    n_chunks = tm // CHUNK
    uv = jax.nn.gelu(_dot(a, wuv_ref[...]))
    u = uv[:, :SGU_WIDTH]
    vn = _rms(uv[:, SGU_WIDTH:], sg_ref[...]).astype(BF16)
    t_idx = lax.broadcasted_iota(jnp.int32, (CHUNK, CHUNK), 0)
    s_idx = lax.broadcasted_iota(jnp.int32, (CHUNK, CHUNK), 1)
    causal = t_idx >= s_idx
    mixed_cols = []
    for g in range(SGU_GROUPS):
        ws = jnp.where(causal, ws_ref[g], 0.0).astype(BF16)
        gs = slice(g * SGU_GROUP_DIM, (g + 1) * SGU_GROUP_DIM)
        rhs = jnp.concatenate([vn[c * CHUNK:(c + 1) * CHUNK, gs] for c in range(n_chunks)], axis=1)
        mixed_cols.append(_dot(ws, rhs))
    bfull = bfull_ref[...]
    rows = []
    for c in range(n_chunks):
        cs = slice(c * SGU_GROUP_DIM, (c + 1) * SGU_GROUP_DIM)
        mixed = jnp.concatenate([mixed_cols[g][:, cs] for g in range(SGU_GROUPS)], axis=1)
        rows.append(u[c * CHUNK:(c + 1) * CHUNK] * (mixed + bfull))
    sgu_out = jnp.concatenate(rows, axis=0).astype(BF16)
    y_sgu = _dot(sgu_out, wos_ref[...])
    gate = jax.nn.sigmoid(_dot(a, wg1_ref[...]) + bg1_ref[...])
    m_ref[...] = gate * y_sgu


def _sgu_branch(a, w_uv, sgu_g, w_s, b_full, w_o_sgu, w_g1, b_g1, tm):
    T, D = a.shape
    row = lambda i: (i, 0)
    return pl.pallas_call(
        _sgu_kernel,
        grid=(T // tm,),
        in_specs=[
            pl.BlockSpec((tm, D), row),
            _resident(w_uv.shape),
            _resident(sgu_g.shape),
            _resident(w_s.shape),
            _resident(b_full.shape),
            _resident(w_o_sgu.shape),
            _resident(w_g1.shape),
            _resident(b_g1.shape),
        ],
        out_specs=pl.BlockSpec((tm, D), row),
        out_shape=jax.ShapeDtypeStruct((T, D), F32),
        compiler_params=_params(1),
        name="sgu_branch",
    )(a, w_uv, sgu_g, w_s, b_full, w_o_sgu, w_g1, b_g1)


def _attn_kernel(q_ref, kn_ref, kpe_ref, v_ref, o_ref, kf_ref, *, tq, scale):
    seq = q_ref.shape[2]
    row = lax.broadcasted_iota(jnp.int32, (tq, tq), 0)
    col = lax.broadcasted_iota(jnp.int32, (tq, tq), 1)
    causal = row >= col
    neg = jnp.finfo(F32).min
    kf_ref[:, 0:LANES] = kn_ref[0, 0]
    kf_ref[:, LANES:2 * LANES] = kpe_ref[0]

    for qi in range(seq // tq):
        q0 = qi * tq
        k_len = q0 + tq
        q = q_ref[0, 0, q0:k_len, :]
        s = lax.dot_general(q, kf_ref[0:k_len, :], (((1,), (1,)), ((), ())),
                            preferred_element_type=F32) * (scale * LOG2_E)
        s_diag = jnp.where(causal, s[:, q0:k_len], neg)
        s = jnp.concatenate([s[:, 0:q0], s_diag], axis=1) if qi else s_diag
        m = jnp.max(s, axis=-1, keepdims=True)
        p = jnp.exp2(s - m)
        l = jnp.sum(p, axis=-1, keepdims=True)
        acc = _dot(p.astype(BF16), v_ref[0, 0, 0:k_len, :])
        o_ref[0, q0:k_len, :] = (acc / l).astype(o_ref.dtype)


def _attention(q, k_nope, k_pe, v, tq):
    B, H, S, _ = q.shape
    scale = (QK_NOPE_DIM + QK_ROPE_DIM) ** -0.5
    return pl.pallas_call(
        functools.partial(_attn_kernel, tq=tq, scale=scale),
        grid=(B, H),
        in_specs=[
            pl.BlockSpec((1, 1, S, 2 * LANES), lambda b, h: (b, h, 0, 0)),
            pl.BlockSpec((1, 1, S, LANES), lambda b, h: (b, h, 0, 0)),
            pl.BlockSpec((1, S, LANES), lambda b, h: (b, 0, 0)),
            pl.BlockSpec((1, 1, S, LANES), lambda b, h: (b, h, 0, 0)),
        ],
        out_specs=pl.BlockSpec((1, S, LANES), lambda b, h: (b, 0, h)),
        out_shape=jax.ShapeDtypeStruct((B, S, H * V_HEAD_DIM), BF16),
        scratch_shapes=[pltpu.VMEM((S, 2 * LANES), BF16)],
        compiler_params=_params(2),
        name="mla_attention",
    )(q, k_nope, k_pe, v)


def _merge_kernel(attn_ref, a_ref, m_ref, x_ref, woa_ref, wg0_ref, bg0_ref, wout_ref, fg_ref,
                  h_ref, f_ref):
    y_attn = _dot(attn_ref[...], woa_ref[...])
    gate = jax.nn.sigmoid(_dot(a_ref[...], wg0_ref[...]) + bg0_ref[...])
    merged = (gate * y_attn + m_ref[...]).astype(BF16)
    h = x_ref[...] + _dot(merged, wout_ref[...])
    h_ref[...] = h
    f_ref[...] = _rms(h, fg_ref[...]).astype(BF16)


def _merge(attn, a, m_sgu, x, w_o_attn, w_g0, b_g0, w_out, ffn_g, tm):
    T, D = x.shape
    row = lambda i: (i, 0)
    tile = pl.BlockSpec((tm, D), row)
    return pl.pallas_call(
        _merge_kernel,
        grid=(T // tm,),
        in_specs=[tile, tile, tile, tile,
                  _resident(w_o_attn.shape), _resident(w_g0.shape), _resident(b_g0.shape),
                  _resident(w_out.shape), _resident(ffn_g.shape)],
        out_specs=[tile, tile],
        out_shape=[jax.ShapeDtypeStruct((T, D), F32), jax.ShapeDtypeStruct((T, D), BF16)],
        compiler_params=_params(1),
        name="merge_outproj",
    )(attn, a, m_sgu, x, w_o_attn, w_g0, b_g0, w_out, ffn_g)


def _ffn_kernel(f_ref, h_ref, wg_ref, wu_ref, wd_ref, ng_ref, o_ref):
    j = pl.program_id(1)

    @pl.when(j == 0)
    def _():
        o_ref[...] = h_ref[...]

    f = f_ref[...]
    gate = _dot(f, wg_ref[...].astype(BF16))
    up = _dot(f, wu_ref[...].astype(BF16))
    act = (jax.nn.silu(gate) * up).astype(BF16)
    o_ref[...] += _dot(act, wd_ref[...].astype(BF16))

    @pl.when(j == pl.num_programs(1) - 1)
    def _():
        o_ref[...] = _rms(o_ref[...], ng_ref[...])


def _ffn(f, h, w_gate, w_up, w_down, final_g, tm, tf):
    T, D = h.shape
    d_ff = w_gate.shape[1]
    row = lambda i, j: (i, 0)
    return pl.pallas_call(
        _ffn_kernel,
        grid=(T // tm, d_ff // tf),
        in_specs=[
            pl.BlockSpec((tm, D), row),
            pl.BlockSpec((tm, D), row, pipeline_mode=pl.Buffered(1)),
            pl.BlockSpec((D, tf), lambda i, j: (0, j)),
            pl.BlockSpec((D, tf), lambda i, j: (0, j)),
            pl.BlockSpec((tf, D), lambda i, j: (j, 0)),
            _resident((1, D)),
        ],
        out_specs=pl.BlockSpec((tm, D), row),
        out_shape=jax.ShapeDtypeStruct((T, D), F32),
        compiler_params=_params(2),
        name="swiglu_ffn",
    )(f, h, w_gate, w_up, w_down, final_g)


def _dup_rope_cols(w):
    x1, x2 = w[..., :HALF_ROPE], w[..., HALF_ROPE:]
    return jnp.concatenate([x1, x2, x2, x1], axis=-1)


def kernel(x, positions, norm_mix_g, w_in, b_gate, q_norm_g, w_uq, kv_norm_g, w_ukv, w_o_attn,
           sgu_norm_g, w_sgu, b_sgu, w_o_sgu, w_out, norm_ffn_g, w_gate_ffn, w_up_ffn,
           w_down_ffn, norm_final_g):
    B, S, D = x.shape
    T = B * S
    depth = w_in.shape[0]
    assert depth == 1, "the final norm is fused into the FFN epilogue of a single layer"
    o1 = Q_LORA_RANK
    o2 = o1 + KV_LORA_RANK
    o3 = o2 + QK_ROPE_DIM
    o4 = o3 + 2 * SGU_WIDTH

    cs = _rope_tables(positions).reshape(B, S, LANES)
    row_vec = lambda v: v.reshape(1, -1).astype(F32)

    h = x
    out = None
    for l in range(depth):
        wi = w_in[l]
        w_lat = jnp.concatenate([wi[:, :o2], _dup_rope_cols(wi[:, o2:o3])], axis=1).astype(BF16)
        w_uv = wi[:, o3:o4].astype(BF16)
        w_g0 = wi[:, o4:o4 + D].astype(BF16)
        w_g1 = wi[:, o4 + D:].astype(BF16)
        uq = w_uq[l].reshape(Q_LORA_RANK, N_HEADS, QK_NOPE_DIM + QK_ROPE_DIM)
        w_uq_p = jnp.concatenate(
            [uq[..., :QK_NOPE_DIM].reshape(Q_LORA_RANK, -1),
             _dup_rope_cols(uq[..., QK_NOPE_DIM:]).reshape(Q_LORA_RANK, -1)], axis=1).astype(BF16)
        ukv = w_ukv[l].reshape(KV_LORA_RANK, N_HEADS, QK_NOPE_DIM + V_HEAD_DIM)
        w_ukv_p = jnp.concatenate(
            [ukv[..., :QK_NOPE_DIM].reshape(KV_LORA_RANK, -1),
             ukv[..., QK_NOPE_DIM:].reshape(KV_LORA_RANK, -1)], axis=1).astype(BF16)
        b_full = jnp.repeat(b_sgu[l].T, SGU_GROUP_DIM, axis=1).astype(F32)

        a, q, k_nope, v, k_pe = _inproj(
            h, row_vec(norm_mix_g[l]), cs, w_lat, row_vec(q_norm_g[l]), row_vec(kv_norm_g[l]),
            w_uq_p, w_ukv_p, tm=256)
        a2 = a.reshape(T, D)
        m_sgu = _sgu_branch(
            a2, w_uv, row_vec(sgu_norm_g[l]), w_sgu[l], b_full, w_o_sgu[l].astype(BF16),
            w_g1, row_vec(b_gate[l, D:]), tm=256)
        attn = _attention(q, k_nope, k_pe, v, tq=256)
        h_mid, f = _merge(
            attn.reshape(T, D), a2, m_sgu, h.reshape(T, D), w_o_attn[l].astype(BF16), w_g0,
            row_vec(b_gate[l, :D]), w_out[l].astype(BF16), row_vec(norm_ffn_g[l]), tm=256)
        out = _ffn(f, h_mid, w_gate_ffn[l], w_up_ffn[l], w_down_ffn[l], row_vec(norm_final_g),
                   tm=1024, tf=256)
        h = out.reshape(B, S, D)
    return h
```

```python
import functools

import jax
import jax.numpy as jnp
from jax import lax
from jax.experimental import pallas as pl
from jax.experimental.pallas import tpu as pltpu

D_MODEL = 2048
N_HEADS = 16
QK_NOPE_DIM = 128
QK_ROPE_DIM = 64
V_HEAD_DIM = 128
Q_LORA_RANK = 512
KV_LORA_RANK = 512
ROPE_THETA = 10000.0
SGU_GROUPS = 8
SGU_GROUP_DIM = 128
SGU_WIDTH = SGU_GROUPS * SGU_GROUP_DIM
CHUNK = 128
RMS_EPS = 1e-6
LANES = 128
HALF_ROPE = QK_ROPE_DIM // 2
LOG2_E = 1.4426950408889634
BF16_SUBLANES = 16

VMEM_LIMIT_BYTES = 56 * 1024 * 1024

F32 = jnp.float32
BF16 = jnp.bfloat16


def _rms(x, g):
    return x * lax.rsqrt(jnp.mean(x * x, axis=-1, keepdims=True) + RMS_EPS) * g


def _dot(a, b):
    return jnp.dot(a, b, preferred_element_type=F32)


def _resident(shape):
    return pl.BlockSpec(shape, lambda *_: (0,) * len(shape), pipeline_mode=pl.Buffered(1))


def _params(n_axes):
    return pltpu.CompilerParams(
        dimension_semantics=("arbitrary",) * n_axes, vmem_limit_bytes=VMEM_LIMIT_BYTES)


def _rope_table_kernel(pos_ref, freq_ref, cos_ref, sin_ref):
    ang = pos_ref[...] * freq_ref[...]
    cos_ref[...] = jnp.cos(ang)
    sin_ref[...] = jnp.sin(ang)


def _rope_tables(positions):
    n_tok = positions.size
    per_row = LANES // HALF_ROPE
    inv_freq = ROPE_THETA ** (-jnp.arange(0, QK_ROPE_DIM, 2, dtype=F32) / QK_ROPE_DIM)
    pos_rep = jnp.repeat(positions.astype(F32).reshape(n_tok // per_row, per_row), HALF_ROPE, axis=1)
    freq = jnp.tile(inv_freq, per_row).reshape(1, LANES)
    shape = jax.ShapeDtypeStruct((n_tok // per_row, LANES), F32)
    cos, sin = pl.pallas_call(
        _rope_table_kernel, out_shape=(shape, shape), name="rope_tables")(pos_rep, freq)
    cos = cos.reshape(n_tok, HALF_ROPE)
    sin = sin.reshape(n_tok, HALF_ROPE)
    return jnp.concatenate([cos, cos, -sin, sin], axis=-1)


def _rope_dup(x, cs):
    y = x * cs
    return y + pltpu.roll(y, LANES // 2, 1)


def _inproj_kernel(x_ref, g_ref, cs_ref, wlat_ref, qg_ref, kvg_ref, wuq_ref, wukv_ref,
                   a_ref, q_ref, kn_ref, v_ref, kpe_ref):
    a = _rms(x_ref[0], g_ref[...]).astype(BF16)
    a_ref[0] = a
    z = _dot(a, wlat_ref[...])
    qn = _rms(z[:, :Q_LORA_RANK], qg_ref[...]).astype(BF16)
    kvn = _rms(z[:, Q_LORA_RANK:Q_LORA_RANK + KV_LORA_RANK], kvg_ref[...]).astype(BF16)
    cs = cs_ref[0]
    kpe = _rope_dup(z[:, Q_LORA_RANK + KV_LORA_RANK:], cs)
    lane = lax.broadcasted_iota(jnp.int32, kpe.shape, 1)
    kpe_ref[0] = jnp.where(lane < QK_ROPE_DIM, kpe, 0.0).astype(BF16)

    heads_per_dot = 4
    width = heads_per_dot * LANES
    nope_cols = N_HEADS * QK_NOPE_DIM
    for hg in range(N_HEADS // heads_per_dot):
        c0 = hg * width
        q_nope = _dot(qn, wuq_ref[:, c0:c0 + width])
        q_pe = _dot(qn, wuq_ref[:, nope_cols + c0:nope_cols + c0 + width])
        k_nope = _dot(kvn, wukv_ref[:, c0:c0 + width])
        val = _dot(kvn, wukv_ref[:, nope_cols + c0:nope_cols + c0 + width])
        for hh in range(heads_per_dot):
            h = hg * heads_per_dot + hh
            sl = slice(hh * LANES, (hh + 1) * LANES)
            q_ref[0, h, :, 0:LANES] = q_nope[:, sl].astype(BF16)
            q_ref[0, h, :, LANES:2 * LANES] = _rope_dup(q_pe[:, sl], cs).astype(BF16)
            kn_ref[0, h] = k_nope[:, sl].astype(BF16)
            v_ref[0, h] = val[:, sl].astype(BF16)


def _inproj(x, norm_g, cs, w_lat, q_g, kv_g, w_uq, w_ukv, tm):
    B, S, D = x.shape
    lat = w_lat.shape[1]
    row = lambda b, i: (b, i, 0)
    head = lambda b, i: (b, 0, i, 0)
    return pl.pallas_call(
        _inproj_kernel,
        grid=(B, S // tm),
        in_specs=[
            pl.BlockSpec((1, tm, D), row),
            _resident((1, D)),
            pl.BlockSpec((1, tm, LANES), row),
            _resident((D, lat)),
            _resident((1, Q_LORA_RANK)),
            _resident((1, KV_LORA_RANK)),
            _resident(w_uq.shape),
            _resident(w_ukv.shape),
        ],
        out_specs=[
            pl.BlockSpec((1, tm, D), row),
            pl.BlockSpec((1, N_HEADS, tm, 2 * LANES), head),
            pl.BlockSpec((1, N_HEADS, tm, LANES), head),
            pl.BlockSpec((1, N_HEADS, tm, LANES), head),
            pl.BlockSpec((1, tm, LANES), row),
        ],
        out_shape=[
            jax.ShapeDtypeStruct((B, S, D), BF16),
            jax.ShapeDtypeStruct((B, N_HEADS, S, 2 * LANES), BF16),
            jax.ShapeDtypeStruct((B, N_HEADS, S, LANES), BF16),
            jax.ShapeDtypeStruct((B, N_HEADS, S, LANES), BF16),
            jax.ShapeDtypeStruct((B, S, LANES), BF16),
        ],
        compiler_params=_params(2),
        name="inproj",
    )(x, norm_g, cs, w_lat, q_g, kv_g, w_uq, w_ukv)


def _sgu_kernel(a_ref, wuv_ref, sg_ref, ws_ref, bfull_ref, wos_ref, wg1_ref, bg1_ref, m_ref):
    a = a_ref[...]
    tm = a.shape[0]
    n_chunks = tm // CHUNK
    uv = jax.nn.gelu(_dot(a, wuv_ref[...]))
    u = uv[:, :SGU_WIDTH]
    vn = _rms(uv[:, SGU_WIDTH:], sg_ref[...]).astype(BF16)
    t_idx = lax.broadcasted_iota(jnp.int32, (CHUNK, CHUNK), 0)
    s_idx = lax.broadcasted_iota(jnp.int32, (CHUNK, CHUNK), 1)
    causal = t_idx >= s_idx
    mixed_cols = []
    for g in range(SGU_GROUPS):
        ws = jnp.where(causal, ws_ref[g], 0.0).astype(BF16)
        gs = slice(g * SGU_GROUP_DIM, (g + 1) * SGU_GROUP_DIM)
        rhs = jnp.concatenate([vn[c * CHUNK:(c + 1) * CHUNK, gs] for c in range(n_chunks)], axis=1)
        mixed_cols.append(_dot(ws, rhs))
    bfull = bfull_ref[...]
    rows = []
    for c in range(n_chunks):
        cs = slice(c * SGU_GROUP_DIM, (c + 1) * SGU_GROUP_DIM)
        mixed = jnp.concatenate([mixed_cols[g][:, cs] for g in range(SGU_GROUPS)], axis=1)
        rows.append(u[c * CHUNK:(c + 1) * CHUNK] * (mixed + bfull))
    sgu_out = jnp.concatenate(rows, axis=0).astype(BF16)
    y_sgu = _dot(sgu_out, wos_ref[...])
    gate = jax.nn.sigmoid(_dot(a, wg1_ref[...]) + bg1_ref[...])
    m_ref[...] = gate * y_sgu


def _sgu_branch(a, w_uv, sgu_g, w_s, b_full, w_o_sgu, w_g1, b_g1, tm):
    T, D = a.shape
    row = lambda i: (i, 0)
    return pl.pallas_call(
        _sgu_kernel,
        grid=(T // tm,),
        in_specs=[
            pl.BlockSpec((tm, D), row),
            _resident(w_uv.shape),
            _resident(sgu_g.shape),
            _resident(w_s.shape),
            _resident(b_full.shape),
            _resident(w_o_sgu.shape),
            _resident(w_g1.shape),
            _resident(b_g1.shape),
        ],
        out_specs=pl.BlockSpec((tm, D), row),
        out_shape=jax.ShapeDtypeStruct((T, D), F32),
        compiler_params=_params(1),
        name="sgu_branch",
    )(a, w_uv, sgu_g, w_s, b_full, w_o_sgu, w_g1, b_g1)


def _attn_kernel(q_ref, kn_ref, kpe_ref, v_ref, *rest, tq, scale, n_cast):
    cast_in, (o_ref,), cast_out, (kf_ref,) = (
        rest[:n_cast], rest[n_cast:n_cast + 1], rest[n_cast + 1:2 * n_cast + 1], rest[2 * n_cast + 1:])
    for src, dst in zip(cast_in, cast_out):
        dst[...] = src[...].astype(dst.dtype)
    seq = q_ref.shape[2]
    row = lax.broadcasted_iota(jnp.int32, (tq, tq), 0)
    col = lax.broadcasted_iota(jnp.int32, (tq, tq), 1)
    causal = row >= col
    neg = jnp.finfo(F32).min
    kf_ref[:, 0:LANES] = kn_ref[0, 0]
    kf_ref[:, LANES:2 * LANES] = kpe_ref[0]

    for qi in range(seq // tq):
        q0 = qi * tq
        k_len = q0 + tq
        q = q_ref[0, 0, q0:k_len, :]
        s = lax.dot_general(q, kf_ref[0:k_len, :], (((1,), (1,)), ((), ())),
                            preferred_element_type=F32) * (scale * LOG2_E)
        s_diag = jnp.where(causal, s[:, q0:k_len], neg)
        s = jnp.concatenate([s[:, 0:q0], s_diag], axis=1) if qi else s_diag
        m = jnp.max(s, axis=-1, keepdims=True)
        p = jnp.exp2(s - m)
        l = jnp.sum(p, axis=-1, keepdims=True)
        acc = _dot(p.astype(BF16), v_ref[0, 0, 0:k_len, :])
        o_ref[0, q0:k_len, :] = (acc / l).astype(o_ref.dtype)


def _cast_block_spec(w, n_steps, n_heads):
    rows, cols = w.shape
    share = 1 if (rows // n_steps) % BF16_SUBLANES == 0 else 2
    blk = rows * share // n_steps
    assert blk * n_steps == rows * share and blk % BF16_SUBLANES == 0, w.shape
    return pl.BlockSpec((blk, cols), lambda b, h: ((b * n_heads + h) // share, 0))


def _attention(q, k_nope, k_pe, v, cast_weights, tq):
    B, H, S, _ = q.shape
    scale = (QK_NOPE_DIM + QK_ROPE_DIM) ** -0.5
    cast_specs = [_cast_block_spec(w, B * H, H) for w in cast_weights]
    outs = pl.pallas_call(
        functools.partial(_attn_kernel, tq=tq, scale=scale, n_cast=len(cast_weights)),
        grid=(B, H),
        in_specs=[
            pl.BlockSpec((1, 1, S, 2 * LANES), lambda b, h: (b, h, 0, 0)),
            pl.BlockSpec((1, 1, S, LANES), lambda b, h: (b, h, 0, 0)),
            pl.BlockSpec((1, S, LANES), lambda b, h: (b, 0, 0)),
            pl.BlockSpec((1, 1, S, LANES), lambda b, h: (b, h, 0, 0)),
        ] + cast_specs,
        out_specs=[pl.BlockSpec((1, S, LANES), lambda b, h: (b, 0, h))] + cast_specs,
        out_shape=[jax.ShapeDtypeStruct((B, S, H * V_HEAD_DIM), BF16)]
        + [jax.ShapeDtypeStruct(w.shape, BF16) for w in cast_weights],
        scratch_shapes=[pltpu.VMEM((S, 2 * LANES), BF16)],
        compiler_params=_params(2),
        name="mla_attention",
    )(q, k_nope, k_pe, v, *cast_weights)
    return outs[0], outs[1:]


def _merge_kernel(attn_ref, a_ref, m_ref, x_ref, woa_ref, wg0_ref, bg0_ref, wout_ref, fg_ref,
                  h_ref, f_ref):
    y_attn = _dot(attn_ref[...], woa_ref[...])
    gate = jax.nn.sigmoid(_dot(a_ref[...], wg0_ref[...]) + bg0_ref[...])
    merged = (gate * y_attn + m_ref[...]).astype(BF16)
    h = x_ref[...] + _dot(merged, wout_ref[...])
    h_ref[...] = h
    f_ref[...] = _rms(h, fg_ref[...]).astype(BF16)


def _merge(attn, a, m_sgu, x, w_o_attn, w_g0, b_g0, w_out, ffn_g, tm):
    T, D = x.shape
    row = lambda i: (i, 0)
    tile = pl.BlockSpec((tm, D), row)
    return pl.pallas_call(
        _merge_kernel,
        grid=(T // tm,),
        in_specs=[tile, tile, tile, tile,
                  _resident(w_o_attn.shape), _resident(w_g0.shape), _resident(b_g0.shape),
                  _resident(w_out.shape), _resident(ffn_g.shape)],
        out_specs=[tile, tile],
        out_shape=[jax.ShapeDtypeStruct((T, D), F32), jax.ShapeDtypeStruct((T, D), BF16)],
        compiler_params=_params(1),
        name="merge_outproj",
    )(attn, a, m_sgu, x, w_o_attn, w_g0, b_g0, w_out, ffn_g)


def _ffn_kernel(f_ref, h_ref, wg_ref, wu_ref, wd_ref, ng_ref, o_ref):
    j = pl.program_id(1)

    @pl.when(j == 0)
    def _():
        o_ref[...] = h_ref[...]

    f = f_ref[...]
    gate = _dot(f, wg_ref[...])
    up = _dot(f, wu_ref[...])
    act = (jax.nn.silu(gate) * up).astype(BF16)
    o_ref[...] += _dot(act, wd_ref[...])

    @pl.when(j == pl.num_programs(1) - 1)
    def _():
        o_ref[...] = _rms(o_ref[...], ng_ref[...])


def _ffn(f, h, w_gate, w_up, w_down, final_g, tm, tf):
    T, D = h.shape
    d_ff = w_gate.shape[1]
    row = lambda i, j: (i, 0)
    return pl.pallas_call(
        _ffn_kernel,
        grid=(T // tm, d_ff // tf),
        in_specs=[
            pl.BlockSpec((tm, D), row),
            pl.BlockSpec((tm, D), row, pipeline_mode=pl.Buffered(1)),
            pl.BlockSpec((D, tf), lambda i, j: (0, j)),
            pl.BlockSpec((D, tf), lambda i, j: (0, j)),
            pl.BlockSpec((tf, D), lambda i, j: (j, 0)),
            _resident((1, D)),
        ],
        out_specs=pl.BlockSpec((tm, D), row),
        out_shape=jax.ShapeDtypeStruct((T, D), F32),
        compiler_params=_params(2),
        name="swiglu_ffn",
    )(f, h, w_gate, w_up, w_down, final_g)


def _dup_rope_cols(w):
    x1, x2 = w[..., :HALF_ROPE], w[..., HALF_ROPE:]
    return jnp.concatenate([x1, x2, x2, x1], axis=-1)


def kernel(x, positions, norm_mix_g, w_in, b_gate, q_norm_g, w_uq, kv_norm_g, w_ukv, w_o_attn,
           sgu_norm_g, w_sgu, b_sgu, w_o_sgu, w_out, norm_ffn_g, w_gate_ffn, w_up_ffn,
           w_down_ffn, norm_final_g):
    B, S, D = x.shape
    T = B * S
    depth = w_in.shape[0]
    assert depth == 1, "the final norm is fused into the FFN epilogue of a single layer"
    o1 = Q_LORA_RANK
    o2 = o1 + KV_LORA_RANK
    o3 = o2 + QK_ROPE_DIM
    o4 = o3 + 2 * SGU_WIDTH

    cs = _rope_tables(positions).reshape(B, S, LANES)
    row_vec = lambda v: v.reshape(1, -1).astype(F32)

    h = x
    out = None
    for l in range(depth):
        wi = w_in[l]
        w_lat = jnp.concatenate([wi[:, :o2], _dup_rope_cols(wi[:, o2:o3])], axis=1).astype(BF16)
        w_uv = wi[:, o3:o4].astype(BF16)
        w_g0 = wi[:, o4:o4 + D].astype(BF16)
        w_g1 = wi[:, o4 + D:].astype(BF16)
        uq = w_uq[l].reshape(Q_LORA_RANK, N_HEADS, QK_NOPE_DIM + QK_ROPE_DIM)
        w_uq_p = jnp.concatenate(
            [uq[..., :QK_NOPE_DIM].reshape(Q_LORA_RANK, -1),
             _dup_rope_cols(uq[..., QK_NOPE_DIM:]).reshape(Q_LORA_RANK, -1)], axis=1).astype(BF16)
        ukv = w_ukv[l].reshape(KV_LORA_RANK, N_HEADS, QK_NOPE_DIM + V_HEAD_DIM)
        w_ukv_p = jnp.concatenate(
            [ukv[..., :QK_NOPE_DIM].reshape(KV_LORA_RANK, -1),
             ukv[..., QK_NOPE_DIM:].reshape(KV_LORA_RANK, -1)], axis=1).astype(BF16)
        b_full = jnp.repeat(b_sgu[l].T, SGU_GROUP_DIM, axis=1).astype(F32)

        a, q, k_nope, v, k_pe = _inproj(
            h, row_vec(norm_mix_g[l]), cs, w_lat, row_vec(q_norm_g[l]), row_vec(kv_norm_g[l]),
            w_uq_p, w_ukv_p, tm=256)
        a2 = a.reshape(T, D)
        attn, (w_os, w_oa, w_o, w_gf, w_uf, w_df) = _attention(
            q, k_nope, k_pe, v,
            [w_o_sgu[l], w_o_attn[l], w_out[l], w_gate_ffn[l], w_up_ffn[l], w_down_ffn[l]], tq=256)
        m_sgu = _sgu_branch(
            a2, w_uv, row_vec(sgu_norm_g[l]), w_sgu[l], b_full, w_os, w_g1,
            row_vec(b_gate[l, D:]), tm=256)
        h_mid, f = _merge(
            attn.reshape(T, D), a2, m_sgu, h.reshape(T, D), w_oa, w_g0,
            row_vec(b_gate[l, :D]), w_o, row_vec(norm_ffn_g[l]), tm=256)
        out = _ffn(f, h_mid, w_gf, w_uf, w_df, row_vec(norm_final_g), tm=1024, tf=512)
        h = out.reshape(B, S, D)
    return h
```

```python
import functools

import jax
import jax.numpy as jnp
from jax import lax
from jax.experimental import pallas as pl
from jax.experimental.pallas import tpu as pltpu

D_MODEL = 2048
N_HEADS = 16
QK_NOPE_DIM = 128
QK_ROPE_DIM = 64
V_HEAD_DIM = 128
Q_LORA_RANK = 512
KV_LORA_RANK = 512
ROPE_THETA = 10000.0
SGU_GROUPS = 8
SGU_GROUP_DIM = 128
SGU_WIDTH = SGU_GROUPS * SGU_GROUP_DIM
CHUNK = 128
N_BRANCH = 2
RMS_EPS = 1e-6
KPE_OFF = Q_LORA_RANK + KV_LORA_RANK
UV_OFF = KPE_OFF + QK_ROPE_DIM
GATE_OFF = UV_OFF + 2 * SGU_WIDTH
D_IN = GATE_OFF + N_BRANCH * D_MODEL
LANES = 128
HALF_ROPE = QK_ROPE_DIM // 2
LOG2_E = 1.4426950408889634
BF16_SUBLANES = 16

VMEM_LIMIT_BYTES = 56 * 1024 * 1024

F32 = jnp.float32
BF16 = jnp.bfloat16


def _rms(x, g):
    return x * lax.rsqrt(jnp.mean(x * x, axis=-1, keepdims=True) + RMS_EPS) * g


def _dot(a, b):
    return jnp.dot(a, b, preferred_element_type=F32)


def _resident(shape):
    return pl.BlockSpec(shape, lambda *_: (0,) * len(shape), pipeline_mode=pl.Buffered(1))


def _params(n_axes):
    return pltpu.CompilerParams(
        dimension_semantics=("arbitrary",) * n_axes, vmem_limit_bytes=VMEM_LIMIT_BYTES)


def _rope_table_kernel(pos_ref, freq_ref, cos_ref, sin_ref):
    ang = pos_ref[...] * freq_ref[...]
    cos_ref[...] = jnp.cos(ang)
    sin_ref[...] = jnp.sin(ang)


def _rope_tables(positions):
    n_tok = positions.size
    per_row = LANES // HALF_ROPE
    inv_freq = ROPE_THETA ** (-jnp.arange(0, QK_ROPE_DIM, 2, dtype=F32) / QK_ROPE_DIM)
    pos_rep = jnp.repeat(positions.astype(F32).reshape(n_tok // per_row, per_row), HALF_ROPE, axis=1)
    freq = jnp.tile(inv_freq, per_row).reshape(1, LANES)
    shape = jax.ShapeDtypeStruct((n_tok // per_row, LANES), F32)
    cos, sin = pl.pallas_call(
        _rope_table_kernel, out_shape=(shape, shape), name="rope_tables")(pos_rep, freq)
    cos = cos.reshape(n_tok, HALF_ROPE)
    sin = sin.reshape(n_tok, HALF_ROPE)
    return jnp.concatenate([cos, cos, -sin, sin], axis=-1)


def _rope_dup(x, cs):
    y = x * cs
    return y + pltpu.roll(y, LANES // 2, 1)


def _inproj_kernel(x_ref, g_ref, cs_ref, wlat_ref, qg_ref, kvg_ref, wuq_ref, wukv_ref,
                   a_ref, q_ref, kn_ref, v_ref, kpe_ref):
    a = _rms(x_ref[0], g_ref[...]).astype(BF16)
    a_ref[0] = a
    z = _dot(a, wlat_ref[...])
    qn = _rms(z[:, :Q_LORA_RANK], qg_ref[...]).astype(BF16)
    kvn = _rms(z[:, Q_LORA_RANK:Q_LORA_RANK + KV_LORA_RANK], kvg_ref[...]).astype(BF16)
    cs = cs_ref[0]
    kpe = _rope_dup(z[:, Q_LORA_RANK + KV_LORA_RANK:], cs)
    lane = lax.broadcasted_iota(jnp.int32, kpe.shape, 1)
    kpe_ref[0] = jnp.where(lane < QK_ROPE_DIM, kpe, 0.0).astype(BF16)

    heads_per_dot = 4
    width = heads_per_dot * LANES
    nope_cols = N_HEADS * QK_NOPE_DIM
    for hg in range(N_HEADS // heads_per_dot):
        c0 = hg * width
        q_nope = _dot(qn, wuq_ref[:, c0:c0 + width])
        q_pe = _dot(qn, wuq_ref[:, nope_cols + c0:nope_cols + c0 + width])
        k_nope = _dot(kvn, wukv_ref[:, c0:c0 + width])
        val = _dot(kvn, wukv_ref[:, nope_cols + c0:nope_cols + c0 + width])
        for hh in range(heads_per_dot):
            h = hg * heads_per_dot + hh
            sl = slice(hh * LANES, (hh + 1) * LANES)
            q_ref[0, h, :, 0:LANES] = q_nope[:, sl].astype(BF16)
            q_ref[0, h, :, LANES:2 * LANES] = _rope_dup(q_pe[:, sl], cs).astype(BF16)
            kn_ref[0, h] = k_nope[:, sl].astype(BF16)
            v_ref[0, h] = val[:, sl].astype(BF16)


def _inproj(x, norm_g, cs, w_lat, q_g, kv_g, w_uq, w_ukv, tm):
    B, S, D = x.shape
    lat = w_lat.shape[1]
    row = lambda b, i: (b, i, 0)
    head = lambda b, i: (b, 0, i, 0)
    return pl.pallas_call(
        _inproj_kernel,
        grid=(B, S // tm),
        in_specs=[
            pl.BlockSpec((1, tm, D), row),
            _resident((1, D)),
            pl.BlockSpec((1, tm, LANES), row),
            _resident((D, lat)),
            _resident((1, Q_LORA_RANK)),
            _resident((1, KV_LORA_RANK)),
            _resident(w_uq.shape),
            _resident(w_ukv.shape),
        ],
        out_specs=[
            pl.BlockSpec((1, tm, D), row),
            pl.BlockSpec((1, N_HEADS, tm, 2 * LANES), head),
            pl.BlockSpec((1, N_HEADS, tm, LANES), head),
            pl.BlockSpec((1, N_HEADS, tm, LANES), head),
            pl.BlockSpec((1, tm, LANES), row),
        ],
        out_shape=[
            jax.ShapeDtypeStruct((B, S, D), BF16),
            jax.ShapeDtypeStruct((B, N_HEADS, S, 2 * LANES), BF16),
            jax.ShapeDtypeStruct((B, N_HEADS, S, LANES), BF16),
            jax.ShapeDtypeStruct((B, N_HEADS, S, LANES), BF16),
            jax.ShapeDtypeStruct((B, S, LANES), BF16),
        ],
        compiler_params=_params(2),
        name="inproj",
    )(x, norm_g, cs, w_lat, q_g, kv_g, w_uq, w_ukv)


def _sgu_kernel(a_ref, wuv_ref, sg_ref, ws_ref, bfull_ref, wos_ref, wg1_ref, bg1_ref, m_ref):
    a = a_ref[...]
    tm = a.shape[0]
    n_chunks = tm // CHUNK
    uv = jax.nn.gelu(_dot(a, wuv_ref[...]))
    u = uv[:, :SGU_WIDTH]
    vn = _rms(uv[:, SGU_WIDTH:], sg_ref[...]).astype(BF16)
    t_idx = lax.broadcasted_iota(jnp.int32, (CHUNK, CHUNK), 0)
    s_idx = lax.broadcasted_iota(jnp.int32, (CHUNK, CHUNK), 1)
    causal = t_idx >= s_idx
    mixed_cols = []
    for g in range(SGU_GROUPS):
        ws = jnp.where(causal, ws_ref[g], 0.0).astype(BF16)
        gs = slice(g * SGU_GROUP_DIM, (g + 1) * SGU_GROUP_DIM)
        rhs = jnp.concatenate([vn[c * CHUNK:(c + 1) * CHUNK, gs] for c in range(n_chunks)], axis=1)
        mixed_cols.append(_dot(ws, rhs))
    bfull = bfull_ref[...]
    rows = []
    for c in range(n_chunks):
        cs = slice(c * SGU_GROUP_DIM, (c + 1) * SGU_GROUP_DIM)
        mixed = jnp.concatenate([mixed_cols[g][:, cs] for g in range(SGU_GROUPS)], axis=1)
        rows.append(u[c * CHUNK:(c + 1) * CHUNK] * (mixed + bfull))
    sgu_out = jnp.concatenate(rows, axis=0).astype(BF16)
    y_sgu = _dot(sgu_out, wos_ref[...])
    gate = jax.nn.sigmoid(_dot(a, wg1_ref[...]) + bg1_ref[...])
    m_ref[...] = gate * y_sgu


def _sgu_branch(a, w_uv, sgu_g, w_s, b_full, w_o_sgu, w_g1, b_g1, tm):
    T, D = a.shape
    row = lambda i: (i, 0)
    return pl.pallas_call(
        _sgu_kernel,
        grid=(T // tm,),
        in_specs=[
            pl.BlockSpec((tm, D), row),
            _resident(w_uv.shape),
            _resident(sgu_g.shape),
            _resident(w_s.shape),
            _resident(b_full.shape),
            _resident(w_o_sgu.shape),
            _resident(w_g1.shape),
            _resident(b_g1.shape),
        ],
        out_specs=pl.BlockSpec((tm, D), row),
        out_shape=jax.ShapeDtypeStruct((T, D), F32),
        compiler_params=_params(1),
        name="sgu_branch",
    )(a, w_uv, sgu_g, w_s, b_full, w_o_sgu, w_g1, b_g1)


def _split_w_in_rows(win_ref, tail_ref, wuv_ref, wg0_ref, wg1_ref):
    half = LANES // 2
    n_uv = 2 * SGU_WIDTH
    x = win_ref[:, UV_OFF - half:UV_OFF - half + n_uv + LANES]
    wuv_ref[...] = pltpu.roll(x, n_uv + LANES - half, 1)[:, :n_uv].astype(BF16)
    n_gate = D_IN - GATE_OFF
    xg = win_ref[:, GATE_OFF - half:GATE_OFF - half + n_gate]
    sh = pltpu.roll(xg, n_gate - half, 1)
    lane = lax.broadcasted_iota(jnp.int32, tail_ref.shape, 1)
    last = jnp.where(lane < half, sh[:, n_gate - LANES:], pltpu.roll(tail_ref[...], half, 1))
    wg0_ref[...] = sh[:, :D_MODEL].astype(BF16)
    wg1_ref[:, :D_MODEL - LANES] = sh[:, D_MODEL:n_gate - LANES].astype(BF16)
    wg1_ref[:, D_MODEL - LANES:] = last.astype(BF16)


def _attn_kernel(q_ref, kn_ref, kpe_ref, v_ref, win_ref, tail_ref, *rest, tq, scale, n_cast):
    cast_in, rest = rest[:n_cast], rest[n_cast:]
    o_ref, wuv_ref, wg0_ref, wg1_ref = rest[:4]
    cast_out, (kf_ref,) = rest[4:4 + n_cast], rest[4 + n_cast:]
    _split_w_in_rows(win_ref, tail_ref, wuv_ref, wg0_ref, wg1_ref)
    for src, dst in zip(cast_in, cast_out):
        dst[...] = src[...].astype(dst.dtype)
    seq = q_ref.shape[2]
    row = lax.broadcasted_iota(jnp.int32, (tq, tq), 0)
    col = lax.broadcasted_iota(jnp.int32, (tq, tq), 1)
    causal = row >= col
    neg = jnp.finfo(F32).min
    kf_ref[:, 0:LANES] = kn_ref[0, 0]
    kf_ref[:, LANES:2 * LANES] = kpe_ref[0]

    for qi in range(seq // tq):
        q0 = qi * tq
        k_len = q0 + tq
        q = q_ref[0, 0, q0:k_len, :]
        s = lax.dot_general(q, kf_ref[0:k_len, :], (((1,), (1,)), ((), ())),
                            preferred_element_type=F32) * (scale * LOG2_E)
        s_diag = jnp.where(causal, s[:, q0:k_len], neg)
        s = jnp.concatenate([s[:, 0:q0], s_diag], axis=1) if qi else s_diag
        m = jnp.max(s, axis=-1, keepdims=True)
        p = jnp.exp2(s - m)
        l = jnp.sum(p, axis=-1, keepdims=True)
        acc = _dot(p.astype(BF16), v_ref[0, 0, 0:k_len, :])
        o_ref[0, q0:k_len, :] = (acc / l).astype(o_ref.dtype)


def _cast_block_spec(w, n_steps, n_heads):
    rows, cols = w.shape
    share = 1 if (rows // n_steps) % BF16_SUBLANES == 0 else 2
    blk = rows * share // n_steps
    assert blk * n_steps == rows * share and blk % BF16_SUBLANES == 0, w.shape
    return pl.BlockSpec((blk, cols), lambda b, h: ((b * n_heads + h) // share, 0))


def _attention(q, k_nope, k_pe, v, w_in, w_in_tail, cast_weights, tq):
    B, H, S, _ = q.shape
    scale = (QK_NOPE_DIM + QK_ROPE_DIM) ** -0.5
    cast_specs = [_cast_block_spec(w, B * H, H) for w in cast_weights]
    section = jax.ShapeDtypeStruct((w_in.shape[0], D_MODEL), BF16)
    section_spec = _cast_block_spec(section, B * H, H)
    outs = pl.pallas_call(
        functools.partial(_attn_kernel, tq=tq, scale=scale, n_cast=len(cast_weights)),
        grid=(B, H),
        in_specs=[
            pl.BlockSpec((1, 1, S, 2 * LANES), lambda b, h: (b, h, 0, 0)),
            pl.BlockSpec((1, 1, S, LANES), lambda b, h: (b, h, 0, 0)),
            pl.BlockSpec((1, S, LANES), lambda b, h: (b, 0, 0)),
            pl.BlockSpec((1, 1, S, LANES), lambda b, h: (b, h, 0, 0)),
            _cast_block_spec(w_in, B * H, H),
            _cast_block_spec(w_in_tail, B * H, H),
        ] + cast_specs,
        out_specs=[pl.BlockSpec((1, S, LANES), lambda b, h: (b, 0, h))] + [section_spec] * 3
        + cast_specs,
        out_shape=[jax.ShapeDtypeStruct((B, S, H * V_HEAD_DIM), BF16)] + [section] * 3
        + [jax.ShapeDtypeStruct(w.shape, BF16) for w in cast_weights],
        scratch_shapes=[pltpu.VMEM((S, 2 * LANES), BF16)],
        compiler_params=_params(2),
        name="mla_attention",
    )(q, k_nope, k_pe, v, w_in, w_in_tail, *cast_weights)
    return outs[0], outs[1:4], outs[4:]


def _merge_kernel(attn_ref, a_ref, m_ref, x_ref, woa_ref, wg0_ref, bg0_ref, wout_ref, fg_ref,
                  h_ref, f_ref):
    y_attn = _dot(attn_ref[...], woa_ref[...])
    gate = jax.nn.sigmoid(_dot(a_ref[...], wg0_ref[...]) + bg0_ref[...])
    merged = (gate * y_attn + m_ref[...]).astype(BF16)
    h = x_ref[...] + _dot(merged, wout_ref[...])
    h_ref[...] = h
    f_ref[...] = _rms(h, fg_ref[...]).astype(BF16)


def _merge(attn, a, m_sgu, x, w_o_attn, w_g0, b_g0, w_out, ffn_g, tm):
    T, D = x.shape
    row = lambda i: (i, 0)
    tile = pl.BlockSpec((tm, D), row)
    return pl.pallas_call(
        _merge_kernel,
        grid=(T // tm,),
        in_specs=[tile, tile, tile, tile,
                  _resident(w_o_attn.shape), _resident(w_g0.shape), _resident(b_g0.shape),
                  _resident(w_out.shape), _resident(ffn_g.shape)],
        out_specs=[tile, tile],
        out_shape=[jax.ShapeDtypeStruct((T, D), F32), jax.ShapeDtypeStruct((T, D), BF16)],
        compiler_params=_params(1),
        name="merge_outproj",
    )(attn, a, m_sgu, x, w_o_attn, w_g0, b_g0, w_out, ffn_g)


def _ffn_kernel(f_ref, h_ref, wg_ref, wu_ref, wd_ref, ng_ref, o_ref):
    j = pl.program_id(1)

    @pl.when(j == 0)
    def _():
        o_ref[...] = h_ref[...]

    f = f_ref[...]
    gate = _dot(f, wg_ref[...])
    up = _dot(f, wu_ref[...])
    act = (jax.nn.silu(gate) * up).astype(BF16)
    o_ref[...] += _dot(act, wd_ref[...])

    @pl.when(j == pl.num_programs(1) - 1)
    def _():
        o_ref[...] = _rms(o_ref[...], ng_ref[...])


def _ffn(f, h, w_gate, w_up, w_down, final_g, tm, tf):
    T, D = h.shape
    d_ff = w_gate.shape[1]
    row = lambda i, j: (i, 0)
    return pl.pallas_call(
        _ffn_kernel,
        grid=(T // tm, d_ff // tf),
        in_specs=[
            pl.BlockSpec((tm, D), row),
            pl.BlockSpec((tm, D), row, pipeline_mode=pl.Buffered(1)),
            pl.BlockSpec((D, tf), lambda i, j: (0, j)),
            pl.BlockSpec((D, tf), lambda i, j: (0, j)),
            pl.BlockSpec((tf, D), lambda i, j: (j, 0)),
            _resident((1, D)),
        ],
        out_specs=pl.BlockSpec((tm, D), row),
        out_shape=jax.ShapeDtypeStruct((T, D), F32),
        compiler_params=_params(2),
        name="swiglu_ffn",
    )(f, h, w_gate, w_up, w_down, final_g)


def _dup_rope_cols(w):
    x1, x2 = w[..., :HALF_ROPE], w[..., HALF_ROPE:]
    return jnp.concatenate([x1, x2, x2, x1], axis=-1)


def kernel(x, positions, norm_mix_g, w_in, b_gate, q_norm_g, w_uq, kv_norm_g, w_ukv, w_o_attn,
           sgu_norm_g, w_sgu, b_sgu, w_o_sgu, w_out, norm_ffn_g, w_gate_ffn, w_up_ffn,
           w_down_ffn, norm_final_g):
    B, S, D = x.shape
    T = B * S
    depth = w_in.shape[0]
    assert depth == 1, "the final norm is fused into the FFN epilogue of a single layer"
    assert w_in.shape[1:] == (D, D_IN)

    cs = _rope_tables(positions).reshape(B, S, LANES)
    row_vec = lambda v: v.reshape(1, -1).astype(F32)

    h = x
    out = None
    for l in range(depth):
        wi = w_in[l]
        w_lat = jnp.concatenate(
            [wi[:, :KPE_OFF], _dup_rope_cols(wi[:, KPE_OFF:UV_OFF])], axis=1).astype(BF16)
        w_in_tail = jnp.pad(wi[:, D_IN - LANES // 2:], ((0, 0), (0, LANES // 2)))
        uq = w_uq[l].reshape(Q_LORA_RANK, N_HEADS, QK_NOPE_DIM + QK_ROPE_DIM)
        w_uq_p = jnp.concatenate(
            [uq[..., :QK_NOPE_DIM].reshape(Q_LORA_RANK, -1),
             _dup_rope_cols(uq[..., QK_NOPE_DIM:]).reshape(Q_LORA_RANK, -1)], axis=1).astype(BF16)
        ukv = w_ukv[l].reshape(KV_LORA_RANK, N_HEADS, QK_NOPE_DIM + V_HEAD_DIM)
        w_ukv_p = jnp.concatenate(
            [ukv[..., :QK_NOPE_DIM].reshape(KV_LORA_RANK, -1),
             ukv[..., QK_NOPE_DIM:].reshape(KV_LORA_RANK, -1)], axis=1).astype(BF16)
        b_full = jnp.repeat(b_sgu[l].T, SGU_GROUP_DIM, axis=1).astype(F32)

        a, q, k_nope, v, k_pe = _inproj(
            h, row_vec(norm_mix_g[l]), cs, w_lat, row_vec(q_norm_g[l]), row_vec(kv_norm_g[l]),
            w_uq_p, w_ukv_p, tm=256)
        a2 = a.reshape(T, D)
        attn, (w_uv, w_g0, w_g1), (w_os, w_oa, w_o, w_gf, w_uf, w_df) = _attention(
            q, k_nope, k_pe, v, wi, w_in_tail,
            [w_o_sgu[l], w_o_attn[l], w_out[l], w_gate_ffn[l], w_up_ffn[l], w_down_ffn[l]], tq=256)
        m_sgu = _sgu_branch(
            a2, w_uv, row_vec(sgu_norm_g[l]), w_sgu[l], b_full, w_os, w_g1,
            row_vec(b_gate[l, D:]), tm=256)
        h_mid, f = _merge(
            attn.reshape(T, D), a2, m_sgu, h.reshape(T, D), w_oa, w_g0,
            row_vec(b_gate[l, :D]), w_o, row_vec(norm_ffn_g[l]), tm=256)
        out = _ffn(f, h_mid, w_gf, w_uf, w_df, row_vec(norm_final_g), tm=1024, tf=512)
        h = out.reshape(B, S, D)
    return h
```

```python
import functools

import jax
import jax.numpy as jnp
from jax import lax
from jax.experimental import pallas as pl
from jax.experimental.pallas import tpu as pltpu

D_MODEL = 2048
N_HEADS = 16
QK_NOPE_DIM = 128
QK_ROPE_DIM = 64
V_HEAD_DIM = 128
Q_LORA_RANK = 512
KV_LORA_RANK = 512
ROPE_THETA = 10000.0
SGU_GROUPS = 8
SGU_GROUP_DIM = 128
SGU_WIDTH = SGU_GROUPS * SGU_GROUP_DIM
CHUNK = 128
N_BRANCH = 2
RMS_EPS = 1e-6
KPE_OFF = Q_LORA_RANK + KV_LORA_RANK
UV_OFF = KPE_OFF + QK_ROPE_DIM
GATE_OFF = UV_OFF + 2 * SGU_WIDTH
D_IN = GATE_OFF + N_BRANCH * D_MODEL
LANES = 128
HALF_ROPE = QK_ROPE_DIM // 2
LOG2_E = 1.4426950408889634
BF16_SUBLANES = 16

VMEM_LIMIT_BYTES = 56 * 1024 * 1024

F32 = jnp.float32
BF16 = jnp.bfloat16


def _rms(x, g):
    return x * lax.rsqrt(jnp.mean(x * x, axis=-1, keepdims=True) + RMS_EPS) * g


def _dot(a, b):
    return jnp.dot(a, b, preferred_element_type=F32)


def _dot_t(a, b_t):
    return lax.dot_general(a, b_t, (((1,), (1,)), ((), ())), preferred_element_type=F32)


def _resident(shape):
    return pl.BlockSpec(shape, lambda *_: (0,) * len(shape), pipeline_mode=pl.Buffered(1))


def _params(n_axes):
    return pltpu.CompilerParams(
        dimension_semantics=("arbitrary",) * n_axes, vmem_limit_bytes=VMEM_LIMIT_BYTES)


def _rope_table_kernel(pos_ref, freq_ref, cos_ref, sin_ref):
    ang = pos_ref[...] * freq_ref[...]
    cos_ref[...] = jnp.cos(ang)
    sin_ref[...] = jnp.sin(ang)


def _rope_tables(positions):
    n_tok = positions.size
    per_row = LANES // HALF_ROPE
    inv_freq = ROPE_THETA ** (-jnp.arange(0, QK_ROPE_DIM, 2, dtype=F32) / QK_ROPE_DIM)
    pos_rep = jnp.repeat(positions.astype(F32).reshape(n_tok // per_row, per_row), HALF_ROPE, axis=1)
    freq = jnp.tile(inv_freq, per_row).reshape(1, LANES)
    shape = jax.ShapeDtypeStruct((n_tok // per_row, LANES), F32)
    cos, sin = pl.pallas_call(
        _rope_table_kernel, out_shape=(shape, shape), name="rope_tables")(pos_rep, freq)
    cos = cos.reshape(n_tok, HALF_ROPE)
    sin = sin.reshape(n_tok, HALF_ROPE)
    return jnp.concatenate([cos, cos, -sin, sin], axis=-1)


def _rope_dup(x, cs):
    y = x * cs
    return y + pltpu.roll(y, LANES // 2, 1)


def _inproj_kernel(x_ref, g_ref, cs_ref, wlat_ref, qg_ref, kvg_ref, wuq_ref, wukv_ref,
                   a_ref, q_ref, kn_ref, v_ref, kpe_ref):
    a = _rms(x_ref[0], g_ref[...]).astype(BF16)
    a_ref[0] = a
    z = _dot_t(a, wlat_ref[...])
    qn = _rms(z[:, :Q_LORA_RANK], qg_ref[...]).astype(BF16)
    kvn = _rms(z[:, Q_LORA_RANK:Q_LORA_RANK + KV_LORA_RANK], kvg_ref[...]).astype(BF16)
    cs = cs_ref[0]
    kpe = _rope_dup(z[:, Q_LORA_RANK + KV_LORA_RANK:], cs)
    lane = lax.broadcasted_iota(jnp.int32, kpe.shape, 1)
    kpe_ref[0] = jnp.where(lane < QK_ROPE_DIM, kpe, 0.0).astype(BF16)

    heads_per_dot = 4
    width = heads_per_dot * LANES
    nope_cols = N_HEADS * QK_NOPE_DIM
    for hg in range(N_HEADS // heads_per_dot):
        c0 = hg * width
        q_nope = _dot(qn, wuq_ref[:, c0:c0 + width])
        q_pe = _dot(qn, wuq_ref[:, nope_cols + c0:nope_cols + c0 + width])
        k_nope = _dot(kvn, wukv_ref[:, c0:c0 + width])
        val = _dot(kvn, wukv_ref[:, nope_cols + c0:nope_cols + c0 + width])
        for hh in range(heads_per_dot):
            h = hg * heads_per_dot + hh
            sl = slice(hh * LANES, (hh + 1) * LANES)
            q_ref[0, h, :, 0:LANES] = q_nope[:, sl].astype(BF16)
            q_ref[0, h, :, LANES:2 * LANES] = _rope_dup(q_pe[:, sl], cs).astype(BF16)
            kn_ref[0, h] = k_nope[:, sl].astype(BF16)
            v_ref[0, h] = val[:, sl].astype(BF16)


def _inproj(x, norm_g, cs, w_lat_t, q_g, kv_g, w_uq, w_ukv, tm):
    B, S, D = x.shape
    row = lambda b, i: (b, i, 0)
    head = lambda b, i: (b, 0, i, 0)
    return pl.pallas_call(
        _inproj_kernel,
        grid=(B, S // tm),
        in_specs=[
            pl.BlockSpec((1, tm, D), row),
            _resident((1, D)),
            pl.BlockSpec((1, tm, LANES), row),
            _resident(w_lat_t.shape),
            _resident((1, Q_LORA_RANK)),
            _resident((1, KV_LORA_RANK)),
            _resident(w_uq.shape),
            _resident(w_ukv.shape),
        ],
        out_specs=[
            pl.BlockSpec((1, tm, D), row),
            pl.BlockSpec((1, N_HEADS, tm, 2 * LANES), head),
            pl.BlockSpec((1, N_HEADS, tm, LANES), head),
            pl.BlockSpec((1, N_HEADS, tm, LANES), head),
            pl.BlockSpec((1, tm, LANES), row),
        ],
        out_shape=[
            jax.ShapeDtypeStruct((B, S, D), BF16),
            jax.ShapeDtypeStruct((B, N_HEADS, S, 2 * LANES), BF16),
            jax.ShapeDtypeStruct((B, N_HEADS, S, LANES), BF16),
            jax.ShapeDtypeStruct((B, N_HEADS, S, LANES), BF16),
            jax.ShapeDtypeStruct((B, S, LANES), BF16),
        ],
        compiler_params=_params(2),
        name="inproj",
    )(x, norm_g, cs, w_lat_t, q_g, kv_g, w_uq, w_ukv)


def _sgu_kernel(a_ref, wuv_ref, sg_ref, ws_ref, bfull_ref, wos_ref, wg1_ref, bg1_ref, m_ref):
    a = a_ref[...]
    tm = a.shape[0]
    n_chunks = tm // CHUNK
    uv = jax.nn.gelu(_dot_t(a, wuv_ref[...]))
    u = uv[:, :SGU_WIDTH]
    vn = _rms(uv[:, SGU_WIDTH:], sg_ref[...]).astype(BF16)
    t_idx = lax.broadcasted_iota(jnp.int32, (CHUNK, CHUNK), 0)
    s_idx = lax.broadcasted_iota(jnp.int32, (CHUNK, CHUNK), 1)
    causal = t_idx >= s_idx
    mixed_cols = []
    for g in range(SGU_GROUPS):
        ws = jnp.where(causal, ws_ref[g], 0.0).astype(BF16)
        gs = slice(g * SGU_GROUP_DIM, (g + 1) * SGU_GROUP_DIM)
        rhs = jnp.concatenate([vn[c * CHUNK:(c + 1) * CHUNK, gs] for c in range(n_chunks)], axis=1)
        mixed_cols.append(_dot(ws, rhs))
    bfull = bfull_ref[...]
    rows = []
    for c in range(n_chunks):
        cs = slice(c * SGU_GROUP_DIM, (c + 1) * SGU_GROUP_DIM)
        mixed = jnp.concatenate([mixed_cols[g][:, cs] for g in range(SGU_GROUPS)], axis=1)
        rows.append(u[c * CHUNK:(c + 1) * CHUNK] * (mixed + bfull))
    sgu_out = jnp.concatenate(rows, axis=0).astype(BF16)
    y_sgu = _dot(sgu_out, wos_ref[...])
    gate = jax.nn.sigmoid(_dot_t(a, wg1_ref[...]) + bg1_ref[...])
    m_ref[...] = gate * y_sgu


def _sgu_branch(a, w_uv, sgu_g, w_s, b_full, w_o_sgu, w_g1, b_g1, tm):
    T, D = a.shape
    row = lambda i: (i, 0)
    return pl.pallas_call(
        _sgu_kernel,
        grid=(T // tm,),
        in_specs=[
            pl.BlockSpec((tm, D), row),
            _resident(w_uv.shape),
            _resident(sgu_g.shape),
            _resident(w_s.shape),
            _resident(b_full.shape),
            _resident(w_o_sgu.shape),
            _resident(w_g1.shape),
            _resident(b_g1.shape),
        ],
        out_specs=pl.BlockSpec((tm, D), row),
        out_shape=jax.ShapeDtypeStruct((T, D), F32),
        compiler_params=_params(1),
        name="sgu_branch",
    )(a, w_uv, sgu_g, w_s, b_full, w_o_sgu, w_g1, b_g1)


def _attn_kernel(q_ref, kn_ref, kpe_ref, v_ref, *rest, tq, scale, n_cast):
    cast_in, (o_ref,), cast_out, (kf_ref,) = (
        rest[:n_cast], rest[n_cast:n_cast + 1], rest[n_cast + 1:2 * n_cast + 1], rest[2 * n_cast + 1:])
    for src, dst in zip(cast_in, cast_out):
        dst[...] = src[...].astype(dst.dtype)
    seq = q_ref.shape[2]
    row = lax.broadcasted_iota(jnp.int32, (tq, tq), 0)
    col = lax.broadcasted_iota(jnp.int32, (tq, tq), 1)
    causal = row >= col
    neg = jnp.finfo(F32).min
    kf_ref[:, 0:LANES] = kn_ref[0, 0]
    kf_ref[:, LANES:2 * LANES] = kpe_ref[0]

    for qi in range(seq // tq):
        q0 = qi * tq
        k_len = q0 + tq
        q = q_ref[0, 0, q0:k_len, :]
        s = lax.dot_general(q, kf_ref[0:k_len, :], (((1,), (1,)), ((), ())),
                            preferred_element_type=F32) * (scale * LOG2_E)
        s_diag = jnp.where(causal, s[:, q0:k_len], neg)
        s = jnp.concatenate([s[:, 0:q0], s_diag], axis=1) if qi else s_diag
        m = jnp.max(s, axis=-1, keepdims=True)
        p = jnp.exp2(s - m)
        l = jnp.sum(p, axis=-1, keepdims=True)
        acc = _dot(p.astype(BF16), v_ref[0, 0, 0:k_len, :])
        o_ref[0, q0:k_len, :] = (acc / l).astype(o_ref.dtype)


def _cast_block_specs(n_rows, n_cols, row0, n_steps, n_heads):
    share = 1 if (n_rows // n_steps) % BF16_SUBLANES == 0 else 2
    blk = n_rows * share // n_steps
    assert blk * n_steps == n_rows * share and blk % BF16_SUBLANES == 0 and row0 % blk == 0
    first = row0 // blk
    step = lambda b, h: (b * n_heads + h) // share
    return (pl.BlockSpec((blk, n_cols), lambda b, h: (first + step(b, h), 0)),
            pl.BlockSpec((blk, n_cols), lambda b, h: (step(b, h), 0)))


def _attention(q, k_nope, k_pe, v, casts, tq):
    B, H, S, _ = q.shape
    scale = (QK_NOPE_DIM + QK_ROPE_DIM) ** -0.5
    specs = [_cast_block_specs(n, w.shape[1], r0, B * H, H) for w, r0, n in casts]
    outs = pl.pallas_call(
        functools.partial(_attn_kernel, tq=tq, scale=scale, n_cast=len(casts)),
        grid=(B, H),
        in_specs=[
            pl.BlockSpec((1, 1, S, 2 * LANES), lambda b, h: (b, h, 0, 0)),
            pl.BlockSpec((1, 1, S, LANES), lambda b, h: (b, h, 0, 0)),
            pl.BlockSpec((1, S, LANES), lambda b, h: (b, 0, 0)),
            pl.BlockSpec((1, 1, S, LANES), lambda b, h: (b, h, 0, 0)),
        ] + [s_in for s_in, _ in specs],
        out_specs=[pl.BlockSpec((1, S, LANES), lambda b, h: (b, 0, h))]
        + [s_out for _, s_out in specs],
        out_shape=[jax.ShapeDtypeStruct((B, S, H * V_HEAD_DIM), BF16)]
        + [jax.ShapeDtypeStruct((n, w.shape[1]), BF16) for w, _, n in casts],
        scratch_shapes=[pltpu.VMEM((S, 2 * LANES), BF16)],
        compiler_params=_params(2),
        name="mla_attention",
    )(q, k_nope, k_pe, v, *[w for w, _, _ in casts])
    return outs[0], outs[1:]


def _merge_kernel(attn_ref, a_ref, m_ref, x_ref, woa_ref, wg0_ref, bg0_ref, wout_ref, fg_ref,
                  h_ref, f_ref):
    y_attn = _dot(attn_ref[...], woa_ref[...])
    gate = jax.nn.sigmoid(_dot_t(a_ref[...], wg0_ref[...]) + bg0_ref[...])
    merged = (gate * y_attn + m_ref[...]).astype(BF16)
    h = x_ref[...] + _dot(merged, wout_ref[...])
    h_ref[...] = h
    f_ref[...] = _rms(h, fg_ref[...]).astype(BF16)


def _merge(attn, a, m_sgu, x, w_o_attn, w_g0, b_g0, w_out, ffn_g, tm):
    T, D = x.shape
    row = lambda i: (i, 0)
    tile = pl.BlockSpec((tm, D), row)
    return pl.pallas_call(
        _merge_kernel,
        grid=(T // tm,),
        in_specs=[tile, tile, tile, tile,
                  _resident(w_o_attn.shape), _resident(w_g0.shape), _resident(b_g0.shape),
                  _resident(w_out.shape), _resident(ffn_g.shape)],
        out_specs=[tile, tile],
        out_shape=[jax.ShapeDtypeStruct((T, D), F32), jax.ShapeDtypeStruct((T, D), BF16)],
        compiler_params=_params(1),
        name="merge_outproj",
    )(attn, a, m_sgu, x, w_o_attn, w_g0, b_g0, w_out, ffn_g)


def _ffn_kernel(f_ref, h_ref, wg_ref, wu_ref, wd_ref, ng_ref, o_ref):
    j = pl.program_id(1)

    @pl.when(j == 0)
    def _():
        o_ref[...] = h_ref[...]

    f = f_ref[...]
    gate = _dot(f, wg_ref[...])
    up = _dot(f, wu_ref[...])
    act = (jax.nn.silu(gate) * up).astype(BF16)
    o_ref[...] += _dot(act, wd_ref[...])

    @pl.when(j == pl.num_programs(1) - 1)
    def _():
        o_ref[...] = _rms(o_ref[...], ng_ref[...])


def _ffn(f, h, w_gate, w_up, w_down, final_g, tm, tf):
    T, D = h.shape
    d_ff = w_gate.shape[1]
    row = lambda i, j: (i, 0)
    return pl.pallas_call(
        _ffn_kernel,
        grid=(T // tm, d_ff // tf),
        in_specs=[
            pl.BlockSpec((tm, D), row),
            pl.BlockSpec((tm, D), row, pipeline_mode=pl.Buffered(1)),
            pl.BlockSpec((D, tf), lambda i, j: (0, j)),
            pl.BlockSpec((D, tf), lambda i, j: (0, j)),
            pl.BlockSpec((tf, D), lambda i, j: (j, 0)),
            _resident((1, D)),
        ],
        out_specs=pl.BlockSpec((tm, D), row),
        out_shape=jax.ShapeDtypeStruct((T, D), F32),
        compiler_params=_params(2),
        name="swiglu_ffn",
    )(f, h, w_gate, w_up, w_down, final_g)


def _dup_rope_cols(w, axis=-1):
    x1, x2 = jnp.split(w, 2, axis=axis)
    return jnp.concatenate([x1, x2, x2, x1], axis=axis)


def kernel(x, positions, norm_mix_g, w_in, b_gate, q_norm_g, w_uq, kv_norm_g, w_ukv, w_o_attn,
           sgu_norm_g, w_sgu, b_sgu, w_o_sgu, w_out, norm_ffn_g, w_gate_ffn, w_up_ffn,
           w_down_ffn, norm_final_g):
    B, S, D = x.shape
    T = B * S
    depth = w_in.shape[0]
    assert depth == 1, "the final norm is fused into the FFN epilogue of a single layer"
    assert w_in.shape[1:] == (D, D_IN)

    cs = _rope_tables(positions).reshape(B, S, LANES)
    row_vec = lambda v: v.reshape(1, -1).astype(F32)

    h = x
    out = None
    for l in range(depth):
        w_in_t = jnp.swapaxes(w_in[l], 0, 1)
        w_lat_t = jnp.concatenate(
            [w_in_t[:KPE_OFF], _dup_rope_cols(w_in_t[KPE_OFF:UV_OFF], axis=0)], axis=0).astype(BF16)
        uq = w_uq[l].reshape(Q_LORA_RANK, N_HEADS, QK_NOPE_DIM + QK_ROPE_DIM)
        w_uq_p = jnp.concatenate(
            [uq[..., :QK_NOPE_DIM].reshape(Q_LORA_RANK, -1),
             _dup_rope_cols(uq[..., QK_NOPE_DIM:]).reshape(Q_LORA_RANK, -1)], axis=1).astype(BF16)
        ukv = w_ukv[l].reshape(KV_LORA_RANK, N_HEADS, QK_NOPE_DIM + V_HEAD_DIM)
        w_ukv_p = jnp.concatenate(
            [ukv[..., :QK_NOPE_DIM].reshape(KV_LORA_RANK, -1),
             ukv[..., QK_NOPE_DIM:].reshape(KV_LORA_RANK, -1)], axis=1).astype(BF16)
        b_full = jnp.repeat(b_sgu[l].T, SGU_GROUP_DIM, axis=1).astype(F32)

        a, q, k_nope, v, k_pe = _inproj(
            h, row_vec(norm_mix_g[l]), cs, w_lat_t, row_vec(q_norm_g[l]), row_vec(kv_norm_g[l]),
            w_uq_p, w_ukv_p, tm=256)
        a2 = a.reshape(T, D)
        whole = lambda w: (w, 0, w.shape[0])
        attn, (w_uv_t, w_g0_t, w_g1_t, w_os, w_oa, w_o, w_gf, w_uf, w_df) = _attention(
            q, k_nope, k_pe, v,
            [(w_in_t, UV_OFF, 2 * SGU_WIDTH), (w_in_t, GATE_OFF, D), (w_in_t, GATE_OFF + D, D),
             whole(w_o_sgu[l]), whole(w_o_attn[l]), whole(w_out[l]),
             whole(w_gate_ffn[l]), whole(w_up_ffn[l]), whole(w_down_ffn[l])], tq=256)
        m_sgu = _sgu_branch(
            a2, w_uv_t, row_vec(sgu_norm_g[l]), w_sgu[l], b_full, w_os, w_g1_t,
            row_vec(b_gate[l, D:]), tm=256)
        h_mid, f = _merge(
            attn.reshape(T, D), a2, m_sgu, h.reshape(T, D), w_oa, w_g0_t,
            row_vec(b_gate[l, :D]), w_o, row_vec(norm_ffn_g[l]), tm=256)
        out = _ffn(f, h_mid, w_gf, w_uf, w_df, row_vec(norm_final_g), tm=1024, tf=512)
        h = out.reshape(B, S, D)
    return h
```

```python
import functools

import jax
import jax.numpy as jnp
from jax import lax
from jax.experimental import pallas as pl
from jax.experimental.pallas import tpu as pltpu

D_MODEL = 2048
N_HEADS = 16
QK_NOPE_DIM = 128
QK_ROPE_DIM = 64
V_HEAD_DIM = 128
Q_LORA_RANK = 512
KV_LORA_RANK = 512
ROPE_THETA = 10000.0
SGU_GROUPS = 8
SGU_GROUP_DIM = 128
SGU_WIDTH = SGU_GROUPS * SGU_GROUP_DIM
CHUNK = 128
N_BRANCH = 2
RMS_EPS = 1e-6
KPE_OFF = Q_LORA_RANK + KV_LORA_RANK
UV_OFF = KPE_OFF + QK_ROPE_DIM
GATE_OFF = UV_OFF + 2 * SGU_WIDTH
D_IN = GATE_OFF + N_BRANCH * D_MODEL
LANES = 128
HALF_ROPE = QK_ROPE_DIM // 2
LOG2_E = 1.4426950408889634
BF16_SUBLANES = 16

VMEM_LIMIT_BYTES = 56 * 1024 * 1024

F32 = jnp.float32
BF16 = jnp.bfloat16


def _rms(x, g):
    return x * lax.rsqrt(jnp.mean(x * x, axis=-1, keepdims=True) + RMS_EPS) * g


def _dot(a, b):
    return jnp.dot(a, b, preferred_element_type=F32)


def _dot_t(a, b_t):
    return lax.dot_general(a, b_t, (((1,), (1,)), ((), ())), preferred_element_type=F32)


def _resident(shape):
    return pl.BlockSpec(shape, lambda *_: (0,) * len(shape), pipeline_mode=pl.Buffered(1))


def _params(n_axes):
    return pltpu.CompilerParams(
        dimension_semantics=("arbitrary",) * n_axes, vmem_limit_bytes=VMEM_LIMIT_BYTES)


def _rope_table_kernel(pos_ref, freq_ref, cos_ref, sin_ref):
    ang = pos_ref[...] * freq_ref[...]
    cos_ref[...] = jnp.cos(ang)
    sin_ref[...] = jnp.sin(ang)


def _rope_tables(positions):
    n_tok = positions.size
    per_row = LANES // HALF_ROPE
    inv_freq = ROPE_THETA ** (-jnp.arange(0, QK_ROPE_DIM, 2, dtype=F32) / QK_ROPE_DIM)
    pos_rep = jnp.repeat(positions.astype(F32).reshape(n_tok // per_row, per_row), HALF_ROPE, axis=1)
    freq = jnp.tile(inv_freq, per_row).reshape(1, LANES)
    shape = jax.ShapeDtypeStruct((n_tok // per_row, LANES), F32)
    cos, sin = pl.pallas_call(
        _rope_table_kernel, out_shape=(shape, shape), name="rope_tables")(pos_rep, freq)
    cos = cos.reshape(n_tok, HALF_ROPE)
    sin = sin.reshape(n_tok, HALF_ROPE)
    return jnp.concatenate([cos, cos, -sin, sin], axis=-1)


def _rope_dup(x, cs):
    y = x * cs
    return y + pltpu.roll(y, LANES // 2, 1)


def _dup_rope_lanes(src):
    lane = lax.broadcasted_iota(jnp.int32, src.shape, 1)
    return jnp.where(lane < 2 * HALF_ROPE, src,
                     jnp.where(lane < 3 * HALF_ROPE, pltpu.roll(src, HALF_ROPE, 1),
                               pltpu.roll(src, 3 * HALF_ROPE, 1)))


def _prep_kernel(wlat_ref, wuq_ref, wukv_ref, olat_ref, ouq_ref, oukv_ref):
    olat_ref[0:KPE_OFF] = wlat_ref[0:KPE_OFF].astype(BF16)
    x1 = wlat_ref[KPE_OFF:KPE_OFF + HALF_ROPE].astype(BF16)
    x2 = wlat_ref[KPE_OFF + HALF_ROPE:UV_OFF].astype(BF16)
    for k, part in enumerate((x1, x2, x2, x1)):
        olat_ref[KPE_OFF + k * HALF_ROPE:KPE_OFF + (k + 1) * HALF_ROPE] = part
    oukv_ref[...] = wukv_ref[...].astype(BF16)
    half = LANES // 2
    nope_cols = N_HEADS * QK_NOPE_DIM
    lane = lax.broadcasted_iota(jnp.int32, (wuq_ref.shape[0], LANES), 1)
    for pair in range(N_HEADS // 2):
        t0, t1, t2 = (wuq_ref[:, (3 * pair + k) * LANES:(3 * pair + k + 1) * LANES] for k in range(3))
        r1 = pltpu.roll(t1, half, 1)
        r2 = pltpu.roll(t2, half, 1)
        heads = ((2 * pair, t0, t1), (2 * pair + 1, jnp.where(lane < half, r1, r2), r2))
        for h, nope, rope_src in heads:
            ouq_ref[:, h * LANES:(h + 1) * LANES] = nope.astype(BF16)
            ouq_ref[:, nope_cols + h * LANES:nope_cols + (h + 1) * LANES] = (
                _dup_rope_lanes(rope_src).astype(BF16))


def _prep_inproj_weights(w_in_t, w_uq, w_ukv, n_chunks=4):
    d = w_in_t.shape[1]
    r_q, r_kv = w_uq.shape[0], w_ukv.shape[0]
    lat_rows = UV_OFF + LANES - QK_ROPE_DIM
    uq_cols = N_HEADS * (QK_NOPE_DIM + LANES)
    return pl.pallas_call(
        _prep_kernel,
        grid=(n_chunks,),
        in_specs=[
            pl.BlockSpec((UV_OFF, d // n_chunks), lambda c: (0, c)),
            pl.BlockSpec((r_q // n_chunks, w_uq.shape[1]), lambda c: (c, 0)),
            pl.BlockSpec((r_kv // n_chunks, w_ukv.shape[1]), lambda c: (c, 0)),
        ],
        out_specs=[
            pl.BlockSpec((lat_rows, d // n_chunks), lambda c: (0, c)),
            pl.BlockSpec((r_q // n_chunks, uq_cols), lambda c: (c, 0)),
            pl.BlockSpec((r_kv // n_chunks, w_ukv.shape[1]), lambda c: (c, 0)),
        ],
        out_shape=[
            jax.ShapeDtypeStruct((lat_rows, d), BF16),
            jax.ShapeDtypeStruct((r_q, uq_cols), BF16),
            jax.ShapeDtypeStruct(w_ukv.shape, BF16),
        ],
        compiler_params=_params(1),
        name="prep_inproj_weights",
    )(w_in_t, w_uq, w_ukv)


def _inproj_kernel(x_ref, g_ref, cs_ref, wlat_ref, qg_ref, kvg_ref, wuq_ref, wukv_ref,
                   a_ref, q_ref, kn_ref, v_ref, kpe_ref):
    a = _rms(x_ref[0], g_ref[...]).astype(BF16)
    a_ref[0] = a
    z = _dot_t(a, wlat_ref[...])
    qn = _rms(z[:, :Q_LORA_RANK], qg_ref[...]).astype(BF16)
    kvn = _rms(z[:, Q_LORA_RANK:Q_LORA_RANK + KV_LORA_RANK], kvg_ref[...]).astype(BF16)
    cs = cs_ref[0]
    kpe = _rope_dup(z[:, Q_LORA_RANK + KV_LORA_RANK:], cs)
    lane = lax.broadcasted_iota(jnp.int32, kpe.shape, 1)
    kpe_ref[0] = jnp.where(lane < QK_ROPE_DIM, kpe, 0.0).astype(BF16)

    heads_per_dot = 4
    width = heads_per_dot * LANES
    nope_cols = N_HEADS * QK_NOPE_DIM
    for hg in range(N_HEADS // heads_per_dot):
        c0 = hg * width
        q_nope = _dot(qn, wuq_ref[:, c0:c0 + width])
        q_pe = _dot(qn, wuq_ref[:, nope_cols + c0:nope_cols + c0 + width])
        kv0 = 2 * c0
        kv_a = _dot(kvn, wukv_ref[:, kv0:kv0 + width])
        kv_b = _dot(kvn, wukv_ref[:, kv0 + width:kv0 + 2 * width])
        for hh in range(heads_per_dot):
            h = hg * heads_per_dot + hh
            sl = slice(hh * LANES, (hh + 1) * LANES)
            q_ref[0, h, :, 0:LANES] = q_nope[:, sl].astype(BF16)
            q_ref[0, h, :, LANES:2 * LANES] = _rope_dup(q_pe[:, sl], cs).astype(BF16)
            kv = kv_a if hh < heads_per_dot // 2 else kv_b
            k0 = (hh % (heads_per_dot // 2)) * 2 * LANES
            kn_ref[0, h] = kv[:, k0:k0 + LANES].astype(BF16)
            v_ref[0, h] = kv[:, k0 + LANES:k0 + 2 * LANES].astype(BF16)


def _inproj(x, norm_g, cs, w_lat_t, q_g, kv_g, w_uq, w_ukv, tm):
    B, S, D = x.shape
    row = lambda b, i: (b, i, 0)
    head = lambda b, i: (b, 0, i, 0)
    return pl.pallas_call(
        _inproj_kernel,
        grid=(B, S // tm),
        in_specs=[
            pl.BlockSpec((1, tm, D), row),
            _resident((1, D)),
            pl.BlockSpec((1, tm, LANES), row),
            _resident(w_lat_t.shape),
            _resident((1, Q_LORA_RANK)),
            _resident((1, KV_LORA_RANK)),
            _resident(w_uq.shape),
            _resident(w_ukv.shape),
        ],
        out_specs=[
            pl.BlockSpec((1, tm, D), row),
            pl.BlockSpec((1, N_HEADS, tm, 2 * LANES), head),
            pl.BlockSpec((1, N_HEADS, tm, LANES), head),
            pl.BlockSpec((1, N_HEADS, tm, LANES), head),
            pl.BlockSpec((1, tm, LANES), row),
        ],
        out_shape=[
            jax.ShapeDtypeStruct((B, S, D), BF16),
            jax.ShapeDtypeStruct((B, N_HEADS, S, 2 * LANES), BF16),
            jax.ShapeDtypeStruct((B, N_HEADS, S, LANES), BF16),
            jax.ShapeDtypeStruct((B, N_HEADS, S, LANES), BF16),
            jax.ShapeDtypeStruct((B, S, LANES), BF16),
        ],
        compiler_params=_params(2),
        name="inproj",
    )(x, norm_g, cs, w_lat_t, q_g, kv_g, w_uq, w_ukv)


def _sgu_kernel(a_ref, wuv_ref, sg_ref, ws_ref, bfull_ref, wos_ref, wg1_ref, bg1_ref, m_ref):
    a = a_ref[...]
    tm = a.shape[0]
    n_chunks = tm // CHUNK
    uv = jax.nn.gelu(_dot_t(a, wuv_ref[...]))
    u = uv[:, :SGU_WIDTH]
    vn = _rms(uv[:, SGU_WIDTH:], sg_ref[...]).astype(BF16)
    t_idx = lax.broadcasted_iota(jnp.int32, (CHUNK, CHUNK), 0)
    s_idx = lax.broadcasted_iota(jnp.int32, (CHUNK, CHUNK), 1)
    causal = t_idx >= s_idx
    mixed_cols = []
    for g in range(SGU_GROUPS):
        ws = jnp.where(causal, ws_ref[g], 0.0).astype(BF16)
        gs = slice(g * SGU_GROUP_DIM, (g + 1) * SGU_GROUP_DIM)
        rhs = jnp.concatenate([vn[c * CHUNK:(c + 1) * CHUNK, gs] for c in range(n_chunks)], axis=1)
        mixed_cols.append(_dot(ws, rhs))
    bfull = bfull_ref[...]
    rows = []
    for c in range(n_chunks):
        cs = slice(c * SGU_GROUP_DIM, (c + 1) * SGU_GROUP_DIM)
        mixed = jnp.concatenate([mixed_cols[g][:, cs] for g in range(SGU_GROUPS)], axis=1)
        rows.append(u[c * CHUNK:(c + 1) * CHUNK] * (mixed + bfull))
    sgu_out = jnp.concatenate(rows, axis=0).astype(BF16)
    y_sgu = _dot(sgu_out, wos_ref[...])
    gate = jax.nn.sigmoid(_dot_t(a, wg1_ref[...]) + bg1_ref[...])
    m_ref[...] = gate * y_sgu


def _sgu_branch(a, w_uv, sgu_g, w_s, b_full, w_o_sgu, w_g1, b_g1, tm):
    T, D = a.shape
    row = lambda i: (i, 0)
    return pl.pallas_call(
        _sgu_kernel,
        grid=(T // tm,),
        in_specs=[
            pl.BlockSpec((tm, D), row),
            _resident(w_uv.shape),
            _resident(sgu_g.shape),
            _resident(w_s.shape),
            _resident(b_full.shape),
            _resident(w_o_sgu.shape),
            _resident(w_g1.shape),
            _resident(b_g1.shape),
        ],
        out_specs=pl.BlockSpec((tm, D), row),
        out_shape=jax.ShapeDtypeStruct((T, D), F32),
        compiler_params=_params(1),
        name="sgu_branch",
    )(a, w_uv, sgu_g, w_s, b_full, w_o_sgu, w_g1, b_g1)


def _attn_kernel(q_ref, kn_ref, kpe_ref, v_ref, *rest, tq, scale, n_cast):
    cast_in, (o_ref,), cast_out, (kf_ref,) = (
        rest[:n_cast], rest[n_cast:n_cast + 1], rest[n_cast + 1:2 * n_cast + 1], rest[2 * n_cast + 1:])
    for src, dst in zip(cast_in, cast_out):
        dst[...] = src[...].astype(dst.dtype)
    seq = q_ref.shape[2]
    row = lax.broadcasted_iota(jnp.int32, (tq, tq), 0)
    col = lax.broadcasted_iota(jnp.int32, (tq, tq), 1)
    causal = row >= col
    neg = jnp.finfo(F32).min
    kf_ref[:, 0:LANES] = kn_ref[0, 0]
    kf_ref[:, LANES:2 * LANES] = kpe_ref[0]

    for qi in range(seq // tq):
        q0 = qi * tq
        k_len = q0 + tq
        q = q_ref[0, 0, q0:k_len, :]
        s = lax.dot_general(q, kf_ref[0:k_len, :], (((1,), (1,)), ((), ())),
                            preferred_element_type=F32) * (scale * LOG2_E)
        s_diag = jnp.where(causal, s[:, q0:k_len], neg)
        s = jnp.concatenate([s[:, 0:q0], s_diag], axis=1) if qi else s_diag
        m = jnp.max(s, axis=-1, keepdims=True)
        p = jnp.exp2(s - m)
        l = jnp.sum(p, axis=-1, keepdims=True)
        acc = _dot(p.astype(BF16), v_ref[0, 0, 0:k_len, :])
        o_ref[0, q0:k_len, :] = (acc / l).astype(o_ref.dtype)


def _cast_block_specs(n_rows, n_cols, row0, n_steps, n_heads):
    share = 1 if (n_rows // n_steps) % BF16_SUBLANES == 0 else 2
    blk = n_rows * share // n_steps
    assert blk * n_steps == n_rows * share and blk % BF16_SUBLANES == 0 and row0 % blk == 0
    first = row0 // blk
    step = lambda b, h: (b * n_heads + h) // share
    return (pl.BlockSpec((blk, n_cols), lambda b, h: (first + step(b, h), 0)),
            pl.BlockSpec((blk, n_cols), lambda b, h: (step(b, h), 0)))


def _attention(q, k_nope, k_pe, v, casts, tq):
    B, H, S, _ = q.shape
    scale = (QK_NOPE_DIM + QK_ROPE_DIM) ** -0.5
    specs = [_cast_block_specs(n, w.shape[1], r0, B * H, H) for w, r0, n in casts]
    outs = pl.pallas_call(
        functools.partial(_attn_kernel, tq=tq, scale=scale, n_cast=len(casts)),
        grid=(B, H),
        in_specs=[
            pl.BlockSpec((1, 1, S, 2 * LANES), lambda b, h: (b, h, 0, 0)),
            pl.BlockSpec((1, 1, S, LANES), lambda b, h: (b, h, 0, 0)),
            pl.BlockSpec((1, S, LANES), lambda b, h: (b, 0, 0)),
            pl.BlockSpec((1, 1, S, LANES), lambda b, h: (b, h, 0, 0)),
        ] + [s_in for s_in, _ in specs],
        out_specs=[pl.BlockSpec((1, S, LANES), lambda b, h: (b, 0, h))]
        + [s_out for _, s_out in specs],
        out_shape=[jax.ShapeDtypeStruct((B, S, H * V_HEAD_DIM), BF16)]
        + [jax.ShapeDtypeStruct((n, w.shape[1]), BF16) for w, _, n in casts],
        scratch_shapes=[pltpu.VMEM((S, 2 * LANES), BF16)],
        compiler_params=_params(2),
        name="mla_attention",
    )(q, k_nope, k_pe, v, *[w for w, _, _ in casts])
    return outs[0], outs[1:]


def _merge_kernel(attn_ref, a_ref, m_ref, x_ref, woa_ref, wg0_ref, bg0_ref, wout_ref, fg_ref,
                  h_ref, f_ref):
    y_attn = _dot(attn_ref[...], woa_ref[...])
    gate = jax.nn.sigmoid(_dot_t(a_ref[...], wg0_ref[...]) + bg0_ref[...])
    merged = (gate * y_attn + m_ref[...]).astype(BF16)
    h = x_ref[...] + _dot(merged, wout_ref[...])
    h_ref[...] = h
    f_ref[...] = _rms(h, fg_ref[...]).astype(BF16)


def _merge(attn, a, m_sgu, x, w_o_attn, w_g0, b_g0, w_out, ffn_g, tm):
    T, D = x.shape
    row = lambda i: (i, 0)
    tile = pl.BlockSpec((tm, D), row)
    return pl.pallas_call(
        _merge_kernel,
        grid=(T // tm,),
        in_specs=[tile, tile, tile, tile,
                  _resident(w_o_attn.shape), _resident(w_g0.shape), _resident(b_g0.shape),
                  _resident(w_out.shape), _resident(ffn_g.shape)],
        out_specs=[tile, tile],
        out_shape=[jax.ShapeDtypeStruct((T, D), F32), jax.ShapeDtypeStruct((T, D), BF16)],
        compiler_params=_params(1),
        name="merge_outproj",
    )(attn, a, m_sgu, x, w_o_attn, w_g0, b_g0, w_out, ffn_g)


def _ffn_kernel(f_ref, h_ref, wg_ref, wu_ref, wd_ref, ng_ref, o_ref):
    j = pl.program_id(1)

    @pl.when(j == 0)
    def _():
        o_ref[...] = h_ref[...]

    f = f_ref[...]
    gate = _dot(f, wg_ref[...])
    up = _dot(f, wu_ref[...])
    act = (jax.nn.silu(gate) * up).astype(BF16)
    o_ref[...] += _dot(act, wd_ref[...])

    @pl.when(j == pl.num_programs(1) - 1)
    def _():
        o_ref[...] = _rms(o_ref[...], ng_ref[...])


def _ffn(f, h, w_gate, w_up, w_down, final_g, tm, tf):
    T, D = h.shape
    d_ff = w_gate.shape[1]
    row = lambda i, j: (i, 0)
    return pl.pallas_call(
        _ffn_kernel,
        grid=(T // tm, d_ff // tf),
        in_specs=[
            pl.BlockSpec((tm, D), row),
            pl.BlockSpec((tm, D), row),
            pl.BlockSpec((D, tf), lambda i, j: (0, j)),
            pl.BlockSpec((D, tf), lambda i, j: (0, j)),
            pl.BlockSpec((tf, D), lambda i, j: (j, 0)),
            _resident((1, D)),
        ],
        out_specs=pl.BlockSpec((tm, D), row),
        out_shape=jax.ShapeDtypeStruct((T, D), F32),
        compiler_params=_params(2),
        name="swiglu_ffn",
    )(f, h, w_gate, w_up, w_down, final_g)


def kernel(x, positions, norm_mix_g, w_in, b_gate, q_norm_g, w_uq, kv_norm_g, w_ukv, w_o_attn,
           sgu_norm_g, w_sgu, b_sgu, w_o_sgu, w_out, norm_ffn_g, w_gate_ffn, w_up_ffn,
           w_down_ffn, norm_final_g):
    B, S, D = x.shape
    T = B * S
    depth = w_in.shape[0]
    assert depth == 1, "the final norm is fused into the FFN epilogue of a single layer"
    assert w_in.shape[1:] == (D, D_IN)

    cs = _rope_tables(positions).reshape(B, S, LANES)
    row_vec = lambda v: v.reshape(1, -1).astype(F32)

    h = x
    out = None
    for l in range(depth):
        w_in_t = jnp.swapaxes(w_in[l], 0, 1)
        w_lat_t, w_uq_p, w_ukv_p = _prep_inproj_weights(w_in_t, w_uq[l], w_ukv[l])
        b_full = jnp.repeat(b_sgu[l].T, SGU_GROUP_DIM, axis=1).astype(F32)

        a, q, k_nope, v, k_pe = _inproj(
            h, row_vec(norm_mix_g[l]), cs, w_lat_t, row_vec(q_norm_g[l]), row_vec(kv_norm_g[l]),
            w_uq_p, w_ukv_p, tm=256)
        a2 = a.reshape(T, D)
        whole = lambda w: (w, 0, w.shape[0])
        attn, (w_uv_t, w_g0_t, w_g1_t, w_os, w_oa, w_o, w_gf, w_uf, w_df) = _attention(
            q, k_nope, k_pe, v,
            [(w_in_t, UV_OFF, 2 * SGU_WIDTH), (w_in_t, GATE_OFF, D), (w_in_t, GATE_OFF + D, D),
             whole(w_o_sgu[l]), whole(w_o_attn[l]), whole(w_out[l]),
             whole(w_gate_ffn[l]), whole(w_up_ffn[l]), whole(w_down_ffn[l])], tq=256)
        m_sgu = _sgu_branch(
            a2, w_uv_t, row_vec(sgu_norm_g[l]), w_sgu[l], b_full, w_os, w_g1_t,
            row_vec(b_gate[l, D:]), tm=256)
        h_mid, f = _merge(
            attn.reshape(T, D), a2, m_sgu, h.reshape(T, D), w_oa, w_g0_t,
            row_vec(b_gate[l, :D]), w_o, row_vec(norm_ffn_g[l]), tm=256)
        out = _ffn(f, h_mid, w_gf, w_uf, w_df, row_vec(norm_final_g), tm=1024, tf=256)
        h = out.reshape(B, S, D)
    return h
```

```python
import functools

import jax
import jax.numpy as jnp
from jax import lax
from jax.experimental import pallas as pl
from jax.experimental.pallas import tpu as pltpu

D_MODEL = 2048
N_HEADS = 16
QK_NOPE_DIM = 128
QK_ROPE_DIM = 64
V_HEAD_DIM = 128
Q_LORA_RANK = 512
KV_LORA_RANK = 512
ROPE_THETA = 10000.0
SGU_GROUPS = 8
SGU_GROUP_DIM = 128
SGU_WIDTH = SGU_GROUPS * SGU_GROUP_DIM
CHUNK = 128
N_BRANCH = 2
RMS_EPS = 1e-6
KPE_OFF = Q_LORA_RANK + KV_LORA_RANK
UV_OFF = KPE_OFF + QK_ROPE_DIM
GATE_OFF = UV_OFF + 2 * SGU_WIDTH
D_IN = GATE_OFF + N_BRANCH * D_MODEL
LANES = 128
HALF_ROPE = QK_ROPE_DIM // 2
LOG2_E = 1.4426950408889634
QK_LOG2_SCALE = (QK_NOPE_DIM + QK_ROPE_DIM) ** -0.5 * LOG2_E
BF16_SUBLANES = 16

VMEM_LIMIT_BYTES = 56 * 1024 * 1024

F32 = jnp.float32
BF16 = jnp.bfloat16


def _rms(x, g):
    return x * lax.rsqrt(jnp.mean(x * x, axis=-1, keepdims=True) + RMS_EPS) * g


def _dot(a, b):
    return jnp.dot(a, b, preferred_element_type=F32)


def _dot_t(a, b_t):
    return lax.dot_general(a, b_t, (((1,), (1,)), ((), ())), preferred_element_type=F32)


def _resident(shape):
    return pl.BlockSpec(shape, lambda *_: (0,) * len(shape), pipeline_mode=pl.Buffered(1))


def _params(n_axes):
    return pltpu.CompilerParams(
        dimension_semantics=("arbitrary",) * n_axes, vmem_limit_bytes=VMEM_LIMIT_BYTES)


def _rope_table_kernel(pos_ref, freq_ref, cos_ref, sin_ref):
    ang = pos_ref[...] * freq_ref[...]
    cos_ref[...] = jnp.cos(ang)
    sin_ref[...] = jnp.sin(ang)


def _rope_tables(positions):
    n_tok = positions.size
    per_row = LANES // HALF_ROPE
    inv_freq = ROPE_THETA ** (-jnp.arange(0, QK_ROPE_DIM, 2, dtype=F32) / QK_ROPE_DIM)
    pos_rep = jnp.repeat(positions.astype(F32).reshape(n_tok // per_row, per_row), HALF_ROPE, axis=1)
    freq = jnp.tile(inv_freq, per_row).reshape(1, LANES)
    shape = jax.ShapeDtypeStruct((n_tok // per_row, LANES), F32)
    cos, sin = pl.pallas_call(
        _rope_table_kernel, out_shape=(shape, shape), name="rope_tables")(pos_rep, freq)
    cos = cos.reshape(n_tok, HALF_ROPE)
    sin = sin.reshape(n_tok, HALF_ROPE)
    return jnp.concatenate([cos, cos, -sin, sin], axis=-1)


def _rope_dup(x, cs):
    y = x * cs
    return y + pltpu.roll(y, LANES // 2, 1)


def _dup_rope_lanes(src):
    lane = lax.broadcasted_iota(jnp.int32, src.shape, 1)
    return jnp.where(lane < 2 * HALF_ROPE, src,
                     jnp.where(lane < 3 * HALF_ROPE, pltpu.roll(src, HALF_ROPE, 1),
                               pltpu.roll(src, 3 * HALF_ROPE, 1)))


def _prep_kernel(wlat_ref, wuq_ref, wukv_ref, olat_ref, ouq_ref, oukv_ref):
    olat_ref[0:KPE_OFF] = wlat_ref[0:KPE_OFF].astype(BF16)
    x1 = wlat_ref[KPE_OFF:KPE_OFF + HALF_ROPE].astype(BF16)
    x2 = wlat_ref[KPE_OFF + HALF_ROPE:UV_OFF].astype(BF16)
    for k, part in enumerate((x1, x2, x2, x1)):
        olat_ref[KPE_OFF + k * HALF_ROPE:KPE_OFF + (k + 1) * HALF_ROPE] = part
    oukv_ref[...] = wukv_ref[...].astype(BF16)
    half = LANES // 2
    nope_cols = N_HEADS * QK_NOPE_DIM
    lane = lax.broadcasted_iota(jnp.int32, (wuq_ref.shape[0], LANES), 1)
    for pair in range(N_HEADS // 2):
        t0, t1, t2 = (wuq_ref[:, (3 * pair + k) * LANES:(3 * pair + k + 1) * LANES] for k in range(3))
        r1 = pltpu.roll(t1, half, 1)
        r2 = pltpu.roll(t2, half, 1)
        heads = ((2 * pair, t0, t1), (2 * pair + 1, jnp.where(lane < half, r1, r2), r2))
        for h, nope, rope_src in heads:
            ouq_ref[:, h * LANES:(h + 1) * LANES] = nope.astype(BF16)
            ouq_ref[:, nope_cols + h * LANES:nope_cols + (h + 1) * LANES] = (
                _dup_rope_lanes(rope_src).astype(BF16))


def _prep_inproj_weights(w_in_t, w_uq, w_ukv, n_chunks=4):
    d = w_in_t.shape[1]
    r_q, r_kv = w_uq.shape[0], w_ukv.shape[0]
    lat_rows = UV_OFF + LANES - QK_ROPE_DIM
    uq_cols = N_HEADS * (QK_NOPE_DIM + LANES)
    return pl.pallas_call(
        _prep_kernel,
        grid=(n_chunks,),
        in_specs=[
            pl.BlockSpec((UV_OFF, d // n_chunks), lambda c: (0, c)),
            pl.BlockSpec((r_q // n_chunks, w_uq.shape[1]), lambda c: (c, 0)),
            pl.BlockSpec((r_kv // n_chunks, w_ukv.shape[1]), lambda c: (c, 0)),
        ],
        out_specs=[
            pl.BlockSpec((lat_rows, d // n_chunks), lambda c: (0, c)),
            pl.BlockSpec((r_q // n_chunks, uq_cols), lambda c: (c, 0)),
            pl.BlockSpec((r_kv // n_chunks, w_ukv.shape[1]), lambda c: (c, 0)),
        ],
        out_shape=[
            jax.ShapeDtypeStruct((lat_rows, d), BF16),
            jax.ShapeDtypeStruct((r_q, uq_cols), BF16),
            jax.ShapeDtypeStruct(w_ukv.shape, BF16),
        ],
        compiler_params=_params(1),
        name="prep_inproj_weights",
    )(w_in_t, w_uq, w_ukv)


def _inproj_kernel(x_ref, g_ref, cs_ref, wlat_ref, qg_ref, kvg_ref, wuq_ref, wukv_ref,
                   a_ref, q_ref, kn_ref, v_ref, kpe_ref):
    a = _rms(x_ref[0], g_ref[...]).astype(BF16)
    a_ref[0] = a
    z = _dot_t(a, wlat_ref[...])
    qn = (_rms(z[:, :Q_LORA_RANK], qg_ref[...]) * QK_LOG2_SCALE).astype(BF16)
    kvn = _rms(z[:, Q_LORA_RANK:Q_LORA_RANK + KV_LORA_RANK], kvg_ref[...]).astype(BF16)
    cs = cs_ref[0]
    kpe = _rope_dup(z[:, Q_LORA_RANK + KV_LORA_RANK:], cs)
    lane = lax.broadcasted_iota(jnp.int32, kpe.shape, 1)
    kpe_ref[0] = jnp.where(lane < QK_ROPE_DIM, kpe, 0.0).astype(BF16)

    heads_per_dot = 4
    width = heads_per_dot * LANES
    nope_cols = N_HEADS * QK_NOPE_DIM
    for hg in range(N_HEADS // heads_per_dot):
        c0 = hg * width
        q_nope = _dot(qn, wuq_ref[:, c0:c0 + width])
        q_pe = _dot(qn, wuq_ref[:, nope_cols + c0:nope_cols + c0 + width])
        kv0 = 2 * c0
        kv_a = _dot(kvn, wukv_ref[:, kv0:kv0 + width])
        kv_b = _dot(kvn, wukv_ref[:, kv0 + width:kv0 + 2 * width])
        for hh in range(heads_per_dot):
            h = hg * heads_per_dot + hh
            sl = slice(hh * LANES, (hh + 1) * LANES)
            q_ref[0, h, :, 0:LANES] = q_nope[:, sl].astype(BF16)
            q_ref[0, h, :, LANES:2 * LANES] = _rope_dup(q_pe[:, sl], cs).astype(BF16)
            kv = kv_a if hh < heads_per_dot // 2 else kv_b
            k0 = (hh % (heads_per_dot // 2)) * 2 * LANES
            kn_ref[0, h] = kv[:, k0:k0 + LANES].astype(BF16)
            v_ref[0, h] = kv[:, k0 + LANES:k0 + 2 * LANES].astype(BF16)


def _inproj(x, norm_g, cs, w_lat_t, q_g, kv_g, w_uq, w_ukv, tm):
    B, S, D = x.shape
    row = lambda b, i: (b, i, 0)
    head = lambda b, i: (b, 0, i, 0)
    return pl.pallas_call(
        _inproj_kernel,
        grid=(B, S // tm),
        in_specs=[
            pl.BlockSpec((1, tm, D), row),
            _resident((1, D)),
            pl.BlockSpec((1, tm, LANES), row),
            _resident(w_lat_t.shape),
            _resident((1, Q_LORA_RANK)),
            _resident((1, KV_LORA_RANK)),
            _resident(w_uq.shape),
            _resident(w_ukv.shape),
        ],
        out_specs=[
            pl.BlockSpec((1, tm, D), row),
            pl.BlockSpec((1, N_HEADS, tm, 2 * LANES), head),
            pl.BlockSpec((1, N_HEADS, tm, LANES), head),
            pl.BlockSpec((1, N_HEADS, tm, LANES), head),
            pl.BlockSpec((1, tm, LANES), row),
        ],
        out_shape=[
            jax.ShapeDtypeStruct((B, S, D), BF16),
            jax.ShapeDtypeStruct((B, N_HEADS, S, 2 * LANES), BF16),
            jax.ShapeDtypeStruct((B, N_HEADS, S, LANES), BF16),
            jax.ShapeDtypeStruct((B, N_HEADS, S, LANES), BF16),
            jax.ShapeDtypeStruct((B, S, LANES), BF16),
        ],
        compiler_params=_params(2),
        name="inproj",
    )(x, norm_g, cs, w_lat_t, q_g, kv_g, w_uq, w_ukv)


def _sgu_kernel(a_ref, wuv_ref, sg_ref, ws_ref, bfull_ref, wos_ref, wg1_ref, bg1_ref, m_ref):
    a = a_ref[...]
    tm = a.shape[0]
    n_chunks = tm // CHUNK
    uv = jax.nn.gelu(_dot_t(a, wuv_ref[...]))
    u = uv[:, :SGU_WIDTH]
    vn = _rms(uv[:, SGU_WIDTH:], sg_ref[...]).astype(BF16)
    t_idx = lax.broadcasted_iota(jnp.int32, (CHUNK, CHUNK), 0)
    s_idx = lax.broadcasted_iota(jnp.int32, (CHUNK, CHUNK), 1)
    causal = t_idx >= s_idx
    mixed_cols = []
    for g in range(SGU_GROUPS):
        ws = jnp.where(causal, ws_ref[g], 0.0).astype(BF16)
        gs = slice(g * SGU_GROUP_DIM, (g + 1) * SGU_GROUP_DIM)
        rhs = jnp.concatenate([vn[c * CHUNK:(c + 1) * CHUNK, gs] for c in range(n_chunks)], axis=1)
        mixed_cols.append(_dot(ws, rhs))
    bfull = bfull_ref[...]
    rows = []
    for c in range(n_chunks):
        cs = slice(c * SGU_GROUP_DIM, (c + 1) * SGU_GROUP_DIM)
        mixed = jnp.concatenate([mixed_cols[g][:, cs] for g in range(SGU_GROUPS)], axis=1)
        rows.append(u[c * CHUNK:(c + 1) * CHUNK] * (mixed + bfull))
    sgu_out = jnp.concatenate(rows, axis=0).astype(BF16)
    y_sgu = _dot(sgu_out, wos_ref[...])
    gate = jax.nn.sigmoid(_dot_t(a, wg1_ref[...]) + bg1_ref[...])
    m_ref[...] = gate * y_sgu


def _sgu_branch(a, w_uv, sgu_g, w_s, b_full, w_o_sgu, w_g1, b_g1, tm):
    T, D = a.shape
    row = lambda i: (i, 0)
    return pl.pallas_call(
        _sgu_kernel,
        grid=(T // tm,),
        in_specs=[
            pl.BlockSpec((tm, D), row),
            _resident(w_uv.shape),
            _resident(sgu_g.shape),
            _resident(w_s.shape),
            _resident(b_full.shape),
            _resident(w_o_sgu.shape),
            _resident(w_g1.shape),
            _resident(b_g1.shape),
        ],
        out_specs=pl.BlockSpec((tm, D), row),
        out_shape=jax.ShapeDtypeStruct((T, D), F32),
        compiler_params=_params(1),
        name="sgu_branch",
    )(a, w_uv, sgu_g, w_s, b_full, w_o_sgu, w_g1, b_g1)


def _attn_kernel(q_ref, kn_ref, kpe_ref, v_ref, *rest, tq, n_cast):
    cast_in, (o_ref,), cast_out, (kf_ref, vf_ref) = (
        rest[:n_cast], rest[n_cast:n_cast + 1], rest[n_cast + 1:2 * n_cast + 1], rest[2 * n_cast + 1:])
    for src, dst in zip(cast_in, cast_out):
        dst[...] = src[...].astype(dst.dtype)
    seq = q_ref.shape[2]
    row = lax.broadcasted_iota(jnp.int32, (tq, tq), 0)
    col = lax.broadcasted_iota(jnp.int32, (tq, tq), 1)
    causal = row >= col
    neg = jnp.finfo(F32).min
    kf_ref[:, 0:LANES] = kn_ref[0, 0]
    kf_ref[:, LANES:2 * LANES] = kpe_ref[0]
    vf_ref[:, 0:LANES] = v_ref[0, 0]
    vf_ref[:, LANES:2 * LANES] = jnp.ones((seq, LANES), vf_ref.dtype)

    for qi in reversed(range(seq // tq)):
        q0 = qi * tq
        k_len = q0 + tq
        q = q_ref[0, 0, q0:k_len, :]
        s = lax.dot_general(q, kf_ref[0:k_len, :], (((1,), (1,)), ((), ())),
                            preferred_element_type=F32)
        s_diag = jnp.where(causal, s[:, q0:k_len], neg)
        s = jnp.concatenate([s[:, 0:q0], s_diag], axis=1) if qi else s_diag
        m = jnp.max(s, axis=-1, keepdims=True)
        p = jnp.exp2(s - m).astype(BF16)
        acc = _dot(p, vf_ref[0:k_len, :])
        o_ref[0, q0:k_len, :] = (acc[:, 0:LANES] / acc[:, LANES:2 * LANES]).astype(o_ref.dtype)


def _cast_block_specs(n_rows, n_cols, row0, n_steps, n_heads):
    share = 1 if (n_rows // n_steps) % BF16_SUBLANES == 0 else 2
    blk = n_rows * share // n_steps
    assert blk * n_steps == n_rows * share and blk % BF16_SUBLANES == 0 and row0 % blk == 0
    first = row0 // blk
    step = lambda b, h: (b * n_heads + h) // share
    return (pl.BlockSpec((blk, n_cols), lambda b, h: (first + step(b, h), 0)),
            pl.BlockSpec((blk, n_cols), lambda b, h: (step(b, h), 0)))


def _attention(q, k_nope, k_pe, v, casts, tq):
    B, H, S, _ = q.shape
    specs = [_cast_block_specs(n, w.shape[1], r0, B * H, H) for w, r0, n in casts]
    outs = pl.pallas_call(
        functools.partial(_attn_kernel, tq=tq, n_cast=len(casts)),
        grid=(B, H),
        in_specs=[
            pl.BlockSpec((1, 1, S, 2 * LANES), lambda b, h: (b, h, 0, 0)),
            pl.BlockSpec((1, 1, S, LANES), lambda b, h: (b, h, 0, 0)),
            pl.BlockSpec((1, S, LANES), lambda b, h: (b, 0, 0)),
            pl.BlockSpec((1, 1, S, LANES), lambda b, h: (b, h, 0, 0)),
        ] + [s_in for s_in, _ in specs],
        out_specs=[pl.BlockSpec((1, S, LANES), lambda b, h: (b, 0, h))]
        + [s_out for _, s_out in specs],
        out_shape=[jax.ShapeDtypeStruct((B, S, H * V_HEAD_DIM), BF16)]
        + [jax.ShapeDtypeStruct((n, w.shape[1]), BF16) for w, _, n in casts],
        scratch_shapes=[pltpu.VMEM((S, 2 * LANES), BF16), pltpu.VMEM((S, 2 * LANES), BF16)],
        compiler_params=_params(2),
        name="mla_attention",
    )(q, k_nope, k_pe, v, *[w for w, _, _ in casts])
    return outs[0], outs[1:]


def _merge_kernel(attn_ref, a_ref, m_ref, x_ref, woa_ref, wg0_ref, bg0_ref, wout_ref, fg_ref,
                  h_ref, f_ref):
    y_attn = _dot(attn_ref[...], woa_ref[...])
    gate = jax.nn.sigmoid(_dot_t(a_ref[...], wg0_ref[...]) + bg0_ref[...])
    merged = (gate * y_attn + m_ref[...]).astype(BF16)
    h = x_ref[...] + _dot(merged, wout_ref[...])
    h_ref[...] = h
    f_ref[...] = _rms(h, fg_ref[...]).astype(BF16)


def _merge(attn, a, m_sgu, x, w_o_attn, w_g0, b_g0, w_out, ffn_g, tm):
    T, D = x.shape
    row = lambda i: (i, 0)
    tile = pl.BlockSpec((tm, D), row)
    return pl.pallas_call(
        _merge_kernel,
        grid=(T // tm,),
        in_specs=[tile, tile, tile, tile,
                  _resident(w_o_attn.shape), _resident(w_g0.shape), _resident(b_g0.shape),
                  _resident(w_out.shape), _resident(ffn_g.shape)],
        out_specs=[tile, tile],
        out_shape=[jax.ShapeDtypeStruct((T, D), F32), jax.ShapeDtypeStruct((T, D), BF16)],
        compiler_params=_params(1),
        name="merge_outproj",
    )(attn, a, m_sgu, x, w_o_attn, w_g0, b_g0, w_out, ffn_g)


def _ffn_kernel(f_ref, h_ref, wg_ref, wu_ref, wd_ref, ng_ref, o_ref):
    j = pl.program_id(1)

    @pl.when(j == 0)
    def _():
        o_ref[...] = h_ref[...]

    f = f_ref[...]
    gate = _dot(f, wg_ref[...])
    up = _dot(f, wu_ref[...])
    act = (jax.nn.silu(gate) * up).astype(BF16)
    o_ref[...] += _dot(act, wd_ref[...])

    @pl.when(j == pl.num_programs(1) - 1)
    def _():
        o_ref[...] = _rms(o_ref[...], ng_ref[...])


def _ffn(f, h, w_gate, w_up, w_down, final_g, tm, tf):
    T, D = h.shape
    d_ff = w_gate.shape[1]
    row = lambda i, j: (i, 0)
    return pl.pallas_call(
        _ffn_kernel,
        grid=(T // tm, d_ff // tf),
        in_specs=[
            pl.BlockSpec((tm, D), row),
            pl.BlockSpec((tm, D), row),
            pl.BlockSpec((D, tf), lambda i, j: (0, j)),
            pl.BlockSpec((D, tf), lambda i, j: (0, j)),
            pl.BlockSpec((tf, D), lambda i, j: (j, 0)),
            _resident((1, D)),
        ],
        out_specs=pl.BlockSpec((tm, D), row),
        out_shape=jax.ShapeDtypeStruct((T, D), F32),
        compiler_params=_params(2),
        name="swiglu_ffn",
    )(f, h, w_gate, w_up, w_down, final_g)


def kernel(x, positions, norm_mix_g, w_in, b_gate, q_norm_g, w_uq, kv_norm_g, w_ukv, w_o_attn,
           sgu_norm_g, w_sgu, b_sgu, w_o_sgu, w_out, norm_ffn_g, w_gate_ffn, w_up_ffn,
           w_down_ffn, norm_final_g):
    B, S, D = x.shape
    T = B * S
    depth = w_in.shape[0]
    assert depth == 1, "the final norm is fused into the FFN epilogue of a single layer"
    assert w_in.shape[1:] == (D, D_IN)

    cs = _rope_tables(positions).reshape(B, S, LANES)
    row_vec = lambda v: v.reshape(1, -1).astype(F32)

    h = x
    out = None
    for l in range(depth):
        w_in_t = jnp.swapaxes(w_in[l], 0, 1)
        w_lat_t, w_uq_p, w_ukv_p = _prep_inproj_weights(w_in_t, w_uq[l], w_ukv[l])
        b_full = jnp.repeat(b_sgu[l].T, SGU_GROUP_DIM, axis=1).astype(F32)

        a, q, k_nope, v, k_pe = _inproj(
            h, row_vec(norm_mix_g[l]), cs, w_lat_t, row_vec(q_norm_g[l]), row_vec(kv_norm_g[l]),
            w_uq_p, w_ukv_p, tm=256)
        a2 = a.reshape(T, D)
        whole = lambda w: (w, 0, w.shape[0])
        attn, (w_uv_t, w_g0_t, w_g1_t, w_os, w_oa, w_o, w_gf, w_uf, w_df) = _attention(
            q, k_nope, k_pe, v,
            [(w_in_t, UV_OFF, 2 * SGU_WIDTH), (w_in_t, GATE_OFF, D), (w_in_t, GATE_OFF + D, D),
             whole(w_o_sgu[l]), whole(w_o_attn[l]), whole(w_out[l]),
             whole(w_gate_ffn[l]), whole(w_up_ffn[l]), whole(w_down_ffn[l])], tq=256)
        m_sgu = _sgu_branch(
            a2, w_uv_t, row_vec(sgu_norm_g[l]), w_sgu[l], b_full, w_os, w_g1_t,
            row_vec(b_gate[l, D:]), tm=256)
        h_mid, f = _merge(
            attn.reshape(T, D), a2, m_sgu, h.reshape(T, D), w_oa, w_g0_t,
            row_vec(b_gate[l, :D]), w_o, row_vec(norm_ffn_g[l]), tm=256)
        out = _ffn(f, h_mid, w_gf, w_uf, w_df, row_vec(norm_final_g), tm=1024, tf=256)
        h = out.reshape(B, S, D)
    return h
```

```python
import functools

import jax
import jax.numpy as jnp
from jax import lax
from jax.experimental import pallas as pl
from jax.experimental.pallas import tpu as pltpu

D_MODEL = 2048
N_HEADS = 16
QK_NOPE_DIM = 128
QK_ROPE_DIM = 64
V_HEAD_DIM = 128
Q_LORA_RANK = 512
KV_LORA_RANK = 512
ROPE_THETA = 10000.0
SGU_GROUPS = 8
SGU_GROUP_DIM = 128
SGU_WIDTH = SGU_GROUPS * SGU_GROUP_DIM
CHUNK = 128
N_BRANCH = 2
RMS_EPS = 1e-6
KPE_OFF = Q_LORA_RANK + KV_LORA_RANK
UV_OFF = KPE_OFF + QK_ROPE_DIM
GATE_OFF = UV_OFF + 2 * SGU_WIDTH
D_IN = GATE_OFF + N_BRANCH * D_MODEL
LANES = 128
HALF_ROPE = QK_ROPE_DIM // 2
LOG2_E = 1.4426950408889634
QK_LOG2_SCALE = (QK_NOPE_DIM + QK_ROPE_DIM) ** -0.5 * LOG2_E
BF16_SUBLANES = 16

VMEM_LIMIT_BYTES = 56 * 1024 * 1024

F32 = jnp.float32
BF16 = jnp.bfloat16


def _rms(x, g):
    return x * lax.rsqrt(jnp.mean(x * x, axis=-1, keepdims=True) + RMS_EPS) * g


def _dot(a, b):
    return jnp.dot(a, b, preferred_element_type=F32)


def _dot_t(a, b_t):
    return lax.dot_general(a, b_t, (((1,), (1,)), ((), ())), preferred_element_type=F32)


def _resident(shape):
    return pl.BlockSpec(shape, lambda *_: (0,) * len(shape), pipeline_mode=pl.Buffered(1))


def _params(n_axes):
    return pltpu.CompilerParams(
        dimension_semantics=("arbitrary",) * n_axes, vmem_limit_bytes=VMEM_LIMIT_BYTES)


def _rope_table_kernel(pos_ref, freq_ref, cos_ref, sin_ref):
    ang = pos_ref[...] * freq_ref[...]
    cos_ref[...] = jnp.cos(ang)
    sin_ref[...] = jnp.sin(ang)


def _rope_tables(positions):
    n_tok = positions.size
    per_row = LANES // HALF_ROPE
    inv_freq = ROPE_THETA ** (-jnp.arange(0, QK_ROPE_DIM, 2, dtype=F32) / QK_ROPE_DIM)
    pos_rep = jnp.repeat(positions.astype(F32).reshape(n_tok // per_row, per_row), HALF_ROPE, axis=1)
    freq = jnp.tile(inv_freq, per_row).reshape(1, LANES)
    shape = jax.ShapeDtypeStruct((n_tok // per_row, LANES), F32)
    cos, sin = pl.pallas_call(
        _rope_table_kernel, out_shape=(shape, shape), name="rope_tables")(pos_rep, freq)
    cos = cos.reshape(n_tok, HALF_ROPE)
    sin = sin.reshape(n_tok, HALF_ROPE)
    return jnp.concatenate([cos, cos, -sin, sin], axis=-1)


def _rope_dup(x, cs):
    y = x * cs
    return y + pltpu.roll(y, LANES // 2, 1)


def _dup_rope_lanes(src):
    lane = lax.broadcasted_iota(jnp.int32, src.shape, 1)
    return jnp.where(lane < 2 * HALF_ROPE, src,
                     jnp.where(lane < 3 * HALF_ROPE, pltpu.roll(src, HALF_ROPE, 1),
                               pltpu.roll(src, 3 * HALF_ROPE, 1)))


def _prep_kernel(wlat_ref, wuq_ref, wukv_ref, olat_ref, ouq_ref, oukv_ref):
    olat_ref[0:KPE_OFF] = wlat_ref[0:KPE_OFF].astype(BF16)
    x1 = wlat_ref[KPE_OFF:KPE_OFF + HALF_ROPE].astype(BF16)
    x2 = wlat_ref[KPE_OFF + HALF_ROPE:UV_OFF].astype(BF16)
    for k, part in enumerate((x1, x2, x2, x1)):
        olat_ref[KPE_OFF + k * HALF_ROPE:KPE_OFF + (k + 1) * HALF_ROPE] = part
    oukv_ref[...] = wukv_ref[...].astype(BF16)
    half = LANES // 2
    nope_cols = N_HEADS * QK_NOPE_DIM
    lane = lax.broadcasted_iota(jnp.int32, (wuq_ref.shape[0], LANES), 1)
    for pair in range(N_HEADS // 2):
        t0, t1, t2 = (wuq_ref[:, (3 * pair + k) * LANES:(3 * pair + k + 1) * LANES] for k in range(3))
        r1 = pltpu.roll(t1, half, 1)
        r2 = pltpu.roll(t2, half, 1)
        heads = ((2 * pair, t0, t1), (2 * pair + 1, jnp.where(lane < half, r1, r2), r2))
        for h, nope, rope_src in heads:
            ouq_ref[:, h * LANES:(h + 1) * LANES] = nope.astype(BF16)
            ouq_ref[:, nope_cols + h * LANES:nope_cols + (h + 1) * LANES] = (
                _dup_rope_lanes(rope_src).astype(BF16))


def _prep_inproj_weights(w_in_t, w_uq, w_ukv, n_chunks=4):
    d = w_in_t.shape[1]
    r_q, r_kv = w_uq.shape[0], w_ukv.shape[0]
    lat_rows = UV_OFF + LANES - QK_ROPE_DIM
    uq_cols = N_HEADS * (QK_NOPE_DIM + LANES)
    return pl.pallas_call(
        _prep_kernel,
        grid=(n_chunks,),
        in_specs=[
            pl.BlockSpec((UV_OFF, d // n_chunks), lambda c: (0, c)),
            pl.BlockSpec((r_q // n_chunks, w_uq.shape[1]), lambda c: (c, 0)),
            pl.BlockSpec((r_kv // n_chunks, w_ukv.shape[1]), lambda c: (c, 0)),
        ],
        out_specs=[
            pl.BlockSpec((lat_rows, d // n_chunks), lambda c: (0, c)),
            pl.BlockSpec((r_q // n_chunks, uq_cols), lambda c: (c, 0)),
            pl.BlockSpec((r_kv // n_chunks, w_ukv.shape[1]), lambda c: (c, 0)),
        ],
        out_shape=[
            jax.ShapeDtypeStruct((lat_rows, d), BF16),
            jax.ShapeDtypeStruct((r_q, uq_cols), BF16),
            jax.ShapeDtypeStruct(w_ukv.shape, BF16),
        ],
        compiler_params=_params(1),
        name="prep_inproj_weights",
    )(w_in_t, w_uq, w_ukv)


def _inproj_kernel(x_ref, g_ref, cs_ref, wlat_ref, qg_ref, kvg_ref, wuq_ref, wukv_ref,
                   a_ref, q_ref, kn_ref, v_ref, kpe_ref, *, sub):
    def latents(r0):
        a = _rms(x_ref[0, r0:r0 + sub, :], g_ref[...]).astype(BF16)
        a_ref[0, r0:r0 + sub, :] = a
        return _dot_t(a, wlat_ref[...])

    starts = list(range(0, x_ref.shape[1], sub))
    for r0, z in zip(starts, [latents(r0) for r0 in starts]):
        rows = slice(r0, r0 + sub)
        qn = (_rms(z[:, :Q_LORA_RANK], qg_ref[...]) * QK_LOG2_SCALE).astype(BF16)
        kvn = _rms(z[:, Q_LORA_RANK:Q_LORA_RANK + KV_LORA_RANK], kvg_ref[...]).astype(BF16)
        cs = cs_ref[0, rows, :]
        kpe = _rope_dup(z[:, Q_LORA_RANK + KV_LORA_RANK:], cs)
        lane = lax.broadcasted_iota(jnp.int32, kpe.shape, 1)
        kpe_ref[0, rows, :] = jnp.where(lane < QK_ROPE_DIM, kpe, 0.0).astype(BF16)

        heads_per_dot = 4
        width = heads_per_dot * LANES
        nope_cols = N_HEADS * QK_NOPE_DIM
        for hg in range(N_HEADS // heads_per_dot):
            c0 = hg * width
            q_nope = _dot(qn, wuq_ref[:, c0:c0 + width])
            q_pe = _dot(qn, wuq_ref[:, nope_cols + c0:nope_cols + c0 + width])
            kv0 = 2 * c0
            kv_a = _dot(kvn, wukv_ref[:, kv0:kv0 + width])
            kv_b = _dot(kvn, wukv_ref[:, kv0 + width:kv0 + 2 * width])
            for hh in range(heads_per_dot):
                h = hg * heads_per_dot + hh
                sl = slice(hh * LANES, (hh + 1) * LANES)
                q_ref[0, h, rows, 0:LANES] = q_nope[:, sl].astype(BF16)
                q_ref[0, h, rows, LANES:2 * LANES] = _rope_dup(q_pe[:, sl], cs).astype(BF16)
                kv = kv_a if hh < heads_per_dot // 2 else kv_b
                k0 = (hh % (heads_per_dot // 2)) * 2 * LANES
                kn_ref[0, h, rows, :] = kv[:, k0:k0 + LANES].astype(BF16)
                v_ref[0, h, rows, :] = kv[:, k0 + LANES:k0 + 2 * LANES].astype(BF16)


def _inproj(x, norm_g, cs, w_lat_t, q_g, kv_g, w_uq, w_ukv, tm, sub):
    B, S, D = x.shape
    row = lambda b, i: (b, i, 0)
    head = lambda b, i: (b, 0, i, 0)
    return pl.pallas_call(
        functools.partial(_inproj_kernel, sub=sub),
        grid=(B, S // tm),
        in_specs=[
            pl.BlockSpec((1, tm, D), row),
            _resident((1, D)),
            pl.BlockSpec((1, tm, LANES), row),
            _resident(w_lat_t.shape),
            _resident((1, Q_LORA_RANK)),
            _resident((1, KV_LORA_RANK)),
            _resident(w_uq.shape),
            _resident(w_ukv.shape),
        ],
        out_specs=[
            pl.BlockSpec((1, tm, D), row),
            pl.BlockSpec((1, N_HEADS, tm, 2 * LANES), head),
            pl.BlockSpec((1, N_HEADS, tm, LANES), head),
            pl.BlockSpec((1, N_HEADS, tm, LANES), head),
            pl.BlockSpec((1, tm, LANES), row),
        ],
        out_shape=[
            jax.ShapeDtypeStruct((B, S, D), BF16),
            jax.ShapeDtypeStruct((B, N_HEADS, S, 2 * LANES), BF16),
            jax.ShapeDtypeStruct((B, N_HEADS, S, LANES), BF16),
            jax.ShapeDtypeStruct((B, N_HEADS, S, LANES), BF16),
            jax.ShapeDtypeStruct((B, S, LANES), BF16),
        ],
        compiler_params=_params(2),
        name="inproj",
    )(x, norm_g, cs, w_lat_t, q_g, kv_g, w_uq, w_ukv)


def _sgu_kernel(a_ref, wuv_ref, sg_ref, ws_ref, bfull_ref, wos_ref, wg1_ref, bg1_ref, m_ref, *, sub):
    n_chunks = sub // CHUNK
    t_idx = lax.broadcasted_iota(jnp.int32, (CHUNK, CHUNK), 0)
    s_idx = lax.broadcasted_iota(jnp.int32, (CHUNK, CHUNK), 1)
    causal = t_idx >= s_idx
    ws = [jnp.where(causal, ws_ref[g], 0.0).astype(BF16) for g in range(SGU_GROUPS)]
    bfull = bfull_ref[...]
    def gating_unit(uv_raw):
        uv = jax.nn.gelu(uv_raw)
        u = uv[:, :SGU_WIDTH]
        vn = _rms(uv[:, SGU_WIDTH:], sg_ref[...]).astype(BF16)
        mixed_cols = []
        for g in range(SGU_GROUPS):
            gs = slice(g * SGU_GROUP_DIM, (g + 1) * SGU_GROUP_DIM)
            rhs = jnp.concatenate(
                [vn[c * CHUNK:(c + 1) * CHUNK, gs] for c in range(n_chunks)], axis=1)
            mixed_cols.append(_dot(ws[g], rhs))
        rows = []
        for c in range(n_chunks):
            cs = slice(c * SGU_GROUP_DIM, (c + 1) * SGU_GROUP_DIM)
            mixed = jnp.concatenate([mixed_cols[g][:, cs] for g in range(SGU_GROUPS)], axis=1)
            rows.append(u[c * CHUNK:(c + 1) * CHUNK] * (mixed + bfull))
        return jnp.concatenate(rows, axis=0).astype(BF16)

    a = a_ref[...]
    uv_raw = _dot_t(a, wuv_ref[...])
    gate_raw = _dot_t(a, wg1_ref[...])
    for r0 in range(0, a_ref.shape[0], sub):
        rows = slice(r0, r0 + sub)
        y_sgu = _dot(gating_unit(uv_raw[rows]), wos_ref[...])
        m_ref[rows, :] = jax.nn.sigmoid(gate_raw[rows] + bg1_ref[...]) * y_sgu


def _sgu_branch(a, w_uv, sgu_g, w_s, b_full, w_o_sgu, w_g1, b_g1, tm, sub):
    T, D = a.shape
    row = lambda i: (i, 0)
    return pl.pallas_call(
        functools.partial(_sgu_kernel, sub=sub),
        grid=(T // tm,),
        in_specs=[
            pl.BlockSpec((tm, D), row),
            _resident(w_uv.shape),
            _resident(sgu_g.shape),
            _resident(w_s.shape),
            _resident(b_full.shape),
            _resident(w_o_sgu.shape),
            _resident(w_g1.shape),
            _resident(b_g1.shape),
        ],
        out_specs=pl.BlockSpec((tm, D), row),
        out_shape=jax.ShapeDtypeStruct((T, D), F32),
        compiler_params=_params(1),
        name="sgu_branch",
    )(a, w_uv, sgu_g, w_s, b_full, w_o_sgu, w_g1, b_g1)


def _attn_kernel(q_ref, kn_ref, kpe_ref, v_ref, *rest, tq, n_cast):
    cast_in, (o_ref,), cast_out, (kf_ref, vf_ref) = (
        rest[:n_cast], rest[n_cast:n_cast + 1], rest[n_cast + 1:2 * n_cast + 1], rest[2 * n_cast + 1:])
    for src, dst in zip(cast_in, cast_out):
        dst[...] = src[...].astype(dst.dtype)
    seq = q_ref.shape[2]
    row = lax.broadcasted_iota(jnp.int32, (tq, tq), 0)
    col = lax.broadcasted_iota(jnp.int32, (tq, tq), 1)
    causal = row >= col
    neg = jnp.finfo(F32).min
    heads = q_ref.shape[1]
    for hh in range(heads):
        kf_ref[hh, :, 0:LANES] = kn_ref[0, hh]
        kf_ref[hh, :, LANES:2 * LANES] = kpe_ref[0]
        vf_ref[hh, :, 0:LANES] = v_ref[0, hh]
        vf_ref[hh, :, LANES:2 * LANES] = jnp.ones((seq, LANES), vf_ref.dtype)

    for qi in reversed(range(seq // tq)):
        q0 = qi * tq
        k_len = q0 + tq
        for hh in range(heads):
            q = q_ref[0, hh, q0:k_len, :]
            s = lax.dot_general(q, kf_ref[hh, 0:k_len, :], (((1,), (1,)), ((), ())),
                                preferred_element_type=F32)
            s_diag = jnp.where(causal, s[:, q0:k_len], neg)
            s = jnp.concatenate([s[:, 0:q0], s_diag], axis=1) if qi else s_diag
            m = jnp.max(s, axis=-1, keepdims=True)
            p = jnp.exp2(s - m).astype(BF16)
            acc = _dot(p, vf_ref[hh, 0:k_len, :])
            o_ref[0, q0:k_len, hh * LANES:(hh + 1) * LANES] = (
                acc[:, 0:LANES] / acc[:, LANES:2 * LANES]).astype(o_ref.dtype)


def _cast_block_specs(n_rows, n_cols, row0, n_steps, n_heads):
    share = 1 if (n_rows // n_steps) % BF16_SUBLANES == 0 else 2
    blk = n_rows * share // n_steps
    assert blk * n_steps == n_rows * share and blk % BF16_SUBLANES == 0 and row0 % blk == 0
    first = row0 // blk
    step = lambda b, h: (b * n_heads + h) // share
    return (pl.BlockSpec((blk, n_cols), lambda b, h: (first + step(b, h), 0)),
            pl.BlockSpec((blk, n_cols), lambda b, h: (step(b, h), 0)))


def _attention(q, k_nope, k_pe, v, casts, tq, hb):
    B, H, S, _ = q.shape
    groups = H // hb
    specs = [_cast_block_specs(n, w.shape[1], r0, B * groups, groups) for w, r0, n in casts]
    outs = pl.pallas_call(
        functools.partial(_attn_kernel, tq=tq, n_cast=len(casts)),
        grid=(B, groups),
        in_specs=[
            pl.BlockSpec((1, hb, S, 2 * LANES), lambda b, g: (b, g, 0, 0)),
            pl.BlockSpec((1, hb, S, LANES), lambda b, g: (b, g, 0, 0)),
            pl.BlockSpec((1, S, LANES), lambda b, g: (b, 0, 0)),
            pl.BlockSpec((1, hb, S, LANES), lambda b, g: (b, g, 0, 0)),
        ] + [s_in for s_in, _ in specs],
        out_specs=[pl.BlockSpec((1, S, hb * LANES), lambda b, g: (b, 0, g))]
        + [s_out for _, s_out in specs],
        out_shape=[jax.ShapeDtypeStruct((B, S, H * V_HEAD_DIM), BF16)]
        + [jax.ShapeDtypeStruct((n, w.shape[1]), BF16) for w, _, n in casts],
        scratch_shapes=[pltpu.VMEM((hb, S, 2 * LANES), BF16), pltpu.VMEM((hb, S, 2 * LANES), BF16)],
        compiler_params=_params(2),
        name="mla_attention",
    )(q, k_nope, k_pe, v, *[w for w, _, _ in casts])
    return outs[0], outs[1:]


def _merge_kernel(attn_ref, a_ref, m_ref, x_ref, woa_ref, wg0_ref, bg0_ref, wout_ref, fg_ref,
                  h_ref, f_ref):
    y_attn = _dot(attn_ref[...], woa_ref[...])
    gate = jax.nn.sigmoid(_dot_t(a_ref[...], wg0_ref[...]) + bg0_ref[...])
    merged = (gate * y_attn + m_ref[...]).astype(BF16)
    h = x_ref[...] + _dot(merged, wout_ref[...])
    h_ref[...] = h
    f_ref[...] = _rms(h, fg_ref[...]).astype(BF16)


def _merge(attn, a, m_sgu, x, w_o_attn, w_g0, b_g0, w_out, ffn_g, tm):
    T, D = x.shape
    row = lambda i: (i, 0)
    tile = pl.BlockSpec((tm, D), row)
    return pl.pallas_call(
        _merge_kernel,
        grid=(T // tm,),
        in_specs=[tile, tile, tile, tile,
                  _resident(w_o_attn.shape), _resident(w_g0.shape), _resident(b_g0.shape),
                  _resident(w_out.shape), _resident(ffn_g.shape)],
        out_specs=[tile, tile],
        out_shape=[jax.ShapeDtypeStruct((T, D), F32), jax.ShapeDtypeStruct((T, D), BF16)],
        compiler_params=_params(1),
        name="merge_outproj",
    )(attn, a, m_sgu, x, w_o_attn, w_g0, b_g0, w_out, ffn_g)


def _ffn_kernel(f_ref, h_ref, wg_ref, wu_ref, wd_ref, ng_ref, o_ref):
    j = pl.program_id(1)

    @pl.when(j == 0)
    def _():
        o_ref[...] = h_ref[...]

    f = f_ref[...]
    gate = _dot(f, wg_ref[...])
    up = _dot(f, wu_ref[...])
    act = (jax.nn.silu(gate) * up).astype(BF16)
    o_ref[...] += _dot(act, wd_ref[...])

    @pl.when(j == pl.num_programs(1) - 1)
    def _():
        o_ref[...] = _rms(o_ref[...], ng_ref[...])


def _ffn(f, h, w_gate, w_up, w_down, final_g, tm, tf):
    T, D = h.shape
    d_ff = w_gate.shape[1]
    row = lambda i, j: (i, 0)
    return pl.pallas_call(
        _ffn_kernel,
        grid=(T // tm, d_ff // tf),
        in_specs=[
            pl.BlockSpec((tm, D), row),
            pl.BlockSpec((tm, D), row),
            pl.BlockSpec((D, tf), lambda i, j: (0, j)),
            pl.BlockSpec((D, tf), lambda i, j: (0, j)),
            pl.BlockSpec((tf, D), lambda i, j: (j, 0)),
            _resident((1, D)),
        ],
        out_specs=pl.BlockSpec((tm, D), row),
        out_shape=jax.ShapeDtypeStruct((T, D), F32),
        compiler_params=_params(2),
        name="swiglu_ffn",
    )(f, h, w_gate, w_up, w_down, final_g)


def kernel(x, positions, norm_mix_g, w_in, b_gate, q_norm_g, w_uq, kv_norm_g, w_ukv, w_o_attn,
           sgu_norm_g, w_sgu, b_sgu, w_o_sgu, w_out, norm_ffn_g, w_gate_ffn, w_up_ffn,
           w_down_ffn, norm_final_g):
    B, S, D = x.shape
    T = B * S
    depth = w_in.shape[0]
    assert depth == 1, "the final norm is fused into the FFN epilogue of a single layer"
    assert w_in.shape[1:] == (D, D_IN)

    cs = _rope_tables(positions).reshape(B, S, LANES)
    row_vec = lambda v: v.reshape(1, -1).astype(F32)

    h = x
    out = None
    for l in range(depth):
        w_in_t = jnp.swapaxes(w_in[l], 0, 1)
        w_lat_t, w_uq_p, w_ukv_p = _prep_inproj_weights(w_in_t, w_uq[l], w_ukv[l])
        b_full = jnp.repeat(b_sgu[l].T, SGU_GROUP_DIM, axis=1).astype(F32)

        a, q, k_nope, v, k_pe = _inproj(
            h, row_vec(norm_mix_g[l]), cs, w_lat_t, row_vec(q_norm_g[l]), row_vec(kv_norm_g[l]),
            w_uq_p, w_ukv_p, tm=512, sub=256)
        a2 = a.reshape(T, D)
        whole = lambda w: (w, 0, w.shape[0])
        attn, (w_uv_t, w_g0_t, w_g1_t, w_os, w_oa, w_o, w_gf, w_uf, w_df) = _attention(
            q, k_nope, k_pe, v,
            [(w_in_t, UV_OFF, 2 * SGU_WIDTH), (w_in_t, GATE_OFF, D), (w_in_t, GATE_OFF + D, D),
             whole(w_o_sgu[l]), whole(w_o_attn[l]), whole(w_out[l]),
             whole(w_gate_ffn[l]), whole(w_up_ffn[l]), whole(w_down_ffn[l])], tq=256, hb=2)
        m_sgu = _sgu_branch(
            a2, w_uv_t, row_vec(sgu_norm_g[l]), w_sgu[l], b_full, w_os, w_g1_t,
            row_vec(b_gate[l, D:]), tm=512, sub=256)
        h_mid, f = _merge(
            attn.reshape(T, D), a2, m_sgu, h.reshape(T, D), w_oa, w_g0_t,
            row_vec(b_gate[l, :D]), w_o, row_vec(norm_ffn_g[l]), tm=256)
        out = _ffn(f, h_mid, w_gf, w_uf, w_df, row_vec(norm_final_g), tm=1024, tf=256)
        h = out.reshape(B, S, D)
    return h
```

```python
import functools

import jax
import jax.numpy as jnp
from jax import lax
from jax.experimental import pallas as pl
from jax.experimental.pallas import tpu as pltpu

D_MODEL = 2048
N_HEADS = 16
QK_NOPE_DIM = 128
QK_ROPE_DIM = 64
V_HEAD_DIM = 128
Q_LORA_RANK = 512
KV_LORA_RANK = 512
ROPE_THETA = 10000.0
SGU_GROUPS = 8
SGU_GROUP_DIM = 128
SGU_WIDTH = SGU_GROUPS * SGU_GROUP_DIM
CHUNK = 128
N_BRANCH = 2
RMS_EPS = 1e-6
KPE_OFF = Q_LORA_RANK + KV_LORA_RANK
UV_OFF = KPE_OFF + QK_ROPE_DIM
GATE_OFF = UV_OFF + 2 * SGU_WIDTH
D_IN = GATE_OFF + N_BRANCH * D_MODEL
LANES = 128
HALF_ROPE = QK_ROPE_DIM // 2
LOG2_E = 1.4426950408889634
QK_LOG2_SCALE = (QK_NOPE_DIM + QK_ROPE_DIM) ** -0.5 * LOG2_E
BF16_SUBLANES = 16

VMEM_LIMIT_BYTES = 56 * 1024 * 1024

F32 = jnp.float32
BF16 = jnp.bfloat16


def _rms(x, g):
    return x * lax.rsqrt(jnp.mean(x * x, axis=-1, keepdims=True) + RMS_EPS) * g


def _dot(a, b):
    return jnp.dot(a, b, preferred_element_type=F32)


def _dot_t(a, b_t):
    return lax.dot_general(a, b_t, (((1,), (1,)), ((), ())), preferred_element_type=F32)


def _resident(shape):
    return pl.BlockSpec(shape, lambda *_: (0,) * len(shape), pipeline_mode=pl.Buffered(1))


def _params(n_axes):
    return pltpu.CompilerParams(
        dimension_semantics=("arbitrary",) * n_axes, vmem_limit_bytes=VMEM_LIMIT_BYTES)


def _rope_table_kernel(pos_ref, freq_ref, cos_ref, sin_ref):
    ang = pos_ref[...] * freq_ref[...]
    cos_ref[...] = jnp.cos(ang)
    sin_ref[...] = jnp.sin(ang)


def _rope_tables(positions):
    n_tok = positions.size
    per_row = LANES // HALF_ROPE
    inv_freq = ROPE_THETA ** (-jnp.arange(0, QK_ROPE_DIM, 2, dtype=F32) / QK_ROPE_DIM)
    pos_rep = jnp.repeat(positions.astype(F32).reshape(n_tok // per_row, per_row), HALF_ROPE, axis=1)
    freq = jnp.tile(inv_freq, per_row).reshape(1, LANES)
    shape = jax.ShapeDtypeStruct((n_tok // per_row, LANES), F32)
    cos, sin = pl.pallas_call(
        _rope_table_kernel, out_shape=(shape, shape), name="rope_tables")(pos_rep, freq)
    cos = cos.reshape(n_tok, HALF_ROPE)
    sin = sin.reshape(n_tok, HALF_ROPE)
    return jnp.concatenate([cos, cos, -sin, sin], axis=-1)


def _rope_dup(x, cs):
    y = x * cs
    return y + pltpu.roll(y, LANES // 2, 1)


def _dup_rope_lanes(src):
    lane = lax.broadcasted_iota(jnp.int32, src.shape, 1)
    return jnp.where(lane < 2 * HALF_ROPE, src,
                     jnp.where(lane < 3 * HALF_ROPE, pltpu.roll(src, HALF_ROPE, 1),
                               pltpu.roll(src, 3 * HALF_ROPE, 1)))


def _prep_kernel(wlat_ref, wuq_ref, wukv_ref, olat_ref, ouq_ref, oukv_ref):
    olat_ref[0:KPE_OFF] = wlat_ref[0:KPE_OFF].astype(BF16)
    x1 = wlat_ref[KPE_OFF:KPE_OFF + HALF_ROPE].astype(BF16)
    x2 = wlat_ref[KPE_OFF + HALF_ROPE:UV_OFF].astype(BF16)
    for k, part in enumerate((x1, x2, x2, x1)):
        olat_ref[KPE_OFF + k * HALF_ROPE:KPE_OFF + (k + 1) * HALF_ROPE] = part
    oukv_ref[...] = wukv_ref[...].astype(BF16)
    half = LANES // 2
    nope_cols = N_HEADS * QK_NOPE_DIM
    lane = lax.broadcasted_iota(jnp.int32, (wuq_ref.shape[0], LANES), 1)
    for pair in range(N_HEADS // 2):
        t0, t1, t2 = (wuq_ref[:, (3 * pair + k) * LANES:(3 * pair + k + 1) * LANES] for k in range(3))
        r1 = pltpu.roll(t1, half, 1)
        r2 = pltpu.roll(t2, half, 1)
        heads = ((2 * pair, t0, t1), (2 * pair + 1, jnp.where(lane < half, r1, r2), r2))
        for h, nope, rope_src in heads:
            ouq_ref[:, h * LANES:(h + 1) * LANES] = nope.astype(BF16)
            ouq_ref[:, nope_cols + h * LANES:nope_cols + (h + 1) * LANES] = (
                _dup_rope_lanes(rope_src).astype(BF16))


def _prep_inproj_weights(w_in_t, w_uq, w_ukv, n_chunks=4):
    d = w_in_t.shape[1]
    r_q, r_kv = w_uq.shape[0], w_ukv.shape[0]
    lat_rows = UV_OFF + LANES - QK_ROPE_DIM
    uq_cols = N_HEADS * (QK_NOPE_DIM + LANES)
    return pl.pallas_call(
        _prep_kernel,
        grid=(n_chunks,),
        in_specs=[
            pl.BlockSpec((UV_OFF, d // n_chunks), lambda c: (0, c)),
            pl.BlockSpec((r_q // n_chunks, w_uq.shape[1]), lambda c: (c, 0)),
            pl.BlockSpec((r_kv // n_chunks, w_ukv.shape[1]), lambda c: (c, 0)),
        ],
        out_specs=[
            pl.BlockSpec((lat_rows, d // n_chunks), lambda c: (0, c)),
            pl.BlockSpec((r_q // n_chunks, uq_cols), lambda c: (c, 0)),
            pl.BlockSpec((r_kv // n_chunks, w_ukv.shape[1]), lambda c: (c, 0)),
        ],
        out_shape=[
            jax.ShapeDtypeStruct((lat_rows, d), BF16),
            jax.ShapeDtypeStruct((r_q, uq_cols), BF16),
            jax.ShapeDtypeStruct(w_ukv.shape, BF16),
        ],
        compiler_params=_params(1),
        name="prep_inproj_weights",
    )(w_in_t, w_uq, w_ukv)


def _inproj_kernel(x_ref, g_ref, cs_ref, wlat_ref, qg_ref, kvg_ref, wuq_ref, wukv_ref,
                   a_ref, q_ref, kn_ref, v_ref, kpe_ref, *, sub):
    def latents(r0):
        a = _rms(x_ref[0, r0:r0 + sub, :], g_ref[...]).astype(BF16)
        a_ref[0, r0:r0 + sub, :] = a
        return _dot_t(a, wlat_ref[...])

    starts = list(range(0, x_ref.shape[1], sub))
    for r0, z in zip(starts, [latents(r0) for r0 in starts]):
        rows = slice(r0, r0 + sub)
        qn = (_rms(z[:, :Q_LORA_RANK], qg_ref[...]) * QK_LOG2_SCALE).astype(BF16)
        kvn = _rms(z[:, Q_LORA_RANK:Q_LORA_RANK + KV_LORA_RANK], kvg_ref[...]).astype(BF16)
        cs = cs_ref[0, rows, :]
        kpe = _rope_dup(z[:, Q_LORA_RANK + KV_LORA_RANK:], cs)
        lane = lax.broadcasted_iota(jnp.int32, kpe.shape, 1)
        kpe_ref[0, rows, :] = jnp.where(lane < QK_ROPE_DIM, kpe, 0.0).astype(BF16)

        heads_per_dot = 4
        width = heads_per_dot * LANES
        nope_cols = N_HEADS * QK_NOPE_DIM
        for hg in range(N_HEADS // heads_per_dot):
            c0 = hg * width
            q_nope = _dot(qn, wuq_ref[:, c0:c0 + width])
            q_pe = _dot(qn, wuq_ref[:, nope_cols + c0:nope_cols + c0 + width])
            kv0 = 2 * c0
            kv_a = _dot(kvn, wukv_ref[:, kv0:kv0 + width])
            kv_b = _dot(kvn, wukv_ref[:, kv0 + width:kv0 + 2 * width])
            for hh in range(heads_per_dot):
                h = hg * heads_per_dot + hh
                sl = slice(hh * LANES, (hh + 1) * LANES)
                q_ref[0, h, rows, 0:LANES] = q_nope[:, sl].astype(BF16)
                q_ref[0, h, rows, LANES:2 * LANES] = _rope_dup(q_pe[:, sl], cs).astype(BF16)
                kv = kv_a if hh < heads_per_dot // 2 else kv_b
                k0 = (hh % (heads_per_dot // 2)) * 2 * LANES
                kn_ref[0, h, rows, :] = kv[:, k0:k0 + LANES].astype(BF16)
                v_ref[0, h, rows, :] = kv[:, k0 + LANES:k0 + 2 * LANES].astype(BF16)


def _inproj(x, norm_g, cs, w_lat_t, q_g, kv_g, w_uq, w_ukv, tm, sub):
    B, S, D = x.shape
    row = lambda b, i: (b, i, 0)
    head = lambda b, i: (b, 0, i, 0)
    return pl.pallas_call(
        functools.partial(_inproj_kernel, sub=sub),
        grid=(B, S // tm),
        in_specs=[
            pl.BlockSpec((1, tm, D), row),
            _resident((1, D)),
            pl.BlockSpec((1, tm, LANES), row),
            _resident(w_lat_t.shape),
            _resident((1, Q_LORA_RANK)),
            _resident((1, KV_LORA_RANK)),
            _resident(w_uq.shape),
            _resident(w_ukv.shape),
        ],
        out_specs=[
            pl.BlockSpec((1, tm, D), row),
            pl.BlockSpec((1, N_HEADS, tm, 2 * LANES), head),
            pl.BlockSpec((1, N_HEADS, tm, LANES), head),
            pl.BlockSpec((1, N_HEADS, tm, LANES), head),
            pl.BlockSpec((1, tm, LANES), row),
        ],
        out_shape=[
            jax.ShapeDtypeStruct((B, S, D), BF16),
            jax.ShapeDtypeStruct((B, N_HEADS, S, 2 * LANES), BF16),
            jax.ShapeDtypeStruct((B, N_HEADS, S, LANES), BF16),
            jax.ShapeDtypeStruct((B, N_HEADS, S, LANES), BF16),
            jax.ShapeDtypeStruct((B, S, LANES), BF16),
        ],
        compiler_params=_params(2),
        name="inproj",
    )(x, norm_g, cs, w_lat_t, q_g, kv_g, w_uq, w_ukv)


def _sgu_kernel(a_ref, wuv_ref, sg_ref, ws_ref, bfull_ref, wos_ref, wg1_ref, bg1_ref, m_ref, *, sub):
    n_chunks = sub // CHUNK
    t_idx = lax.broadcasted_iota(jnp.int32, (CHUNK, CHUNK), 0)
    s_idx = lax.broadcasted_iota(jnp.int32, (CHUNK, CHUNK), 1)
    causal = t_idx >= s_idx
    ws = [jnp.where(causal, ws_ref[g], 0.0).astype(BF16) for g in range(SGU_GROUPS)]
    bfull = bfull_ref[...]
    def gating_unit(uv_raw):
        uv = jax.nn.gelu(uv_raw)
        u = uv[:, :SGU_WIDTH]
        vn = _rms(uv[:, SGU_WIDTH:], sg_ref[...]).astype(BF16)
        mixed_cols = []
        for g in range(SGU_GROUPS):
            gs = slice(g * SGU_GROUP_DIM, (g + 1) * SGU_GROUP_DIM)
            rhs = jnp.concatenate(
                [vn[c * CHUNK:(c + 1) * CHUNK, gs] for c in range(n_chunks)], axis=1)
            mixed_cols.append(_dot(ws[g], rhs))
        rows = []
        for c in range(n_chunks):
            cs = slice(c * SGU_GROUP_DIM, (c + 1) * SGU_GROUP_DIM)
            mixed = jnp.concatenate([mixed_cols[g][:, cs] for g in range(SGU_GROUPS)], axis=1)
            rows.append(u[c * CHUNK:(c + 1) * CHUNK] * (mixed + bfull))
        return jnp.concatenate(rows, axis=0).astype(BF16)

    a = a_ref[...]
    uv_raw = _dot_t(a, wuv_ref[...])
    gate_raw = _dot_t(a, wg1_ref[...])
    for r0 in range(0, a_ref.shape[0], sub):
        rows = slice(r0, r0 + sub)
        y_sgu = _dot(gating_unit(uv_raw[rows]), wos_ref[...])
        m_ref[rows, :] = jax.nn.sigmoid(gate_raw[rows] + bg1_ref[...]) * y_sgu


def _sgu_branch(a, w_uv, sgu_g, w_s, b_full, w_o_sgu, w_g1, b_g1, tm, sub):
    T, D = a.shape
    row = lambda i: (i, 0)
    return pl.pallas_call(
        functools.partial(_sgu_kernel, sub=sub),
        grid=(T // tm,),
        in_specs=[
            pl.BlockSpec((tm, D), row),
            _resident(w_uv.shape),
            _resident(sgu_g.shape),
            _resident(w_s.shape),
            _resident(b_full.shape),
            _resident(w_o_sgu.shape),
            _resident(w_g1.shape),
            _resident(b_g1.shape),
        ],
        out_specs=pl.BlockSpec((tm, D), row),
        out_shape=jax.ShapeDtypeStruct((T, D), F32),
        compiler_params=_params(1),
        name="sgu_branch",
    )(a, w_uv, sgu_g, w_s, b_full, w_o_sgu, w_g1, b_g1)


def _attn_kernel(q_ref, kn_ref, kpe_ref, v_ref, *rest, tq, n_cast):
    cast_in, (o_ref,), cast_out, (kf_ref, vf_ref) = (
        rest[:n_cast], rest[n_cast:n_cast + 1], rest[n_cast + 1:2 * n_cast + 1], rest[2 * n_cast + 1:])
    for src, dst in zip(cast_in, cast_out):
        dst[...] = src[...].astype(dst.dtype)
    seq = q_ref.shape[2]
    row = lax.broadcasted_iota(jnp.int32, (tq, tq), 0)
    col = lax.broadcasted_iota(jnp.int32, (tq, tq), 1)
    causal = row >= col
    neg = jnp.finfo(F32).min
    heads = q_ref.shape[1]
    for hh in range(heads):
        kf_ref[hh, :, 0:LANES] = kn_ref[0, hh]
        kf_ref[hh, :, LANES:2 * LANES] = kpe_ref[0]
        vf_ref[hh, :, 0:LANES] = v_ref[0, hh]
        vf_ref[hh, :, LANES:2 * LANES] = jnp.ones((seq, LANES), vf_ref.dtype)

    nt = (((1,), (1,)), ((), ()))
    for pair in reversed(range(seq // (2 * tq))):
        q0 = pair * 2 * tq
        k1, k2 = q0 + tq, q0 + 2 * tq
        scores = []
        for hh in range(heads):
            q = q_ref[0, hh, q0:k2, :]
            scores.append((
                lax.dot_general(q, kf_ref[hh, 0:k1, :], nt, preferred_element_type=F32),
                lax.dot_general(q[tq:], kf_ref[hh, k1:k2, :], nt, preferred_element_type=F32)))
        for hh, (s_main, s_ext) in enumerate(scores):
            top = s_main[0:tq]
            top_diag = jnp.where(causal, top[:, q0:k1], neg)
            top = jnp.concatenate([top[:, 0:q0], top_diag], axis=1) if pair else top_diag
            bot = s_main[tq:]
            ext = jnp.where(causal, s_ext, neg)
            m_top = jnp.max(top, axis=-1, keepdims=True)
            m_bot = jnp.maximum(jnp.max(bot, axis=-1, keepdims=True),
                                jnp.max(ext, axis=-1, keepdims=True))
            p_main = jnp.concatenate([jnp.exp2(top - m_top), jnp.exp2(bot - m_bot)], axis=0)
            acc = _dot(p_main.astype(BF16), vf_ref[hh, 0:k1, :])
            acc_bot = acc[tq:] + _dot(jnp.exp2(ext - m_bot).astype(BF16), vf_ref[hh, k1:k2, :])
            lanes = slice(hh * LANES, (hh + 1) * LANES)
            o_ref[0, q0:k1, lanes] = (acc[0:tq, 0:LANES] / acc[0:tq, LANES:]).astype(o_ref.dtype)
            o_ref[0, k1:k2, lanes] = (acc_bot[:, 0:LANES] / acc_bot[:, LANES:]).astype(o_ref.dtype)


def _cast_block_specs(n_rows, n_cols, row0, n_steps, n_heads):
    share = 1 if (n_rows // n_steps) % BF16_SUBLANES == 0 else 2
    blk = n_rows * share // n_steps
    assert blk * n_steps == n_rows * share and blk % BF16_SUBLANES == 0 and row0 % blk == 0
    first = row0 // blk
    step = lambda b, h: (b * n_heads + h) // share
    return (pl.BlockSpec((blk, n_cols), lambda b, h: (first + step(b, h), 0)),
            pl.BlockSpec((blk, n_cols), lambda b, h: (step(b, h), 0)))


def _attention(q, k_nope, k_pe, v, casts, tq, hb):
    B, H, S, _ = q.shape
    groups = H // hb
    specs = [_cast_block_specs(n, w.shape[1], r0, B * groups, groups) for w, r0, n in casts]
    outs = pl.pallas_call(
        functools.partial(_attn_kernel, tq=tq, n_cast=len(casts)),
        grid=(B, groups),
        in_specs=[
            pl.BlockSpec((1, hb, S, 2 * LANES), lambda b, g: (b, g, 0, 0)),
            pl.BlockSpec((1, hb, S, LANES), lambda b, g: (b, g, 0, 0)),
            pl.BlockSpec((1, S, LANES), lambda b, g: (b, 0, 0)),
            pl.BlockSpec((1, hb, S, LANES), lambda b, g: (b, g, 0, 0)),
        ] + [s_in for s_in, _ in specs],
        out_specs=[pl.BlockSpec((1, S, hb * LANES), lambda b, g: (b, 0, g))]
        + [s_out for _, s_out in specs],
        out_shape=[jax.ShapeDtypeStruct((B, S, H * V_HEAD_DIM), BF16)]
        + [jax.ShapeDtypeStruct((n, w.shape[1]), BF16) for w, _, n in casts],
        scratch_shapes=[pltpu.VMEM((hb, S, 2 * LANES), BF16), pltpu.VMEM((hb, S, 2 * LANES), BF16)],
        compiler_params=_params(2),
        name="mla_attention",
    )(q, k_nope, k_pe, v, *[w for w, _, _ in casts])
    return outs[0], outs[1:]


def _merge_kernel(attn_ref, a_ref, m_ref, x_ref, woa_ref, wg0_ref, bg0_ref, wout_ref, fg_ref,
                  h_ref, f_ref):
    y_attn = _dot(attn_ref[...], woa_ref[...])
    gate = jax.nn.sigmoid(_dot_t(a_ref[...], wg0_ref[...]) + bg0_ref[...])
    merged = (gate * y_attn + m_ref[...]).astype(BF16)
    h = x_ref[...] + _dot(merged, wout_ref[...])
    h_ref[...] = h
    f_ref[...] = _rms(h, fg_ref[...]).astype(BF16)


def _merge(attn, a, m_sgu, x, w_o_attn, w_g0, b_g0, w_out, ffn_g, tm):
    T, D = x.shape
    row = lambda i: (i, 0)
    tile = pl.BlockSpec((tm, D), row)
    return pl.pallas_call(
        _merge_kernel,
        grid=(T // tm,),
        in_specs=[tile, tile, tile, tile,
                  _resident(w_o_attn.shape), _resident(w_g0.shape), _resident(b_g0.shape),
                  _resident(w_out.shape), _resident(ffn_g.shape)],
        out_specs=[tile, tile],
        out_shape=[jax.ShapeDtypeStruct((T, D), F32), jax.ShapeDtypeStruct((T, D), BF16)],
        compiler_params=_params(1),
        name="merge_outproj",
    )(attn, a, m_sgu, x, w_o_attn, w_g0, b_g0, w_out, ffn_g)


def _ffn_kernel(f_ref, h_ref, wg_ref, wu_ref, wd_ref, ng_ref, o_ref):
    j = pl.program_id(1)

    @pl.when(j == 0)
    def _():
        o_ref[...] = h_ref[...]

    f = f_ref[...]
    gate = _dot(f, wg_ref[...])
    up = _dot(f, wu_ref[...])
    act = (jax.nn.silu(gate) * up).astype(BF16)
    o_ref[...] += _dot(act, wd_ref[...])

    @pl.when(j == pl.num_programs(1) - 1)
    def _():
        o_ref[...] = _rms(o_ref[...], ng_ref[...])


def _ffn(f, h, w_gate, w_up, w_down, final_g, tm, tf):
    T, D = h.shape
    d_ff = w_gate.shape[1]
    row = lambda i, j: (i, 0)
    return pl.pallas_call(
        _ffn_kernel,
        grid=(T // tm, d_ff // tf),
        in_specs=[
            pl.BlockSpec((tm, D), row),
            pl.BlockSpec((tm, D), row),
            pl.BlockSpec((D, tf), lambda i, j: (0, j)),
            pl.BlockSpec((D, tf), lambda i, j: (0, j)),
            pl.BlockSpec((tf, D), lambda i, j: (j, 0)),
            _resident((1, D)),
        ],
        out_specs=pl.BlockSpec((tm, D), row),
        out_shape=jax.ShapeDtypeStruct((T, D), F32),
        compiler_params=_params(2),
        name="swiglu_ffn",
    )(f, h, w_gate, w_up, w_down, final_g)


def kernel(x, positions, norm_mix_g, w_in, b_gate, q_norm_g, w_uq, kv_norm_g, w_ukv, w_o_attn,
           sgu_norm_g, w_sgu, b_sgu, w_o_sgu, w_out, norm_ffn_g, w_gate_ffn, w_up_ffn,
           w_down_ffn, norm_final_g):
    B, S, D = x.shape
    T = B * S
    depth = w_in.shape[0]
    assert depth == 1, "the final norm is fused into the FFN epilogue of a single layer"
    assert w_in.shape[1:] == (D, D_IN)

    cs = _rope_tables(positions).reshape(B, S, LANES)
    row_vec = lambda v: v.reshape(1, -1).astype(F32)

    h = x
    out = None
    for l in range(depth):
        w_in_t = jnp.swapaxes(w_in[l], 0, 1)
        w_lat_t, w_uq_p, w_ukv_p = _prep_inproj_weights(w_in_t, w_uq[l], w_ukv[l])
        b_full = jnp.repeat(b_sgu[l].T, SGU_GROUP_DIM, axis=1).astype(F32)

        a, q, k_nope, v, k_pe = _inproj(
            h, row_vec(norm_mix_g[l]), cs, w_lat_t, row_vec(q_norm_g[l]), row_vec(kv_norm_g[l]),
            w_uq_p, w_ukv_p, tm=512, sub=256)
        a2 = a.reshape(T, D)
        whole = lambda w: (w, 0, w.shape[0])
        attn, (w_uv_t, w_g0_t, w_g1_t, w_os, w_oa, w_o, w_gf, w_uf, w_df) = _attention(
            q, k_nope, k_pe, v,
            [(w_in_t, UV_OFF, 2 * SGU_WIDTH), (w_in_t, GATE_OFF, D), (w_in_t, GATE_OFF + D, D),
             whole(w_o_sgu[l]), whole(w_o_attn[l]), whole(w_out[l]),
             whole(w_gate_ffn[l]), whole(w_up_ffn[l]), whole(w_down_ffn[l])], tq=256, hb=2)
        m_sgu = _sgu_branch(
            a2, w_uv_t, row_vec(sgu_norm_g[l]), w_sgu[l], b_full, w_os, w_g1_t,
            row_vec(b_gate[l, D:]), tm=512, sub=256)
        h_mid, f = _merge(
            attn.reshape(T, D), a2, m_sgu, h.reshape(T, D), w_oa, w_g0_t,
            row_vec(b_gate[l, :D]), w_o, row_vec(norm_ffn_g[l]), tm=256)
        out = _ffn(f, h_mid, w_gf, w_uf, w_df, row_vec(norm_final_g), tm=1024, tf=256)
        h = out.reshape(B, S, D)
    return h
```

```python
import functools

import jax
import jax.numpy as jnp
from jax import lax
from jax.experimental import pallas as pl
from jax.experimental.pallas import tpu as pltpu

D_MODEL = 2048
N_HEADS = 16
QK_NOPE_DIM = 128
QK_ROPE_DIM = 64
V_HEAD_DIM = 128
Q_LORA_RANK = 512
KV_LORA_RANK = 512
ROPE_THETA = 10000.0
SGU_GROUPS = 8
SGU_GROUP_DIM = 128
SGU_WIDTH = SGU_GROUPS * SGU_GROUP_DIM
CHUNK = 128
N_BRANCH = 2
RMS_EPS = 1e-6
KPE_OFF = Q_LORA_RANK + KV_LORA_RANK
UV_OFF = KPE_OFF + QK_ROPE_DIM
GATE_OFF = UV_OFF + 2 * SGU_WIDTH
D_IN = GATE_OFF + N_BRANCH * D_MODEL
LANES = 128
HALF_ROPE = QK_ROPE_DIM // 2
LOG2_E = 1.4426950408889634
QK_LOG2_SCALE = (QK_NOPE_DIM + QK_ROPE_DIM) ** -0.5 * LOG2_E
BF16_SUBLANES = 16

VMEM_LIMIT_BYTES = 60 * 1024 * 1024

F32 = jnp.float32
BF16 = jnp.bfloat16


def _rms(x, g):
    return x * lax.rsqrt(jnp.mean(x * x, axis=-1, keepdims=True) + RMS_EPS) * g


def _dot(a, b):
    return jnp.dot(a, b, preferred_element_type=F32)


def _dot_t(a, b_t):
    return lax.dot_general(a, b_t, (((1,), (1,)), ((), ())), preferred_element_type=F32)


def _resident(shape):
    return pl.BlockSpec(shape, lambda *_: (0,) * len(shape), pipeline_mode=pl.Buffered(1))


def _params(n_axes):
    return pltpu.CompilerParams(
        dimension_semantics=("arbitrary",) * n_axes, vmem_limit_bytes=VMEM_LIMIT_BYTES)


def _rope_table_kernel(pos_ref, freq_ref, cos_ref, sin_ref):
    ang = pos_ref[...] * freq_ref[...]
    cos_ref[...] = jnp.cos(ang)
    sin_ref[...] = jnp.sin(ang)


def _rope_tables(positions):
    n_tok = positions.size
    per_row = LANES // HALF_ROPE
    inv_freq = ROPE_THETA ** (-jnp.arange(0, QK_ROPE_DIM, 2, dtype=F32) / QK_ROPE_DIM)
    pos_rep = jnp.repeat(positions.astype(F32).reshape(n_tok // per_row, per_row), HALF_ROPE, axis=1)
    freq = jnp.tile(inv_freq, per_row).reshape(1, LANES)
    shape = jax.ShapeDtypeStruct((n_tok // per_row, LANES), F32)
    cos, sin = pl.pallas_call(
        _rope_table_kernel, out_shape=(shape, shape), name="rope_tables")(pos_rep, freq)
    cos = cos.reshape(n_tok, HALF_ROPE)
    sin = sin.reshape(n_tok, HALF_ROPE)
    return jnp.concatenate([cos, cos, -sin, sin], axis=-1)


def _rope_dup(x, cs):
    y = x * cs
    return y + pltpu.roll(y, LANES // 2, 1)


def _dup_rope_lanes(src):
    lane = lax.broadcasted_iota(jnp.int32, src.shape, 1)
    return jnp.where(lane < 2 * HALF_ROPE, src,
                     jnp.where(lane < 3 * HALF_ROPE, pltpu.roll(src, HALF_ROPE, 1),
                               pltpu.roll(src, 3 * HALF_ROPE, 1)))


def _prep_kernel(wlat_ref, wuq_ref, wukv_ref, olat_ref, ouq_ref, oukv_ref):
    olat_ref[0:KPE_OFF] = wlat_ref[0:KPE_OFF].astype(BF16)
    x1 = wlat_ref[KPE_OFF:KPE_OFF + HALF_ROPE].astype(BF16)
    x2 = wlat_ref[KPE_OFF + HALF_ROPE:UV_OFF].astype(BF16)
    for k, part in enumerate((x1, x2, x2, x1)):
        olat_ref[KPE_OFF + k * HALF_ROPE:KPE_OFF + (k + 1) * HALF_ROPE] = part
    oukv_ref[...] = wukv_ref[...].astype(BF16)
    half = LANES // 2
    nope_cols = N_HEADS * QK_NOPE_DIM
    lane = lax.broadcasted_iota(jnp.int32, (wuq_ref.shape[0], LANES), 1)
    for pair in range(N_HEADS // 2):
        t0, t1, t2 = (wuq_ref[:, (3 * pair + k) * LANES:(3 * pair + k + 1) * LANES] for k in range(3))
        r1 = pltpu.roll(t1, half, 1)
        r2 = pltpu.roll(t2, half, 1)
        heads = ((2 * pair, t0, t1), (2 * pair + 1, jnp.where(lane < half, r1, r2), r2))
        for h, nope, rope_src in heads:
            ouq_ref[:, h * LANES:(h + 1) * LANES] = nope.astype(BF16)
            ouq_ref[:, nope_cols + h * LANES:nope_cols + (h + 1) * LANES] = (
                _dup_rope_lanes(rope_src).astype(BF16))


def _prep_inproj_weights(w_in_t, w_uq, w_ukv, n_chunks=4):
    d = w_in_t.shape[1]
    r_q, r_kv = w_uq.shape[0], w_ukv.shape[0]
    lat_rows = UV_OFF + LANES - QK_ROPE_DIM
    uq_cols = N_HEADS * (QK_NOPE_DIM + LANES)
    return pl.pallas_call(
        _prep_kernel,
        grid=(n_chunks,),
        in_specs=[
            pl.BlockSpec((UV_OFF, d // n_chunks), lambda c: (0, c)),
            pl.BlockSpec((r_q // n_chunks, w_uq.shape[1]), lambda c: (c, 0)),
            pl.BlockSpec((r_kv // n_chunks, w_ukv.shape[1]), lambda c: (c, 0)),
        ],
        out_specs=[
            pl.BlockSpec((lat_rows, d // n_chunks), lambda c: (0, c)),
            pl.BlockSpec((r_q // n_chunks, uq_cols), lambda c: (c, 0)),
            pl.BlockSpec((r_kv // n_chunks, w_ukv.shape[1]), lambda c: (c, 0)),
        ],
        out_shape=[
            jax.ShapeDtypeStruct((lat_rows, d), BF16),
            jax.ShapeDtypeStruct((r_q, uq_cols), BF16),
            jax.ShapeDtypeStruct(w_ukv.shape, BF16),
        ],
        compiler_params=_params(1),
        name="prep_inproj_weights",
    )(w_in_t, w_uq, w_ukv)


def _inproj_kernel(x_ref, g_ref, cs_ref, wlat_ref, qg_ref, kvg_ref, wuq_ref, wukv_ref,
                   a_ref, q_ref, kn_ref, v_ref, kpe_ref, *, sub):
    def latents(r0):
        a = _rms(x_ref[0, r0:r0 + sub, :], g_ref[...]).astype(BF16)
        a_ref[0, r0:r0 + sub, :] = a
        return _dot_t(a, wlat_ref[...])

    starts = list(range(0, x_ref.shape[1], sub))
    for r0, z in zip(starts, [latents(r0) for r0 in starts]):
        rows = slice(r0, r0 + sub)
        qn = (_rms(z[:, :Q_LORA_RANK], qg_ref[...]) * QK_LOG2_SCALE).astype(BF16)
        kvn = _rms(z[:, Q_LORA_RANK:Q_LORA_RANK + KV_LORA_RANK], kvg_ref[...]).astype(BF16)
        cs = cs_ref[0, rows, :]
        kpe = _rope_dup(z[:, Q_LORA_RANK + KV_LORA_RANK:], cs)
        lane = lax.broadcasted_iota(jnp.int32, kpe.shape, 1)
        kpe_ref[0, rows, :] = jnp.where(lane < QK_ROPE_DIM, kpe, 0.0).astype(BF16)

        heads_per_dot = 4
        width = heads_per_dot * LANES
        nope_cols = N_HEADS * QK_NOPE_DIM
        for hg in range(N_HEADS // heads_per_dot):
            c0 = hg * width
            q_nope = _dot(qn, wuq_ref[:, c0:c0 + width])
            q_pe = _dot(qn, wuq_ref[:, nope_cols + c0:nope_cols + c0 + width])
            kv0 = 2 * c0
            kv_a = _dot(kvn, wukv_ref[:, kv0:kv0 + width])
            kv_b = _dot(kvn, wukv_ref[:, kv0 + width:kv0 + 2 * width])
            for hh in range(heads_per_dot):
                h = hg * heads_per_dot + hh
                sl = slice(hh * LANES, (hh + 1) * LANES)
                q_ref[0, h, rows, 0:LANES] = q_nope[:, sl].astype(BF16)
                q_ref[0, h, rows, LANES:2 * LANES] = _rope_dup(q_pe[:, sl], cs).astype(BF16)
                kv = kv_a if hh < heads_per_dot // 2 else kv_b
                k0 = (hh % (heads_per_dot // 2)) * 2 * LANES
                kn_ref[0, h, rows, :] = kv[:, k0:k0 + LANES].astype(BF16)
                v_ref[0, h, rows, :] = kv[:, k0 + LANES:k0 + 2 * LANES].astype(BF16)


def _inproj(x, norm_g, cs, w_lat_t, q_g, kv_g, w_uq, w_ukv, tm, sub):
    B, S, D = x.shape
    row = lambda b, i: (b, i, 0)
    head = lambda b, i: (b, 0, i, 0)
    return pl.pallas_call(
        functools.partial(_inproj_kernel, sub=sub),
        grid=(B, S // tm),
        in_specs=[
            pl.BlockSpec((1, tm, D), row),
            _resident((1, D)),
            pl.BlockSpec((1, tm, LANES), row),
            _resident(w_lat_t.shape),
            _resident((1, Q_LORA_RANK)),
            _resident((1, KV_LORA_RANK)),
            _resident(w_uq.shape),
            _resident(w_ukv.shape),
        ],
        out_specs=[
            pl.BlockSpec((1, tm, D), row),
            pl.BlockSpec((1, N_HEADS, tm, 2 * LANES), head),
            pl.BlockSpec((1, N_HEADS, tm, LANES), head),
            pl.BlockSpec((1, N_HEADS, tm, LANES), head),
            pl.BlockSpec((1, tm, LANES), row),
        ],
        out_shape=[
            jax.ShapeDtypeStruct((B, S, D), BF16),
            jax.ShapeDtypeStruct((B, N_HEADS, S, 2 * LANES), BF16),
            jax.ShapeDtypeStruct((B, N_HEADS, S, LANES), BF16),
            jax.ShapeDtypeStruct((B, N_HEADS, S, LANES), BF16),
            jax.ShapeDtypeStruct((B, S, LANES), BF16),
        ],
        compiler_params=_params(2),
        name="inproj",
    )(x, norm_g, cs, w_lat_t, q_g, kv_g, w_uq, w_ukv)


def _sgu_kernel(a_ref, wuv_ref, sg_ref, ws_ref, bfull_ref, wos_ref, wg1_ref, bg1_ref, m_ref, *, sub):
    n_chunks = sub // CHUNK
    t_idx = lax.broadcasted_iota(jnp.int32, (CHUNK, CHUNK), 0)
    s_idx = lax.broadcasted_iota(jnp.int32, (CHUNK, CHUNK), 1)
    causal = t_idx >= s_idx
    ws = [jnp.where(causal, ws_ref[g], 0.0).astype(BF16) for g in range(SGU_GROUPS)]
    bfull = bfull_ref[...]
    def gating_unit(uv_raw):
        uv = jax.nn.gelu(uv_raw)
        u = uv[:, :SGU_WIDTH]
        vn = _rms(uv[:, SGU_WIDTH:], sg_ref[...]).astype(BF16)
        mixed_cols = []
        for g in range(SGU_GROUPS):
            gs = slice(g * SGU_GROUP_DIM, (g + 1) * SGU_GROUP_DIM)
            rhs = jnp.concatenate(
                [vn[c * CHUNK:(c + 1) * CHUNK, gs] for c in range(n_chunks)], axis=1)
            mixed_cols.append(_dot(ws[g], rhs))
        rows = []
        for c in range(n_chunks):
            cs = slice(c * SGU_GROUP_DIM, (c + 1) * SGU_GROUP_DIM)
            mixed = jnp.concatenate([mixed_cols[g][:, cs] for g in range(SGU_GROUPS)], axis=1)
            rows.append(u[c * CHUNK:(c + 1) * CHUNK] * (mixed + bfull))
        return jnp.concatenate(rows, axis=0).astype(BF16)

    a = a_ref[...]
    uv_raw = _dot_t(a, wuv_ref[...])
    gate_raw = _dot_t(a, wg1_ref[...])
    for r0 in range(0, a_ref.shape[0], sub):
        rows = slice(r0, r0 + sub)
        y_sgu = _dot(gating_unit(uv_raw[rows]), wos_ref[...])
        m_ref[rows, :] = jax.nn.sigmoid(gate_raw[rows] + bg1_ref[...]) * y_sgu


def _sgu_branch(a, w_uv, sgu_g, w_s, b_full, w_o_sgu, w_g1, b_g1, tm, sub):
    T, D = a.shape
    row = lambda i: (i, 0)
    return pl.pallas_call(
        functools.partial(_sgu_kernel, sub=sub),
        grid=(T // tm,),
        in_specs=[
            pl.BlockSpec((tm, D), row),
            _resident(w_uv.shape),
            _resident(sgu_g.shape),
            _resident(w_s.shape),
            _resident(b_full.shape),
            _resident(w_o_sgu.shape),
            _resident(w_g1.shape),
            _resident(b_g1.shape),
        ],
        out_specs=pl.BlockSpec((tm, D), row),
        out_shape=jax.ShapeDtypeStruct((T, D), F32),
        compiler_params=_params(1),
        name="sgu_branch",
    )(a, w_uv, sgu_g, w_s, b_full, w_o_sgu, w_g1, b_g1)


def _attn_kernel(q_ref, kn_ref, kpe_ref, v_ref, *rest, tq, n_cast):
    cast_in, (o_ref,), cast_out, (kf_ref, vf_ref) = (
        rest[:n_cast], rest[n_cast:n_cast + 1], rest[n_cast + 1:2 * n_cast + 1], rest[2 * n_cast + 1:])
    for src, dst in zip(cast_in, cast_out):
        dst[...] = src[...].astype(dst.dtype)
    seq = q_ref.shape[2]
    row = lax.broadcasted_iota(jnp.int32, (tq, tq), 0)
    col = lax.broadcasted_iota(jnp.int32, (tq, tq), 1)
    causal = row >= col
    neg = jnp.finfo(F32).min
    heads = q_ref.shape[1]
    for hh in range(heads):
        kf_ref[hh, :, 0:LANES] = kn_ref[0, hh]
        kf_ref[hh, :, LANES:2 * LANES] = kpe_ref[0]
        vf_ref[hh, :, 0:LANES] = v_ref[0, hh]
        vf_ref[hh, :, LANES:2 * LANES] = jnp.ones((seq, LANES), vf_ref.dtype)

    nt = (((1,), (1,)), ((), ()))
    for pair in reversed(range(seq // (2 * tq))):
        q0 = pair * 2 * tq
        k1, k2 = q0 + tq, q0 + 2 * tq
        scores = []
        for hh in range(heads):
            q = q_ref[0, hh, q0:k2, :]
            scores.append((
                lax.dot_general(q, kf_ref[hh, 0:k1, :], nt, preferred_element_type=F32),
                lax.dot_general(q[tq:], kf_ref[hh, k1:k2, :], nt, preferred_element_type=F32)))
        for hh, (s_main, s_ext) in enumerate(scores):
            top = s_main[0:tq]
            top_diag = jnp.where(causal, top[:, q0:k1], neg)
            top = jnp.concatenate([top[:, 0:q0], top_diag], axis=1) if pair else top_diag
            bot = s_main[tq:]
            ext = jnp.where(causal, s_ext, neg)
            m_top = jnp.max(top, axis=-1, keepdims=True)
            m_bot = jnp.maximum(jnp.max(bot, axis=-1, keepdims=True),
                                jnp.max(ext, axis=-1, keepdims=True))
            p_main = jnp.concatenate([jnp.exp2(top - m_top), jnp.exp2(bot - m_bot)], axis=0)
            acc = _dot(p_main.astype(BF16), vf_ref[hh, 0:k1, :])
            acc_bot = acc[tq:] + _dot(jnp.exp2(ext - m_bot).astype(BF16), vf_ref[hh, k1:k2, :])
            lanes = slice(hh * LANES, (hh + 1) * LANES)
            o_ref[0, q0:k1, lanes] = (acc[0:tq, 0:LANES] / acc[0:tq, LANES:]).astype(o_ref.dtype)
            o_ref[0, k1:k2, lanes] = (acc_bot[:, 0:LANES] / acc_bot[:, LANES:]).astype(o_ref.dtype)


def _cast_block_specs(n_rows, n_cols, row0, n_steps, n_heads):
    share = 1 if (n_rows // n_steps) % BF16_SUBLANES == 0 else 2
    blk = n_rows * share // n_steps
    assert blk * n_steps == n_rows * share and blk % BF16_SUBLANES == 0 and row0 % blk == 0
    first = row0 // blk
    step = lambda b, h: (b * n_heads + h) // share
    return (pl.BlockSpec((blk, n_cols), lambda b, h: (first + step(b, h), 0)),
            pl.BlockSpec((blk, n_cols), lambda b, h: (step(b, h), 0)))


def _attention(q, k_nope, k_pe, v, casts, tq, hb):
    B, H, S, _ = q.shape
    groups = H // hb
    specs = [_cast_block_specs(n, w.shape[1], r0, B * groups, groups) for w, r0, n in casts]
    outs = pl.pallas_call(
        functools.partial(_attn_kernel, tq=tq, n_cast=len(casts)),
        grid=(B, groups),
        in_specs=[
            pl.BlockSpec((1, hb, S, 2 * LANES), lambda b, g: (b, g, 0, 0)),
            pl.BlockSpec((1, hb, S, LANES), lambda b, g: (b, g, 0, 0)),
            pl.BlockSpec((1, S, LANES), lambda b, g: (b, 0, 0)),
            pl.BlockSpec((1, hb, S, LANES), lambda b, g: (b, g, 0, 0)),
        ] + [s_in for s_in, _ in specs],
        out_specs=[pl.BlockSpec((1, S, hb * LANES), lambda b, g: (b, 0, g))]
        + [s_out for _, s_out in specs],
        out_shape=[jax.ShapeDtypeStruct((B, S, H * V_HEAD_DIM), BF16)]
        + [jax.ShapeDtypeStruct((n, w.shape[1]), BF16) for w, _, n in casts],
        scratch_shapes=[pltpu.VMEM((hb, S, 2 * LANES), BF16), pltpu.VMEM((hb, S, 2 * LANES), BF16)],
        compiler_params=_params(2),
        name="mla_attention",
    )(q, k_nope, k_pe, v, *[w for w, _, _ in casts])
    return outs[0], outs[1:]


def _merge_kernel(attn_ref, a_ref, m_ref, x_ref, woa_ref, wg0_ref, bg0_ref, wout_ref, fg_ref,
                  h_ref, f_ref):
    y_attn = _dot(attn_ref[...], woa_ref[...])
    gate = jax.nn.sigmoid(_dot_t(a_ref[...], wg0_ref[...]) + bg0_ref[...])
    merged = (gate * y_attn + m_ref[...]).astype(BF16)
    h = x_ref[...] + _dot(merged, wout_ref[...])
    h_ref[...] = h
    f_ref[...] = _rms(h, fg_ref[...]).astype(BF16)


def _merge(attn, a, m_sgu, x, w_o_attn, w_g0, b_g0, w_out, ffn_g, tm):
    T, D = x.shape
    row = lambda i: (i, 0)
    tile = pl.BlockSpec((tm, D), row)
    return pl.pallas_call(
        _merge_kernel,
        grid=(T // tm,),
        in_specs=[tile, tile, tile, tile,
                  _resident(w_o_attn.shape), _resident(w_g0.shape), _resident(b_g0.shape),
                  _resident(w_out.shape), _resident(ffn_g.shape)],
        out_specs=[tile, tile],
        out_shape=[jax.ShapeDtypeStruct((T, D), F32), jax.ShapeDtypeStruct((T, D), BF16)],
        compiler_params=_params(1),
        name="merge_outproj",
    )(attn, a, m_sgu, x, w_o_attn, w_g0, b_g0, w_out, ffn_g)


def _ffn_kernel(f_ref, h_ref, wg_ref, wu_ref, wd_ref, ng_ref, o_ref, *, sub):
    j = pl.program_id(1)

    @pl.when(j == 0)
    def _():
        o_ref[...] = h_ref[...]

    f = f_ref[...]
    starts = list(range(0, wg_ref.shape[1], sub))
    projected = [(_dot(f, wg_ref[:, c0:c0 + sub]), _dot(f, wu_ref[:, c0:c0 + sub]))
                 for c0 in starts]
    part = None
    for c0, (gate, up) in zip(starts, projected):
        act = (jax.nn.silu(gate) * up).astype(BF16)
        down = _dot(act, wd_ref[c0:c0 + sub, :])
        part = down if part is None else part + down
    o_ref[...] += part

    @pl.when(j == pl.num_programs(1) - 1)
    def _():
        o_ref[...] = _rms(o_ref[...], ng_ref[...])


def _ffn(f, h, w_gate, w_up, w_down, final_g, tm, tf, sub):
    T, D = h.shape
    d_ff = w_gate.shape[1]
    row = lambda i, j: (i, 0)
    return pl.pallas_call(
        functools.partial(_ffn_kernel, sub=sub),
        grid=(T // tm, d_ff // tf),
        in_specs=[
            pl.BlockSpec((tm, D), row),
            pl.BlockSpec((tm, D), row),
            pl.BlockSpec((D, tf), lambda i, j: (0, j)),
            pl.BlockSpec((D, tf), lambda i, j: (0, j)),
            pl.BlockSpec((tf, D), lambda i, j: (j, 0)),
            _resident((1, D)),
        ],
        out_specs=pl.BlockSpec((tm, D), row),
        out_shape=jax.ShapeDtypeStruct((T, D), F32),
        compiler_params=_params(2),
        name="swiglu_ffn",
    )(f, h, w_gate, w_up, w_down, final_g)


def kernel(x, positions, norm_mix_g, w_in, b_gate, q_norm_g, w_uq, kv_norm_g, w_ukv, w_o_attn,
           sgu_norm_g, w_sgu, b_sgu, w_o_sgu, w_out, norm_ffn_g, w_gate_ffn, w_up_ffn,
           w_down_ffn, norm_final_g):
    B, S, D = x.shape
    T = B * S
    depth = w_in.shape[0]
    assert depth == 1, "the final norm is fused into the FFN epilogue of a single layer"
    assert w_in.shape[1:] == (D, D_IN)

    cs = _rope_tables(positions).reshape(B, S, LANES)
    row_vec = lambda v: v.reshape(1, -1).astype(F32)

    h = x
    out = None
    for l in range(depth):
        w_in_t = jnp.swapaxes(w_in[l], 0, 1)
        w_lat_t, w_uq_p, w_ukv_p = _prep_inproj_weights(w_in_t, w_uq[l], w_ukv[l])
        b_full = jnp.repeat(b_sgu[l].T, SGU_GROUP_DIM, axis=1).astype(F32)

        a, q, k_nope, v, k_pe = _inproj(
            h, row_vec(norm_mix_g[l]), cs, w_lat_t, row_vec(q_norm_g[l]), row_vec(kv_norm_g[l]),
            w_uq_p, w_ukv_p, tm=512, sub=256)
        a2 = a.reshape(T, D)
        whole = lambda w: (w, 0, w.shape[0])
        attn, (w_uv_t, w_g0_t, w_g1_t, w_os, w_oa, w_o, w_gf, w_uf, w_df) = _attention(
            q, k_nope, k_pe, v,
            [(w_in_t, UV_OFF, 2 * SGU_WIDTH), (w_in_t, GATE_OFF, D), (w_in_t, GATE_OFF + D, D),
             whole(w_o_sgu[l]), whole(w_o_attn[l]), whole(w_out[l]),
             whole(w_gate_ffn[l]), whole(w_up_ffn[l]), whole(w_down_ffn[l])], tq=256, hb=2)
        m_sgu = _sgu_branch(
            a2, w_uv_t, row_vec(sgu_norm_g[l]), w_sgu[l], b_full, w_os, w_g1_t,
            row_vec(b_gate[l, D:]), tm=512, sub=256)
        h_mid, f = _merge(
            attn.reshape(T, D), a2, m_sgu, h.reshape(T, D), w_oa, w_g0_t,
            row_vec(b_gate[l, :D]), w_o, row_vec(norm_ffn_g[l]), tm=256)
        out = _ffn(f, h_mid, w_gf, w_uf, w_df, row_vec(norm_final_g), tm=1024, tf=512,
                   sub=256)
        h = out.reshape(B, S, D)
    return h
```

```python
import functools

import jax
import jax.numpy as jnp
from jax import lax
from jax.experimental import pallas as pl
from jax.experimental.pallas import tpu as pltpu

D_MODEL = 2048
N_HEADS = 16
QK_NOPE_DIM = 128
QK_ROPE_DIM = 64
V_HEAD_DIM = 128
Q_LORA_RANK = 512
KV_LORA_RANK = 512
ROPE_THETA = 10000.0
SGU_GROUPS = 8
SGU_GROUP_DIM = 128
SGU_WIDTH = SGU_GROUPS * SGU_GROUP_DIM
CHUNK = 128
N_BRANCH = 2
RMS_EPS = 1e-6
KPE_OFF = Q_LORA_RANK + KV_LORA_RANK
UV_OFF = KPE_OFF + QK_ROPE_DIM
GATE_OFF = UV_OFF + 2 * SGU_WIDTH
D_IN = GATE_OFF + N_BRANCH * D_MODEL
LANES = 128
HALF_ROPE = QK_ROPE_DIM // 2
LOG2_E = 1.4426950408889634
QK_LOG2_SCALE = (QK_NOPE_DIM + QK_ROPE_DIM) ** -0.5 * LOG2_E
BF16_SUBLANES = 16

VMEM_LIMIT_BYTES = 60 * 1024 * 1024

F32 = jnp.float32
BF16 = jnp.bfloat16


def _rms(x, g):
    return x * lax.rsqrt(jnp.mean(x * x, axis=-1, keepdims=True) + RMS_EPS) * g


def _dot(a, b):
    return jnp.dot(a, b, preferred_element_type=F32)


def _dot_t(a, b_t):
    return lax.dot_general(a, b_t, (((1,), (1,)), ((), ())), preferred_element_type=F32)


def _resident(shape):
    return pl.BlockSpec(shape, lambda *_: (0,) * len(shape), pipeline_mode=pl.Buffered(1))


def _params(n_axes):
    return pltpu.CompilerParams(
        dimension_semantics=("arbitrary",) * n_axes, vmem_limit_bytes=VMEM_LIMIT_BYTES)


def _cast_block_specs(n_rows, n_cols, row0, n_steps, linear_step):
    share = 1 if (n_rows // n_steps) % BF16_SUBLANES == 0 else 2
    blk = n_rows * share // n_steps
    assert blk * n_steps == n_rows * share and blk % BF16_SUBLANES == 0 and row0 % blk == 0
    first = row0 // blk
    return (pl.BlockSpec((blk, n_cols), lambda *idx: (first + linear_step(*idx) // share, 0)),
            pl.BlockSpec((blk, n_cols), lambda *idx: (linear_step(*idx) // share, 0)))


def _cast_rows(srcs, dsts):
    for src, dst in zip(srcs, dsts):
        dst[...] = src[...].astype(dst.dtype)


def _rope_table_kernel(pos_ref, freq_ref, cos_ref, sin_ref):
    ang = pos_ref[...] * freq_ref[...]
    cos_ref[...] = jnp.cos(ang)
    sin_ref[...] = jnp.sin(ang)


def _rope_tables(positions):
    n_tok = positions.size
    per_row = LANES // HALF_ROPE
    inv_freq = ROPE_THETA ** (-jnp.arange(0, QK_ROPE_DIM, 2, dtype=F32) / QK_ROPE_DIM)
    pos_rep = jnp.repeat(positions.astype(F32).reshape(n_tok // per_row, per_row), HALF_ROPE, axis=1)
    freq = jnp.tile(inv_freq, per_row).reshape(1, LANES)
    shape = jax.ShapeDtypeStruct((n_tok // per_row, LANES), F32)
    cos, sin = pl.pallas_call(
        _rope_table_kernel, out_shape=(shape, shape), name="rope_tables")(pos_rep, freq)
    cos = cos.reshape(n_tok, HALF_ROPE)
    sin = sin.reshape(n_tok, HALF_ROPE)
    return jnp.concatenate([cos, cos, -sin, sin], axis=-1)


def _rope_dup(x, cs):
    y = x * cs
    return y + pltpu.roll(y, LANES // 2, 1)


def _dup_rope_lanes(src):
    lane = lax.broadcasted_iota(jnp.int32, src.shape, 1)
    return jnp.where(lane < 2 * HALF_ROPE, src,
                     jnp.where(lane < 3 * HALF_ROPE, pltpu.roll(src, HALF_ROPE, 1),
                               pltpu.roll(src, 3 * HALF_ROPE, 1)))


def _prep_kernel(wlat_ref, wuq_ref, wukv_ref, olat_ref, ouq_ref, oukv_ref):
    olat_ref[0:KPE_OFF] = wlat_ref[0:KPE_OFF].astype(BF16)
    x1 = wlat_ref[KPE_OFF:KPE_OFF + HALF_ROPE].astype(BF16)
    x2 = wlat_ref[KPE_OFF + HALF_ROPE:UV_OFF].astype(BF16)
    for k, part in enumerate((x1, x2, x2, x1)):
        olat_ref[KPE_OFF + k * HALF_ROPE:KPE_OFF + (k + 1) * HALF_ROPE] = part
    oukv_ref[...] = wukv_ref[...].astype(BF16)
    half = LANES // 2
    nope_cols = N_HEADS * QK_NOPE_DIM
    lane = lax.broadcasted_iota(jnp.int32, (wuq_ref.shape[0], LANES), 1)
    for pair in range(N_HEADS // 2):
        t0, t1, t2 = (wuq_ref[:, (3 * pair + k) * LANES:(3 * pair + k + 1) * LANES] for k in range(3))
        r1 = pltpu.roll(t1, half, 1)
        r2 = pltpu.roll(t2, half, 1)
        heads = ((2 * pair, t0, t1), (2 * pair + 1, jnp.where(lane < half, r1, r2), r2))
        for h, nope, rope_src in heads:
            ouq_ref[:, h * LANES:(h + 1) * LANES] = nope.astype(BF16)
            ouq_ref[:, nope_cols + h * LANES:nope_cols + (h + 1) * LANES] = (
                _dup_rope_lanes(rope_src).astype(BF16))


def _prep_inproj_weights(w_in_t, w_uq, w_ukv, n_chunks=4):
    d = w_in_t.shape[1]
    r_q, r_kv = w_uq.shape[0], w_ukv.shape[0]
    lat_rows = UV_OFF + LANES - QK_ROPE_DIM
    uq_cols = N_HEADS * (QK_NOPE_DIM + LANES)
    return pl.pallas_call(
        _prep_kernel,
        grid=(n_chunks,),
        in_specs=[
            pl.BlockSpec((UV_OFF, d // n_chunks), lambda c: (0, c)),
            pl.BlockSpec((r_q // n_chunks, w_uq.shape[1]), lambda c: (c, 0)),
            pl.BlockSpec((r_kv // n_chunks, w_ukv.shape[1]), lambda c: (c, 0)),
        ],
        out_specs=[
            pl.BlockSpec((lat_rows, d // n_chunks), lambda c: (0, c)),
            pl.BlockSpec((r_q // n_chunks, uq_cols), lambda c: (c, 0)),
            pl.BlockSpec((r_kv // n_chunks, w_ukv.shape[1]), lambda c: (c, 0)),
        ],
        out_shape=[
            jax.ShapeDtypeStruct((lat_rows, d), BF16),
            jax.ShapeDtypeStruct((r_q, uq_cols), BF16),
            jax.ShapeDtypeStruct(w_ukv.shape, BF16),
        ],
        compiler_params=_params(1),
        name="prep_inproj_weights",
    )(w_in_t, w_uq, w_ukv)


def _inproj_kernel(x_ref, g_ref, cs_ref, wlat_ref, qg_ref, kvg_ref, wuq_ref, wukv_ref,
                   a_ref, q_ref, kn_ref, v_ref, kpe_ref, *, sub):
    def latents(r0):
        a = _rms(x_ref[0, r0:r0 + sub, :], g_ref[...]).astype(BF16)
        a_ref[0, r0:r0 + sub, :] = a
        return _dot_t(a, wlat_ref[...])

    starts = list(range(0, x_ref.shape[1], sub))
    for r0, z in zip(starts, [latents(r0) for r0 in starts]):
        rows = slice(r0, r0 + sub)
        qn = (_rms(z[:, :Q_LORA_RANK], qg_ref[...]) * QK_LOG2_SCALE).astype(BF16)
        kvn = _rms(z[:, Q_LORA_RANK:Q_LORA_RANK + KV_LORA_RANK], kvg_ref[...]).astype(BF16)
        cs = cs_ref[0, rows, :]
        kpe = _rope_dup(z[:, Q_LORA_RANK + KV_LORA_RANK:], cs)
        lane = lax.broadcasted_iota(jnp.int32, kpe.shape, 1)
        kpe_ref[0, rows, :] = jnp.where(lane < QK_ROPE_DIM, kpe, 0.0).astype(BF16)

        heads_per_dot = 4
        width = heads_per_dot * LANES
        nope_cols = N_HEADS * QK_NOPE_DIM
        for hg in range(N_HEADS // heads_per_dot):
            c0 = hg * width
            q_nope = _dot(qn, wuq_ref[:, c0:c0 + width])
            q_pe = _dot(qn, wuq_ref[:, nope_cols + c0:nope_cols + c0 + width])
            kv0 = 2 * c0
            kv_a = _dot(kvn, wukv_ref[:, kv0:kv0 + width])
            kv_b = _dot(kvn, wukv_ref[:, kv0 + width:kv0 + 2 * width])
            for hh in range(heads_per_dot):
                h = hg * heads_per_dot + hh
                sl = slice(hh * LANES, (hh + 1) * LANES)
                q_ref[0, h, rows, 0:LANES] = q_nope[:, sl].astype(BF16)
                q_ref[0, h, rows, LANES:2 * LANES] = _rope_dup(q_pe[:, sl], cs).astype(BF16)
                kv = kv_a if hh < heads_per_dot // 2 else kv_b
                k0 = (hh % (heads_per_dot // 2)) * 2 * LANES
                kn_ref[0, h, rows, :] = kv[:, k0:k0 + LANES].astype(BF16)
                v_ref[0, h, rows, :] = kv[:, k0 + LANES:k0 + 2 * LANES].astype(BF16)


def _inproj(x, norm_g, cs, w_lat_t, q_g, kv_g, w_uq, w_ukv, tm, sub):
    B, S, D = x.shape
    row = lambda b, i: (b, i, 0)
    head = lambda b, i: (b, 0, i, 0)
    return pl.pallas_call(
        functools.partial(_inproj_kernel, sub=sub),
        grid=(B, S // tm),
        in_specs=[
            pl.BlockSpec((1, tm, D), row),
            _resident((1, D)),
            pl.BlockSpec((1, tm, LANES), row),
            _resident(w_lat_t.shape),
            _resident((1, Q_LORA_RANK)),
            _resident((1, KV_LORA_RANK)),
            _resident(w_uq.shape),
            _resident(w_ukv.shape),
        ],
        out_specs=[
            pl.BlockSpec((1, tm, D), row),
            pl.BlockSpec((1, N_HEADS, tm, 2 * LANES), head),
            pl.BlockSpec((1, N_HEADS, tm, LANES), head),
            pl.BlockSpec((1, N_HEADS, tm, LANES), head),
            pl.BlockSpec((1, tm, LANES), row),
        ],
        out_shape=[
            jax.ShapeDtypeStruct((B, S, D), BF16),
            jax.ShapeDtypeStruct((B, N_HEADS, S, 2 * LANES), BF16),
            jax.ShapeDtypeStruct((B, N_HEADS, S, LANES), BF16),
            jax.ShapeDtypeStruct((B, N_HEADS, S, LANES), BF16),
            jax.ShapeDtypeStruct((B, S, LANES), BF16),
        ],
        compiler_params=_params(2),
        name="inproj",
    )(x, norm_g, cs, w_lat_t, q_g, kv_g, w_uq, w_ukv)


def _sgu_kernel(a_ref, wuv_ref, sg_ref, ws_ref, bfull_ref, wos_ref, wg1_ref, bg1_ref, cast_ref,
                m_ref, cast_out_ref, *, sub):
    _cast_rows([cast_ref], [cast_out_ref])
    n_chunks = sub // CHUNK
    t_idx = lax.broadcasted_iota(jnp.int32, (CHUNK, CHUNK), 0)
    s_idx = lax.broadcasted_iota(jnp.int32, (CHUNK, CHUNK), 1)
    causal = t_idx >= s_idx
    ws = [jnp.where(causal, ws_ref[g], 0.0).astype(BF16) for g in range(SGU_GROUPS)]
    bfull = bfull_ref[...]
    def gating_unit(uv_raw):
        uv = jax.nn.gelu(uv_raw)
        u = uv[:, :SGU_WIDTH]
        vn = _rms(uv[:, SGU_WIDTH:], sg_ref[...]).astype(BF16)
        mixed_cols = []
        for g in range(SGU_GROUPS):
            gs = slice(g * SGU_GROUP_DIM, (g + 1) * SGU_GROUP_DIM)
            rhs = jnp.concatenate(
                [vn[c * CHUNK:(c + 1) * CHUNK, gs] for c in range(n_chunks)], axis=1)
            mixed_cols.append(_dot(ws[g], rhs))
        rows = []
        for c in range(n_chunks):
            cs = slice(c * SGU_GROUP_DIM, (c + 1) * SGU_GROUP_DIM)
            mixed = jnp.concatenate([mixed_cols[g][:, cs] for g in range(SGU_GROUPS)], axis=1)
            rows.append(u[c * CHUNK:(c + 1) * CHUNK] * (mixed + bfull))
        return jnp.concatenate(rows, axis=0).astype(BF16)

    a = a_ref[...]
    uv_raw = _dot_t(a, wuv_ref[...])
    gate_raw = _dot_t(a, wg1_ref[...])
    for r0 in range(0, a_ref.shape[0], sub):
        rows = slice(r0, r0 + sub)
        y_sgu = _dot(gating_unit(uv_raw[rows]), wos_ref[...])
        m_ref[rows, :] = jax.nn.sigmoid(gate_raw[rows] + bg1_ref[...]) * y_sgu


def _sgu_branch(a, w_uv, sgu_g, w_s, b_full, w_o_sgu, w_g1, b_g1, cast_w, tm, sub):
    T, D = a.shape
    row = lambda i: (i, 0)
    cast_in, cast_out = _cast_block_specs(*cast_w.shape, 0, T // tm, lambda i: i)
    return pl.pallas_call(
        functools.partial(_sgu_kernel, sub=sub),
        grid=(T // tm,),
        in_specs=[
            pl.BlockSpec((tm, D), row),
            _resident(w_uv.shape),
            _resident(sgu_g.shape),
            _resident(w_s.shape),
            _resident(b_full.shape),
            _resident(w_o_sgu.shape),
            _resident(w_g1.shape),
            _resident(b_g1.shape),
            cast_in,
        ],
        out_specs=[pl.BlockSpec((tm, D), row), cast_out],
        out_shape=[jax.ShapeDtypeStruct((T, D), F32), jax.ShapeDtypeStruct(cast_w.shape, BF16)],
        compiler_params=_params(1),
        name="sgu_branch",
    )(a, w_uv, sgu_g, w_s, b_full, w_o_sgu, w_g1, b_g1, cast_w)


def _attn_kernel(q_ref, kn_ref, kpe_ref, v_ref, *rest, tq, n_cast):
    cast_in, (o_ref,), cast_out, (kf_ref, vf_ref) = (
        rest[:n_cast], rest[n_cast:n_cast + 1], rest[n_cast + 1:2 * n_cast + 1], rest[2 * n_cast + 1:])
    _cast_rows(cast_in, cast_out)
    seq = q_ref.shape[2]
    row = lax.broadcasted_iota(jnp.int32, (tq, tq), 0)
    col = lax.broadcasted_iota(jnp.int32, (tq, tq), 1)
    causal = row >= col
    neg = jnp.finfo(F32).min
    heads = q_ref.shape[1]
    for hh in range(heads):
        kf_ref[hh, :, 0:LANES] = kn_ref[0, hh]
        kf_ref[hh, :, LANES:2 * LANES] = kpe_ref[0]
        vf_ref[hh, :, 0:LANES] = v_ref[0, hh]
        vf_ref[hh, :, LANES:2 * LANES] = jnp.ones((seq, LANES), vf_ref.dtype)

    nt = (((1,), (1,)), ((), ()))
    for pair in reversed(range(seq // (2 * tq))):
        q0 = pair * 2 * tq
        k1, k2 = q0 + tq, q0 + 2 * tq
        scores = []
        for hh in range(heads):
            q = q_ref[0, hh, q0:k2, :]
            scores.append((
                lax.dot_general(q, kf_ref[hh, 0:k1, :], nt, preferred_element_type=F32),
                lax.dot_general(q[tq:], kf_ref[hh, k1:k2, :], nt, preferred_element_type=F32)))
        for hh, (s_main, s_ext) in enumerate(scores):
            top = s_main[0:tq]
            top_diag = jnp.where(causal, top[:, q0:k1], neg)
            top = jnp.concatenate([top[:, 0:q0], top_diag], axis=1) if pair else top_diag
            bot = s_main[tq:]
            ext = jnp.where(causal, s_ext, neg)
            m_top = jnp.max(top, axis=-1, keepdims=True)
            m_bot = jnp.maximum(jnp.max(bot, axis=-1, keepdims=True),
                                jnp.max(ext, axis=-1, keepdims=True))
            p_main = jnp.concatenate([jnp.exp2(top - m_top), jnp.exp2(bot - m_bot)], axis=0)
            acc = _dot(p_main.astype(BF16), vf_ref[hh, 0:k1, :])
            acc_bot = acc[tq:] + _dot(jnp.exp2(ext - m_bot).astype(BF16), vf_ref[hh, k1:k2, :])
            lanes = slice(hh * LANES, (hh + 1) * LANES)
            o_ref[0, q0:k1, lanes] = (acc[0:tq, 0:LANES] / acc[0:tq, LANES:]).astype(o_ref.dtype)
            o_ref[0, k1:k2, lanes] = (acc_bot[:, 0:LANES] / acc_bot[:, LANES:]).astype(o_ref.dtype)


def _attention(q, k_nope, k_pe, v, casts, tq, hb):
    B, H, S, _ = q.shape
    groups = H // hb
    specs = [_cast_block_specs(n, w.shape[1], r0, B * groups, lambda b, g: b * groups + g)
             for w, r0, n in casts]
    outs = pl.pallas_call(
        functools.partial(_attn_kernel, tq=tq, n_cast=len(casts)),
        grid=(B, groups),
        in_specs=[
            pl.BlockSpec((1, hb, S, 2 * LANES), lambda b, g: (b, g, 0, 0)),
            pl.BlockSpec((1, hb, S, LANES), lambda b, g: (b, g, 0, 0)),
            pl.BlockSpec((1, S, LANES), lambda b, g: (b, 0, 0)),
            pl.BlockSpec((1, hb, S, LANES), lambda b, g: (b, g, 0, 0)),
        ] + [s_in for s_in, _ in specs],
        out_specs=[pl.BlockSpec((1, S, hb * LANES), lambda b, g: (b, 0, g))]
        + [s_out for _, s_out in specs],
        out_shape=[jax.ShapeDtypeStruct((B, S, H * V_HEAD_DIM), BF16)]
        + [jax.ShapeDtypeStruct((n, w.shape[1]), BF16) for w, _, n in casts],
        scratch_shapes=[pltpu.VMEM((hb, S, 2 * LANES), BF16), pltpu.VMEM((hb, S, 2 * LANES), BF16)],
        compiler_params=_params(2),
        name="mla_attention",
    )(q, k_nope, k_pe, v, *[w for w, _, _ in casts])
    return outs[0], outs[1:]


def _merge_kernel(attn_ref, a_ref, m_ref, x_ref, woa_ref, wg0_ref, bg0_ref, wout_ref, fg_ref,
                  cast_ref, h_ref, f_ref, cast_out_ref):
    _cast_rows([cast_ref], [cast_out_ref])
    y_attn = _dot(attn_ref[...], woa_ref[...])
    gate = jax.nn.sigmoid(_dot_t(a_ref[...], wg0_ref[...]) + bg0_ref[...])
    merged = (gate * y_attn + m_ref[...]).astype(BF16)
    h = x_ref[...] + _dot(merged, wout_ref[...])
    h_ref[...] = h
    f_ref[...] = _rms(h, fg_ref[...]).astype(BF16)


def _merge(attn, a, m_sgu, x, w_o_attn, w_g0, b_g0, w_out, ffn_g, cast_w, tm):
    T, D = x.shape
    row = lambda i: (i, 0)
    tile = pl.BlockSpec((tm, D), row)
    cast_in, cast_out = _cast_block_specs(*cast_w.shape, 0, T // tm, lambda i: i)
    return pl.pallas_call(
        _merge_kernel,
        grid=(T // tm,),
        in_specs=[tile, tile, tile, tile,
                  _resident(w_o_attn.shape), _resident(w_g0.shape), _resident(b_g0.shape),
                  _resident(w_out.shape), _resident(ffn_g.shape), cast_in],
        out_specs=[tile, tile, cast_out],
        out_shape=[jax.ShapeDtypeStruct((T, D), F32), jax.ShapeDtypeStruct((T, D), BF16),
                   jax.ShapeDtypeStruct(cast_w.shape, BF16)],
        compiler_params=_params(1),
        name="merge_outproj",
    )(attn, a, m_sgu, x, w_o_attn, w_g0, b_g0, w_out, ffn_g, cast_w)


def _ffn_kernel(f_ref, h_ref, wg_ref, wu_ref, wd_ref, ng_ref, o_ref, *, sub):
    j = pl.program_id(1)

    @pl.when(j == 0)
    def _():
        o_ref[...] = h_ref[...]

    f = f_ref[...]
    starts = list(range(0, wg_ref.shape[1], sub))
    projected = [(_dot(f, wg_ref[:, c0:c0 + sub]), _dot(f, wu_ref[:, c0:c0 + sub]))
                 for c0 in starts]
    part = None
    for c0, (gate, up) in zip(starts, projected):
        act = (jax.nn.silu(gate) * up).astype(BF16)
        down = _dot(act, wd_ref[c0:c0 + sub, :])
        part = down if part is None else part + down
    o_ref[...] += part

    @pl.when(j == pl.num_programs(1) - 1)
    def _():
        o_ref[...] = _rms(o_ref[...], ng_ref[...])


def _ffn(f, h, w_gate, w_up, w_down, final_g, tm, tf, sub):
    T, D = h.shape
    d_ff = w_gate.shape[1]
    row = lambda i, j: (i, 0)
    return pl.pallas_call(
        functools.partial(_ffn_kernel, sub=sub),
        grid=(T // tm, d_ff // tf),
        in_specs=[
            pl.BlockSpec((tm, D), row),
            pl.BlockSpec((tm, D), row),
            pl.BlockSpec((D, tf), lambda i, j: (0, j)),
            pl.BlockSpec((D, tf), lambda i, j: (0, j)),
            pl.BlockSpec((tf, D), lambda i, j: (j, 0)),
            _resident((1, D)),
        ],
        out_specs=pl.BlockSpec((tm, D), row),
        out_shape=jax.ShapeDtypeStruct((T, D), F32),
        compiler_params=_params(2),
        name="swiglu_ffn",
    )(f, h, w_gate, w_up, w_down, final_g)


def kernel(x, positions, norm_mix_g, w_in, b_gate, q_norm_g, w_uq, kv_norm_g, w_ukv, w_o_attn,
           sgu_norm_g, w_sgu, b_sgu, w_o_sgu, w_out, norm_ffn_g, w_gate_ffn, w_up_ffn,
           w_down_ffn, norm_final_g):
    B, S, D = x.shape
    T = B * S
    depth = w_in.shape[0]
    assert depth == 1, "the final norm is fused into the FFN epilogue of a single layer"
    assert w_in.shape[1:] == (D, D_IN)

    cs = _rope_tables(positions).reshape(B, S, LANES)
    row_vec = lambda v: v.reshape(1, -1).astype(F32)

    h = x
    out = None
    for l in range(depth):
        w_in_t = jnp.swapaxes(w_in[l], 0, 1)
        w_lat_t, w_uq_p, w_ukv_p = _prep_inproj_weights(w_in_t, w_uq[l], w_ukv[l])
        b_full = jnp.repeat(b_sgu[l].T, SGU_GROUP_DIM, axis=1).astype(F32)

        a, q, k_nope, v, k_pe = _inproj(
            h, row_vec(norm_mix_g[l]), cs, w_lat_t, row_vec(q_norm_g[l]), row_vec(kv_norm_g[l]),
            w_uq_p, w_ukv_p, tm=512, sub=256)
        a2 = a.reshape(T, D)
        whole = lambda w: (w, 0, w.shape[0])
        attn, (w_uv_t, w_g0_t, w_g1_t, w_os, w_oa, w_o, w_uf) = _attention(
            q, k_nope, k_pe, v,
            [(w_in_t, UV_OFF, 2 * SGU_WIDTH), (w_in_t, GATE_OFF, D), (w_in_t, GATE_OFF + D, D),
             whole(w_o_sgu[l]), whole(w_o_attn[l]), whole(w_out[l]), whole(w_up_ffn[l])],
            tq=256, hb=2)
        m_sgu, w_gf = _sgu_branch(
            a2, w_uv_t, row_vec(sgu_norm_g[l]), w_sgu[l], b_full, w_os, w_g1_t,
            row_vec(b_gate[l, D:]), w_gate_ffn[l], tm=512, sub=256)
        h_mid, f, w_df = _merge(
            attn.reshape(T, D), a2, m_sgu, h.reshape(T, D), w_oa, w_g0_t,
            row_vec(b_gate[l, :D]), w_o, row_vec(norm_ffn_g[l]), w_down_ffn[l], tm=256)
        out = _ffn(f, h_mid, w_gf, w_uf, w_df, row_vec(norm_final_g), tm=1024, tf=512,
                   sub=256)
        h = out.reshape(B, S, D)
    return h
```

```python
import functools

import jax
import jax.numpy as jnp
from jax import lax
from jax.experimental import pallas as pl
from jax.experimental.pallas import tpu as pltpu

D_MODEL = 2048
N_HEADS = 16
QK_NOPE_DIM = 128
QK_ROPE_DIM = 64
V_HEAD_DIM = 128
Q_LORA_RANK = 512
KV_LORA_RANK = 512
ROPE_THETA = 10000.0
SGU_GROUPS = 8
SGU_GROUP_DIM = 128
SGU_WIDTH = SGU_GROUPS * SGU_GROUP_DIM
CHUNK = 128
N_BRANCH = 2
RMS_EPS = 1e-6
KPE_OFF = Q_LORA_RANK + KV_LORA_RANK
UV_OFF = KPE_OFF + QK_ROPE_DIM
GATE_OFF = UV_OFF + 2 * SGU_WIDTH
D_IN = GATE_OFF + N_BRANCH * D_MODEL
LANES = 128
HALF_ROPE = QK_ROPE_DIM // 2
LOG2_E = 1.4426950408889634
QK_LOG2_SCALE = (QK_NOPE_DIM + QK_ROPE_DIM) ** -0.5 * LOG2_E
BF16_SUBLANES = 16

VMEM_LIMIT_BYTES = 60 * 1024 * 1024

INPROJ_ROWS = 512
SGU_ROWS = 512
MERGE_ROWS = 256
ROW_SUB = 256
ATTN_Q_ROWS = 256
ATTN_HEADS_PER_STEP = 2
FFN_ROWS = 1024
FFN_COLS = 512
FFN_COL_SUB = 256

F32 = jnp.float32
BF16 = jnp.bfloat16


def _rms(x, g):
    return x * lax.rsqrt(jnp.mean(x * x, axis=-1, keepdims=True) + RMS_EPS) * g


def _dot(a, b):
    return jnp.dot(a, b, preferred_element_type=F32)


def _dot_t(a, b_t):
    return lax.dot_general(a, b_t, (((1,), (1,)), ((), ())), preferred_element_type=F32)


def _resident(shape):
    return pl.BlockSpec(shape, lambda *_: (0,) * len(shape), pipeline_mode=pl.Buffered(1))


def _params(n_axes):
    return pltpu.CompilerParams(
        dimension_semantics=("arbitrary",) * n_axes, vmem_limit_bytes=VMEM_LIMIT_BYTES)


def _cast_block_specs(n_rows, n_cols, row0, n_steps, linear_step):
    share = 1 if (n_rows // n_steps) % BF16_SUBLANES == 0 else 2
    blk = n_rows * share // n_steps
    assert blk * n_steps == n_rows * share and blk % BF16_SUBLANES == 0 and row0 % blk == 0
    first = row0 // blk
    return (pl.BlockSpec((blk, n_cols), lambda *idx: (first + linear_step(*idx) // share, 0)),
            pl.BlockSpec((blk, n_cols), lambda *idx: (linear_step(*idx) // share, 0)))


def _cast_rows(srcs, dsts):
    for src, dst in zip(srcs, dsts):
        dst[...] = src[...].astype(dst.dtype)


def _rope_table_kernel(pos_ref, freq_ref, cos_ref, sin_ref):
    ang = pos_ref[...] * freq_ref[...]
    cos_ref[...] = jnp.cos(ang)
    sin_ref[...] = jnp.sin(ang)


def _rope_tables(positions):
    n_tok = positions.size
    per_row = LANES // HALF_ROPE
    inv_freq = ROPE_THETA ** (-jnp.arange(0, QK_ROPE_DIM, 2, dtype=F32) / QK_ROPE_DIM)
    pos_rep = jnp.repeat(positions.astype(F32).reshape(n_tok // per_row, per_row), HALF_ROPE, axis=1)
    freq = jnp.tile(inv_freq, per_row).reshape(1, LANES)
    shape = jax.ShapeDtypeStruct((n_tok // per_row, LANES), F32)
    cos, sin = pl.pallas_call(
        _rope_table_kernel, out_shape=(shape, shape), name="rope_tables")(pos_rep, freq)
    cos = cos.reshape(n_tok, HALF_ROPE)
    sin = sin.reshape(n_tok, HALF_ROPE)
    return jnp.concatenate([cos, cos, -sin, sin], axis=-1)


def _rope_dup(x, cs):
    y = x * cs
    return y + pltpu.roll(y, LANES // 2, 1)


def _dup_rope_lanes(src):
    lane = lax.broadcasted_iota(jnp.int32, src.shape, 1)
    return jnp.where(lane < 2 * HALF_ROPE, src,
                     jnp.where(lane < 3 * HALF_ROPE, pltpu.roll(src, HALF_ROPE, 1),
                               pltpu.roll(src, 3 * HALF_ROPE, 1)))


def _prep_kernel(wlat_ref, wuq_ref, wukv_ref, olat_ref, ouq_ref, oukv_ref):
    olat_ref[0:KPE_OFF] = wlat_ref[0:KPE_OFF].astype(BF16)
    x1 = wlat_ref[KPE_OFF:KPE_OFF + HALF_ROPE].astype(BF16)
    x2 = wlat_ref[KPE_OFF + HALF_ROPE:UV_OFF].astype(BF16)
    for k, part in enumerate((x1, x2, x2, x1)):
        olat_ref[KPE_OFF + k * HALF_ROPE:KPE_OFF + (k + 1) * HALF_ROPE] = part
    oukv_ref[...] = wukv_ref[...].astype(BF16)
    half = LANES // 2
    nope_cols = N_HEADS * QK_NOPE_DIM
    lane = lax.broadcasted_iota(jnp.int32, (wuq_ref.shape[0], LANES), 1)
    for pair in range(N_HEADS // 2):
        t0, t1, t2 = (wuq_ref[:, (3 * pair + k) * LANES:(3 * pair + k + 1) * LANES] for k in range(3))
        r1 = pltpu.roll(t1, half, 1)
        r2 = pltpu.roll(t2, half, 1)
        heads = ((2 * pair, t0, t1), (2 * pair + 1, jnp.where(lane < half, r1, r2), r2))
        for h, nope, rope_src in heads:
            ouq_ref[:, h * LANES:(h + 1) * LANES] = nope.astype(BF16)
            ouq_ref[:, nope_cols + h * LANES:nope_cols + (h + 1) * LANES] = (
                _dup_rope_lanes(rope_src).astype(BF16))


def _prep_inproj_weights(w_in_t, w_uq, w_ukv, n_chunks=4):
    d = w_in_t.shape[1]
    r_q, r_kv = w_uq.shape[0], w_ukv.shape[0]
    lat_rows = UV_OFF + LANES - QK_ROPE_DIM
    uq_cols = N_HEADS * (QK_NOPE_DIM + LANES)
    return pl.pallas_call(
        _prep_kernel,
        grid=(n_chunks,),
        in_specs=[
            pl.BlockSpec((UV_OFF, d // n_chunks), lambda c: (0, c)),
            pl.BlockSpec((r_q // n_chunks, w_uq.shape[1]), lambda c: (c, 0)),
            pl.BlockSpec((r_kv // n_chunks, w_ukv.shape[1]), lambda c: (c, 0)),
        ],
        out_specs=[
            pl.BlockSpec((lat_rows, d // n_chunks), lambda c: (0, c)),
            pl.BlockSpec((r_q // n_chunks, uq_cols), lambda c: (c, 0)),
            pl.BlockSpec((r_kv // n_chunks, w_ukv.shape[1]), lambda c: (c, 0)),
        ],
        out_shape=[
            jax.ShapeDtypeStruct((lat_rows, d), BF16),
            jax.ShapeDtypeStruct((r_q, uq_cols), BF16),
            jax.ShapeDtypeStruct(w_ukv.shape, BF16),
        ],
        compiler_params=_params(1),
        name="prep_inproj_weights",
    )(w_in_t, w_uq, w_ukv)


def _inproj_kernel(x_ref, g_ref, cs_ref, wlat_ref, qg_ref, kvg_ref, wuq_ref, wukv_ref,
                   a_ref, q_ref, kn_ref, v_ref, kpe_ref, *, sub):
    def latents(r0):
        a = _rms(x_ref[0, r0:r0 + sub, :], g_ref[...]).astype(BF16)
        a_ref[0, r0:r0 + sub, :] = a
        return _dot_t(a, wlat_ref[...])

    starts = list(range(0, x_ref.shape[1], sub))
    for r0, z in zip(starts, [latents(r0) for r0 in starts]):
        rows = slice(r0, r0 + sub)
        qn = (_rms(z[:, :Q_LORA_RANK], qg_ref[...]) * QK_LOG2_SCALE).astype(BF16)
        kvn = _rms(z[:, Q_LORA_RANK:Q_LORA_RANK + KV_LORA_RANK], kvg_ref[...]).astype(BF16)
        cs = cs_ref[0, rows, :]
        kpe = _rope_dup(z[:, Q_LORA_RANK + KV_LORA_RANK:], cs)
        lane = lax.broadcasted_iota(jnp.int32, kpe.shape, 1)
        kpe_ref[0, rows, :] = jnp.where(lane < QK_ROPE_DIM, kpe, 0.0).astype(BF16)

        heads_per_dot = 4
        width = heads_per_dot * LANES
        nope_cols = N_HEADS * QK_NOPE_DIM
        for hg in range(N_HEADS // heads_per_dot):
            c0 = hg * width
            q_nope = _dot(qn, wuq_ref[:, c0:c0 + width])
            q_pe = _dot(qn, wuq_ref[:, nope_cols + c0:nope_cols + c0 + width])
            kv0 = 2 * c0
            kv_a = _dot(kvn, wukv_ref[:, kv0:kv0 + width])
            kv_b = _dot(kvn, wukv_ref[:, kv0 + width:kv0 + 2 * width])
            for hh in range(heads_per_dot):
                h = hg * heads_per_dot + hh
                sl = slice(hh * LANES, (hh + 1) * LANES)
                q_ref[0, h, rows, 0:LANES] = q_nope[:, sl].astype(BF16)
                q_ref[0, h, rows, LANES:2 * LANES] = _rope_dup(q_pe[:, sl], cs).astype(BF16)
                kv = kv_a if hh < heads_per_dot // 2 else kv_b
                k0 = (hh % (heads_per_dot // 2)) * 2 * LANES
                kn_ref[0, h, rows, :] = kv[:, k0:k0 + LANES].astype(BF16)
                v_ref[0, h, rows, :] = kv[:, k0 + LANES:k0 + 2 * LANES].astype(BF16)


def _inproj(x, norm_g, cs, w_lat_t, q_g, kv_g, w_uq, w_ukv, tm, sub):
    B, S, D = x.shape
    row = lambda b, i: (b, i, 0)
    head = lambda b, i: (b, 0, i, 0)
    return pl.pallas_call(
        functools.partial(_inproj_kernel, sub=sub),
        grid=(B, S // tm),
        in_specs=[
            pl.BlockSpec((1, tm, D), row),
            _resident((1, D)),
            pl.BlockSpec((1, tm, LANES), row),
            _resident(w_lat_t.shape),
            _resident((1, Q_LORA_RANK)),
            _resident((1, KV_LORA_RANK)),
            _resident(w_uq.shape),
            _resident(w_ukv.shape),
        ],
        out_specs=[
            pl.BlockSpec((1, tm, D), row),
            pl.BlockSpec((1, N_HEADS, tm, 2 * LANES), head),
            pl.BlockSpec((1, N_HEADS, tm, LANES), head),
            pl.BlockSpec((1, N_HEADS, tm, LANES), head),
            pl.BlockSpec((1, tm, LANES), row),
        ],
        out_shape=[
            jax.ShapeDtypeStruct((B, S, D), BF16),
            jax.ShapeDtypeStruct((B, N_HEADS, S, 2 * LANES), BF16),
            jax.ShapeDtypeStruct((B, N_HEADS, S, LANES), BF16),
            jax.ShapeDtypeStruct((B, N_HEADS, S, LANES), BF16),
            jax.ShapeDtypeStruct((B, S, LANES), BF16),
        ],
        compiler_params=_params(2),
        name="inproj",
    )(x, norm_g, cs, w_lat_t, q_g, kv_g, w_uq, w_ukv)


def _sgu_kernel(a_ref, wuv_ref, sg_ref, ws_ref, bfull_ref, wos_ref, wg1_ref, bg1_ref, cast_ref,
                m_ref, cast_out_ref, *, sub):
    _cast_rows([cast_ref], [cast_out_ref])
    n_chunks = sub // CHUNK
    t_idx = lax.broadcasted_iota(jnp.int32, (CHUNK, CHUNK), 0)
    s_idx = lax.broadcasted_iota(jnp.int32, (CHUNK, CHUNK), 1)
    causal = t_idx >= s_idx
    ws = [jnp.where(causal, ws_ref[g], 0.0).astype(BF16) for g in range(SGU_GROUPS)]
    bfull = bfull_ref[...]
    def gating_unit(uv_raw):
        uv = jax.nn.gelu(uv_raw)
        u = uv[:, :SGU_WIDTH]
        vn = _rms(uv[:, SGU_WIDTH:], sg_ref[...]).astype(BF16)
        mixed_cols = []
        for g in range(SGU_GROUPS):
            gs = slice(g * SGU_GROUP_DIM, (g + 1) * SGU_GROUP_DIM)
            rhs = jnp.concatenate(
                [vn[c * CHUNK:(c + 1) * CHUNK, gs] for c in range(n_chunks)], axis=1)
            mixed_cols.append(_dot(ws[g], rhs))
        rows = []
        for c in range(n_chunks):
            cs = slice(c * SGU_GROUP_DIM, (c + 1) * SGU_GROUP_DIM)
            mixed = jnp.concatenate([mixed_cols[g][:, cs] for g in range(SGU_GROUPS)], axis=1)
            rows.append(u[c * CHUNK:(c + 1) * CHUNK] * (mixed + bfull))
        return jnp.concatenate(rows, axis=0).astype(BF16)

    a = a_ref[...]
    uv_raw = _dot_t(a, wuv_ref[...])
    gate_raw = _dot_t(a, wg1_ref[...])
    for r0 in range(0, a_ref.shape[0], sub):
        rows = slice(r0, r0 + sub)
        y_sgu = _dot(gating_unit(uv_raw[rows]), wos_ref[...])
        m_ref[rows, :] = jax.nn.sigmoid(gate_raw[rows] + bg1_ref[...]) * y_sgu


def _sgu_branch(a, w_uv, sgu_g, w_s, b_full, w_o_sgu, w_g1, b_g1, cast_w, tm, sub):
    T, D = a.shape
    row = lambda i: (i, 0)
    cast_in, cast_out = _cast_block_specs(*cast_w.shape, 0, T // tm, lambda i: i)
    return pl.pallas_call(
        functools.partial(_sgu_kernel, sub=sub),
        grid=(T // tm,),
        in_specs=[
            pl.BlockSpec((tm, D), row),
            _resident(w_uv.shape),
            _resident(sgu_g.shape),
            _resident(w_s.shape),
            _resident(b_full.shape),
            _resident(w_o_sgu.shape),
            _resident(w_g1.shape),
            _resident(b_g1.shape),
            cast_in,
        ],
        out_specs=[pl.BlockSpec((tm, D), row), cast_out],
        out_shape=[jax.ShapeDtypeStruct((T, D), F32), jax.ShapeDtypeStruct(cast_w.shape, BF16)],
        compiler_params=_params(1),
        name="sgu_branch",
    )(a, w_uv, sgu_g, w_s, b_full, w_o_sgu, w_g1, b_g1, cast_w)


def _attn_kernel(q_ref, kn_ref, kpe_ref, v_ref, *rest, tq, n_cast):
    cast_in, (o_ref,), cast_out, (kf_ref, vf_ref) = (
        rest[:n_cast], rest[n_cast:n_cast + 1], rest[n_cast + 1:2 * n_cast + 1], rest[2 * n_cast + 1:])
    _cast_rows(cast_in, cast_out)
    seq = q_ref.shape[2]
    row = lax.broadcasted_iota(jnp.int32, (tq, tq), 0)
    col = lax.broadcasted_iota(jnp.int32, (tq, tq), 1)
    causal = row >= col
    neg = jnp.finfo(F32).min
    heads = q_ref.shape[1]
    for hh in range(heads):
        kf_ref[hh, :, 0:LANES] = kn_ref[0, hh]
        kf_ref[hh, :, LANES:2 * LANES] = kpe_ref[0]
        vf_ref[hh, :, 0:LANES] = v_ref[0, hh]
        vf_ref[hh, :, LANES:2 * LANES] = jnp.ones((seq, LANES), vf_ref.dtype)

    nt = (((1,), (1,)), ((), ()))
    def scores(pair, hh):
        q0 = pair * 2 * tq
        k1, k2 = q0 + tq, q0 + 2 * tq
        q = q_ref[0, hh, q0:k2, :]
        return (lax.dot_general(q, kf_ref[hh, 0:k1, :], nt, preferred_element_type=F32),
                lax.dot_general(q[tq:], kf_ref[hh, k1:k2, :], nt, preferred_element_type=F32))

    def finish(pair, hh, s_main, s_ext):
        q0 = pair * 2 * tq
        k1, k2 = q0 + tq, q0 + 2 * tq
        top = s_main[0:tq]
        top_diag = jnp.where(causal, top[:, q0:k1], neg)
        top = jnp.concatenate([top[:, 0:q0], top_diag], axis=1) if pair else top_diag
        bot = s_main[tq:]
        ext = jnp.where(causal, s_ext, neg)
        m_top = jnp.max(top, axis=-1, keepdims=True)
        m_bot = jnp.maximum(jnp.max(bot, axis=-1, keepdims=True),
                            jnp.max(ext, axis=-1, keepdims=True))
        p_main = jnp.concatenate([jnp.exp2(top - m_top), jnp.exp2(bot - m_bot)], axis=0)
        acc = _dot(p_main.astype(BF16), vf_ref[hh, 0:k1, :])
        acc_bot = acc[tq:] + _dot(jnp.exp2(ext - m_bot).astype(BF16), vf_ref[hh, k1:k2, :])
        lanes = slice(hh * LANES, (hh + 1) * LANES)
        o_ref[0, q0:k1, lanes] = (acc[0:tq, 0:LANES] / acc[0:tq, LANES:]).astype(o_ref.dtype)
        o_ref[0, k1:k2, lanes] = (acc_bot[:, 0:LANES] / acc_bot[:, LANES:]).astype(o_ref.dtype)

    work = [(pair, hh) for pair in reversed(range(seq // (2 * tq))) for hh in range(heads)]
    lead = 2
    pending = [scores(*w) for w in work[:lead]]
    for idx, (pair, hh) in enumerate(work):
        if idx + lead < len(work):
            pending.append(scores(*work[idx + lead]))
        finish(pair, hh, *pending.pop(0))


def _attention(q, k_nope, k_pe, v, casts, tq, hb):
    B, H, S, _ = q.shape
    groups = H // hb
    specs = [_cast_block_specs(n, w.shape[1], r0, B * groups, lambda b, g: b * groups + g)
             for w, r0, n in casts]
    outs = pl.pallas_call(
        functools.partial(_attn_kernel, tq=tq, n_cast=len(casts)),
        grid=(B, groups),
        in_specs=[
            pl.BlockSpec((1, hb, S, 2 * LANES), lambda b, g: (b, g, 0, 0)),
            pl.BlockSpec((1, hb, S, LANES), lambda b, g: (b, g, 0, 0)),
            pl.BlockSpec((1, S, LANES), lambda b, g: (b, 0, 0)),
            pl.BlockSpec((1, hb, S, LANES), lambda b, g: (b, g, 0, 0)),
        ] + [s_in for s_in, _ in specs],
        out_specs=[pl.BlockSpec((1, S, hb * LANES), lambda b, g: (b, 0, g))]
        + [s_out for _, s_out in specs],
        out_shape=[jax.ShapeDtypeStruct((B, S, H * V_HEAD_DIM), BF16)]
        + [jax.ShapeDtypeStruct((n, w.shape[1]), BF16) for w, _, n in casts],
        scratch_shapes=[pltpu.VMEM((hb, S, 2 * LANES), BF16), pltpu.VMEM((hb, S, 2 * LANES), BF16)],
        compiler_params=_params(2),
        name="mla_attention",
    )(q, k_nope, k_pe, v, *[w for w, _, _ in casts])
    return outs[0], outs[1:]


def _merge_kernel(attn_ref, a_ref, m_ref, x_ref, woa_ref, wg0_ref, bg0_ref, wout_ref, fg_ref,
                  cast_ref, h_ref, f_ref, cast_out_ref):
    _cast_rows([cast_ref], [cast_out_ref])
    y_attn = _dot(attn_ref[...], woa_ref[...])
    gate = jax.nn.sigmoid(_dot_t(a_ref[...], wg0_ref[...]) + bg0_ref[...])
    merged = (gate * y_attn + m_ref[...]).astype(BF16)
    h = x_ref[...] + _dot(merged, wout_ref[...])
    h_ref[...] = h
    f_ref[...] = _rms(h, fg_ref[...]).astype(BF16)


def _merge(attn, a, m_sgu, x, w_o_attn, w_g0, b_g0, w_out, ffn_g, cast_w, tm):
    T, D = x.shape
    row = lambda i: (i, 0)
    tile = pl.BlockSpec((tm, D), row)
    cast_in, cast_out = _cast_block_specs(*cast_w.shape, 0, T // tm, lambda i: i)
    return pl.pallas_call(
        _merge_kernel,
        grid=(T // tm,),
        in_specs=[tile, tile, tile, tile,
                  _resident(w_o_attn.shape), _resident(w_g0.shape), _resident(b_g0.shape),
                  _resident(w_out.shape), _resident(ffn_g.shape), cast_in],
        out_specs=[tile, tile, cast_out],
        out_shape=[jax.ShapeDtypeStruct((T, D), F32), jax.ShapeDtypeStruct((T, D), BF16),
                   jax.ShapeDtypeStruct(cast_w.shape, BF16)],
        compiler_params=_params(1),
        name="merge_outproj",
    )(attn, a, m_sgu, x, w_o_attn, w_g0, b_g0, w_out, ffn_g, cast_w)


def _ffn_kernel(f_ref, h_ref, wg_ref, wu_ref, wd_ref, ng_ref, o_ref, *, sub):
    j = pl.program_id(1)

    f = f_ref[...]
    starts = list(range(0, wg_ref.shape[1], sub))
    projected = [(_dot(f, wg_ref[:, c0:c0 + sub]), _dot(f, wu_ref[:, c0:c0 + sub]))
                 for c0 in starts]
    act = jnp.concatenate(
        [(jax.nn.silu(gate) * up).astype(BF16) for gate, up in projected], axis=1)
    o_ref[...] = jnp.where(j == 0, h_ref[...], o_ref[...]) + _dot(act, wd_ref[...])

    @pl.when(j == pl.num_programs(1) - 1)
    def _():
        o_ref[...] = _rms(o_ref[...], ng_ref[...])


def _ffn(f, h, w_gate, w_up, w_down, final_g, tm, tf, sub):
    T, D = h.shape
    d_ff = w_gate.shape[1]
    row = lambda i, j: (i, 0)
    return pl.pallas_call(
        functools.partial(_ffn_kernel, sub=sub),
        grid=(T // tm, d_ff // tf),
        in_specs=[
            pl.BlockSpec((tm, D), row),
            pl.BlockSpec((tm, D), row),
            pl.BlockSpec((D, tf), lambda i, j: (0, j)),
            pl.BlockSpec((D, tf), lambda i, j: (0, j)),
            pl.BlockSpec((tf, D), lambda i, j: (j, 0)),
            _resident((1, D)),
        ],
        out_specs=pl.BlockSpec((tm, D), row),
        out_shape=jax.ShapeDtypeStruct((T, D), F32),
        compiler_params=_params(2),
        name="swiglu_ffn",
    )(f, h, w_gate, w_up, w_down, final_g)


def kernel(x, positions, norm_mix_g, w_in, b_gate, q_norm_g, w_uq, kv_norm_g, w_ukv, w_o_attn,
           sgu_norm_g, w_sgu, b_sgu, w_o_sgu, w_out, norm_ffn_g, w_gate_ffn, w_up_ffn,
           w_down_ffn, norm_final_g):
    B, S, D = x.shape
    T = B * S
    depth = w_in.shape[0]
    assert depth == 1, "the final norm is fused into the FFN epilogue of a single layer"
    assert w_in.shape[1:] == (D, D_IN)

    cs = _rope_tables(positions).reshape(B, S, LANES)
    row_vec = lambda v: v.reshape(1, -1).astype(F32)

    h = x
    out = None
    for l in range(depth):
        w_in_t = jnp.swapaxes(w_in[l], 0, 1)
        w_lat_t, w_uq_p, w_ukv_p = _prep_inproj_weights(w_in_t, w_uq[l], w_ukv[l])
        b_full = jnp.repeat(b_sgu[l].T, SGU_GROUP_DIM, axis=1).astype(F32)

        a, q, k_nope, v, k_pe = _inproj(
            h, row_vec(norm_mix_g[l]), cs, w_lat_t, row_vec(q_norm_g[l]), row_vec(kv_norm_g[l]),
            w_uq_p, w_ukv_p, tm=INPROJ_ROWS, sub=ROW_SUB)
        a2 = a.reshape(T, D)
        whole = lambda w: (w, 0, w.shape[0])
        attn, (w_uv_t, w_g0_t, w_g1_t, w_os, w_oa, w_o, w_uf) = _attention(
            q, k_nope, k_pe, v,
            [(w_in_t, UV_OFF, 2 * SGU_WIDTH), (w_in_t, GATE_OFF, D), (w_in_t, GATE_OFF + D, D),
             whole(w_o_sgu[l]), whole(w_o_attn[l]), whole(w_out[l]), whole(w_up_ffn[l])],
            tq=ATTN_Q_ROWS, hb=ATTN_HEADS_PER_STEP)
        m_sgu, w_gf = _sgu_branch(
            a2, w_uv_t, row_vec(sgu_norm_g[l]), w_sgu[l], b_full, w_os, w_g1_t,
            row_vec(b_gate[l, D:]), w_gate_ffn[l], tm=SGU_ROWS, sub=ROW_SUB)
        h_mid, f, w_df = _merge(
            attn.reshape(T, D), a2, m_sgu, h.reshape(T, D), w_oa, w_g0_t,
            row_vec(b_gate[l, :D]), w_o, row_vec(norm_ffn_g[l]), w_down_ffn[l], tm=MERGE_ROWS)
        out = _ffn(f, h_mid, w_gf, w_uf, w_df, row_vec(norm_final_g), tm=FFN_ROWS,
                   tf=FFN_COLS, sub=FFN_COL_SUB)
        h = out.reshape(B, S, D)
    return h
```

```python
import functools

import jax
import jax.numpy as jnp
from jax import lax
from jax.experimental import pallas as pl
from jax.experimental.pallas import tpu as pltpu

D_MODEL = 2048
N_HEADS = 16
QK_NOPE_DIM = 128
QK_ROPE_DIM = 64
V_HEAD_DIM = 128
Q_LORA_RANK = 512
KV_LORA_RANK = 512
ROPE_THETA = 10000.0
SGU_GROUPS = 8
SGU_GROUP_DIM = 128
SGU_WIDTH = SGU_GROUPS * SGU_GROUP_DIM
CHUNK = 128
N_BRANCH = 2
RMS_EPS = 1e-6
KPE_OFF = Q_LORA_RANK + KV_LORA_RANK
UV_OFF = KPE_OFF + QK_ROPE_DIM
GATE_OFF = UV_OFF + 2 * SGU_WIDTH
D_IN = GATE_OFF + N_BRANCH * D_MODEL
LANES = 128
HALF_ROPE = QK_ROPE_DIM // 2
LOG2_E = 1.4426950408889634
QK_LOG2_SCALE = (QK_NOPE_DIM + QK_ROPE_DIM) ** -0.5 * LOG2_E
BF16_SUBLANES = 16

VMEM_LIMIT_BYTES = 60 * 1024 * 1024

INPROJ_ROWS = 512
SGU_ROWS = 512
MERGE_ROWS = 256
ROW_SUB = 256
ATTN_Q_ROWS = 256
ATTN_HEADS_PER_STEP = 2
FFN_ROWS = 1024
FFN_COLS = 512
FFN_COL_SUB = 256

F32 = jnp.float32
BF16 = jnp.bfloat16


def _rms(x, g):
    return x * lax.rsqrt(jnp.mean(x * x, axis=-1, keepdims=True) + RMS_EPS) * g


def _dot(a, b):
    return jnp.dot(a, b, preferred_element_type=F32)


def _dot_t(a, b_t):
    return lax.dot_general(a, b_t, (((1,), (1,)), ((), ())), preferred_element_type=F32)


def _resident(shape):
    return pl.BlockSpec(shape, lambda *_: (0,) * len(shape), pipeline_mode=pl.Buffered(1))


def _params(n_axes):
    return pltpu.CompilerParams(
        dimension_semantics=("arbitrary",) * n_axes, vmem_limit_bytes=VMEM_LIMIT_BYTES)


def _cast_block_specs(n_rows, n_cols, row0, n_steps, linear_step):
    share = 1 if (n_rows // n_steps) % BF16_SUBLANES == 0 else 2
    blk = n_rows * share // n_steps
    assert blk * n_steps == n_rows * share and blk % BF16_SUBLANES == 0 and row0 % blk == 0
    first = row0 // blk
    return (pl.BlockSpec((blk, n_cols), lambda *idx: (first + linear_step(*idx) // share, 0)),
            pl.BlockSpec((blk, n_cols), lambda *idx: (linear_step(*idx) // share, 0)))


def _cast_rows(srcs, dsts):
    for src, dst in zip(srcs, dsts):
        dst[...] = src[...].astype(dst.dtype)


def _rope_table_kernel(pos_ref, freq_ref, cs_ref):
    ang = pos_ref[...] * freq_ref[...]
    cos, sin = jnp.cos(ang), jnp.sin(ang)
    n_rows = ang.shape[0]
    per_row = LANES // HALF_ROPE
    lane = lax.broadcasted_iota(jnp.int32, ang.shape, 1)

    def lanes_from(x, src, dst):
        shift = (dst - src) % LANES
        return pltpu.roll(x, shift, 1) if shift else x

    for k in range(per_row):
        src = k * HALF_ROPE
        row = jnp.where(
            lane < HALF_ROPE, lanes_from(cos, src, 0),
            jnp.where(lane < 2 * HALF_ROPE, lanes_from(cos, src, HALF_ROPE),
                      jnp.where(lane < 3 * HALF_ROPE, -lanes_from(sin, src, 2 * HALF_ROPE),
                                lanes_from(sin, src, 3 * HALF_ROPE))))
        cs_ref[pl.ds(k, n_rows, stride=per_row), :] = row


def _rope_tables(positions):
    n_tok = positions.size
    per_row = LANES // HALF_ROPE
    inv_freq = ROPE_THETA ** (-jnp.arange(0, QK_ROPE_DIM, 2, dtype=F32) / QK_ROPE_DIM)
    pos_rep = jnp.repeat(positions.astype(F32).reshape(n_tok // per_row, per_row), HALF_ROPE, axis=1)
    freq = jnp.tile(inv_freq, per_row).reshape(1, LANES)
    return pl.pallas_call(
        _rope_table_kernel, out_shape=jax.ShapeDtypeStruct((n_tok, LANES), F32),
        name="rope_tables")(pos_rep, freq)


def _rope_dup(x, cs):
    y = x * cs
    return y + pltpu.roll(y, LANES // 2, 1)


def _dup_rope_lanes(src):
    lane = lax.broadcasted_iota(jnp.int32, src.shape, 1)
    return jnp.where(lane < 2 * HALF_ROPE, src,
                     jnp.where(lane < 3 * HALF_ROPE, pltpu.roll(src, HALF_ROPE, 1),
                               pltpu.roll(src, 3 * HALF_ROPE, 1)))


def _prep_kernel(wlat_ref, wuq_ref, wukv_ref, olat_ref, ouq_ref, oukv_ref):
    olat_ref[0:KPE_OFF] = wlat_ref[0:KPE_OFF].astype(BF16)
    x1 = wlat_ref[KPE_OFF:KPE_OFF + HALF_ROPE].astype(BF16)
    x2 = wlat_ref[KPE_OFF + HALF_ROPE:UV_OFF].astype(BF16)
    for k, part in enumerate((x1, x2, x2, x1)):
        olat_ref[KPE_OFF + k * HALF_ROPE:KPE_OFF + (k + 1) * HALF_ROPE] = part
    oukv_ref[...] = wukv_ref[...].astype(BF16)
    half = LANES // 2
    nope_cols = N_HEADS * QK_NOPE_DIM
    lane = lax.broadcasted_iota(jnp.int32, (wuq_ref.shape[0], LANES), 1)
    for pair in range(N_HEADS // 2):
        t0, t1, t2 = (wuq_ref[:, (3 * pair + k) * LANES:(3 * pair + k + 1) * LANES] for k in range(3))
        r1 = pltpu.roll(t1, half, 1)
        r2 = pltpu.roll(t2, half, 1)
        heads = ((2 * pair, t0, t1), (2 * pair + 1, jnp.where(lane < half, r1, r2), r2))
        for h, nope, rope_src in heads:
            ouq_ref[:, h * LANES:(h + 1) * LANES] = nope.astype(BF16)
            ouq_ref[:, nope_cols + h * LANES:nope_cols + (h + 1) * LANES] = (
                _dup_rope_lanes(rope_src).astype(BF16))


def _prep_inproj_weights(w_in_t, w_uq, w_ukv, n_chunks=4):
    d = w_in_t.shape[1]
    r_q, r_kv = w_uq.shape[0], w_ukv.shape[0]
    lat_rows = UV_OFF + LANES - QK_ROPE_DIM
    uq_cols = N_HEADS * (QK_NOPE_DIM + LANES)
    return pl.pallas_call(
        _prep_kernel,
        grid=(n_chunks,),
        in_specs=[
            pl.BlockSpec((UV_OFF, d // n_chunks), lambda c: (0, c)),
            pl.BlockSpec((r_q // n_chunks, w_uq.shape[1]), lambda c: (c, 0)),
            pl.BlockSpec((r_kv // n_chunks, w_ukv.shape[1]), lambda c: (c, 0)),
        ],
        out_specs=[
            pl.BlockSpec((lat_rows, d // n_chunks), lambda c: (0, c)),
            pl.BlockSpec((r_q // n_chunks, uq_cols), lambda c: (c, 0)),
            pl.BlockSpec((r_kv // n_chunks, w_ukv.shape[1]), lambda c: (c, 0)),
        ],
        out_shape=[
            jax.ShapeDtypeStruct((lat_rows, d), BF16),
            jax.ShapeDtypeStruct((r_q, uq_cols), BF16),
            jax.ShapeDtypeStruct(w_ukv.shape, BF16),
        ],
        compiler_params=_params(1),
        name="prep_inproj_weights",
    )(w_in_t, w_uq, w_ukv)


def _inproj_kernel(x_ref, g_ref, cs_ref, wlat_ref, qg_ref, kvg_ref, wuq_ref, wukv_ref,
                   a_ref, q_ref, kn_ref, v_ref, kpe_ref, *, sub):
    def latents(r0):
        a = _rms(x_ref[0, r0:r0 + sub, :], g_ref[...]).astype(BF16)
        a_ref[0, r0:r0 + sub, :] = a
        return _dot_t(a, wlat_ref[...])

    starts = list(range(0, x_ref.shape[1], sub))
    for r0, z in zip(starts, [latents(r0) for r0 in starts]):
        rows = slice(r0, r0 + sub)
        qn = (_rms(z[:, :Q_LORA_RANK], qg_ref[...]) * QK_LOG2_SCALE).astype(BF16)
        kvn = _rms(z[:, Q_LORA_RANK:Q_LORA_RANK + KV_LORA_RANK], kvg_ref[...]).astype(BF16)
        cs = cs_ref[0, rows, :]
        kpe = _rope_dup(z[:, Q_LORA_RANK + KV_LORA_RANK:], cs)
        lane = lax.broadcasted_iota(jnp.int32, kpe.shape, 1)
        kpe_ref[0, rows, :] = jnp.where(lane < QK_ROPE_DIM, kpe, 0.0).astype(BF16)

        heads_per_dot = 4
        width = heads_per_dot * LANES
        nope_cols = N_HEADS * QK_NOPE_DIM
        for hg in range(N_HEADS // heads_per_dot):
            c0 = hg * width
            q_nope = _dot(qn, wuq_ref[:, c0:c0 + width])
            q_pe = _dot(qn, wuq_ref[:, nope_cols + c0:nope_cols + c0 + width])
            kv0 = 2 * c0
            kv_a = _dot(kvn, wukv_ref[:, kv0:kv0 + width])
            kv_b = _dot(kvn, wukv_ref[:, kv0 + width:kv0 + 2 * width])
            for hh in range(heads_per_dot):
                h = hg * heads_per_dot + hh
                sl = slice(hh * LANES, (hh + 1) * LANES)
                q_ref[0, h, rows, 0:LANES] = q_nope[:, sl].astype(BF16)
                q_ref[0, h, rows, LANES:2 * LANES] = _rope_dup(q_pe[:, sl], cs).astype(BF16)
                kv = kv_a if hh < heads_per_dot // 2 else kv_b
                k0 = (hh % (heads_per_dot // 2)) * 2 * LANES
                kn_ref[0, h, rows, :] = kv[:, k0:k0 + LANES].astype(BF16)
                v_ref[0, h, rows, :] = kv[:, k0 + LANES:k0 + 2 * LANES].astype(BF16)


def _inproj(x, norm_g, cs, w_lat_t, q_g, kv_g, w_uq, w_ukv, tm, sub):
    B, S, D = x.shape
    row = lambda b, i: (b, i, 0)
    head = lambda b, i: (b, 0, i, 0)
    return pl.pallas_call(
        functools.partial(_inproj_kernel, sub=sub),
        grid=(B, S // tm),
        in_specs=[
            pl.BlockSpec((1, tm, D), row),
            _resident((1, D)),
            pl.BlockSpec((1, tm, LANES), row),
            _resident(w_lat_t.shape),
            _resident((1, Q_LORA_RANK)),
            _resident((1, KV_LORA_RANK)),
            _resident(w_uq.shape),
            _resident(w_ukv.shape),
        ],
        out_specs=[
            pl.BlockSpec((1, tm, D), row),
            pl.BlockSpec((1, N_HEADS, tm, 2 * LANES), head),
            pl.BlockSpec((1, N_HEADS, tm, LANES), head),
            pl.BlockSpec((1, N_HEADS, tm, LANES), head),
            pl.BlockSpec((1, tm, LANES), row),
        ],
        out_shape=[
            jax.ShapeDtypeStruct((B, S, D), BF16),
            jax.ShapeDtypeStruct((B, N_HEADS, S, 2 * LANES), BF16),
            jax.ShapeDtypeStruct((B, N_HEADS, S, LANES), BF16),
            jax.ShapeDtypeStruct((B, N_HEADS, S, LANES), BF16),
            jax.ShapeDtypeStruct((B, S, LANES), BF16),
        ],
        compiler_params=_params(2),
        name="inproj",
    )(x, norm_g, cs, w_lat_t, q_g, kv_g, w_uq, w_ukv)


def _sgu_kernel(a_ref, wuv_ref, sg_ref, ws_ref, bfull_ref, wos_ref, wg1_ref, bg1_ref, cast_ref,
                m_ref, cast_out_ref, *, sub):
    _cast_rows([cast_ref], [cast_out_ref])
    n_chunks = sub // CHUNK
    t_idx = lax.broadcasted_iota(jnp.int32, (CHUNK, CHUNK), 0)
    s_idx = lax.broadcasted_iota(jnp.int32, (CHUNK, CHUNK), 1)
    causal = t_idx >= s_idx
    ws = [jnp.where(causal, ws_ref[g], 0.0).astype(BF16) for g in range(SGU_GROUPS)]
    bfull = bfull_ref[...]
    def gating_unit(uv_raw):
        uv = jax.nn.gelu(uv_raw)
        u = uv[:, :SGU_WIDTH]
        vn = _rms(uv[:, SGU_WIDTH:], sg_ref[...]).astype(BF16)
        mixed_cols = []
        for g in range(SGU_GROUPS):
            gs = slice(g * SGU_GROUP_DIM, (g + 1) * SGU_GROUP_DIM)
            rhs = jnp.concatenate(
                [vn[c * CHUNK:(c + 1) * CHUNK, gs] for c in range(n_chunks)], axis=1)
            mixed_cols.append(_dot(ws[g], rhs))
        rows = []
        for c in range(n_chunks):
            cs = slice(c * SGU_GROUP_DIM, (c + 1) * SGU_GROUP_DIM)
            mixed = jnp.concatenate([mixed_cols[g][:, cs] for g in range(SGU_GROUPS)], axis=1)
            rows.append(u[c * CHUNK:(c + 1) * CHUNK] * (mixed + bfull))
        return jnp.concatenate(rows, axis=0).astype(BF16)

    a = a_ref[...]
    uv_raw = _dot_t(a, wuv_ref[...])
    gate_raw = _dot_t(a, wg1_ref[...])
    for r0 in range(0, a_ref.shape[0], sub):
        rows = slice(r0, r0 + sub)
        y_sgu = _dot(gating_unit(uv_raw[rows]), wos_ref[...])
        m_ref[rows, :] = jax.nn.sigmoid(gate_raw[rows] + bg1_ref[...]) * y_sgu


def _sgu_branch(a, w_uv, sgu_g, w_s, b_full, w_o_sgu, w_g1, b_g1, cast_w, tm, sub):
    T, D = a.shape
    row = lambda i: (i, 0)
    cast_in, cast_out = _cast_block_specs(*cast_w.shape, 0, T // tm, lambda i: i)
    return pl.pallas_call(
        functools.partial(_sgu_kernel, sub=sub),
        grid=(T // tm,),
        in_specs=[
            pl.BlockSpec((tm, D), row),
            _resident(w_uv.shape),
            _resident(sgu_g.shape),
            _resident(w_s.shape),
            _resident(b_full.shape),
            _resident(w_o_sgu.shape),
            _resident(w_g1.shape),
            _resident(b_g1.shape),
            cast_in,
        ],
        out_specs=[pl.BlockSpec((tm, D), row), cast_out],
        out_shape=[jax.ShapeDtypeStruct((T, D), F32), jax.ShapeDtypeStruct(cast_w.shape, BF16)],
        compiler_params=_params(1),
        name="sgu_branch",
    )(a, w_uv, sgu_g, w_s, b_full, w_o_sgu, w_g1, b_g1, cast_w)


def _attn_kernel(q_ref, kn_ref, kpe_ref, v_ref, *rest, tq, n_cast):
    cast_in, (o_ref,), cast_out, (kf_ref, vf_ref) = (
        rest[:n_cast], rest[n_cast:n_cast + 1], rest[n_cast + 1:2 * n_cast + 1], rest[2 * n_cast + 1:])
    _cast_rows(cast_in, cast_out)
    seq = q_ref.shape[2]
    row = lax.broadcasted_iota(jnp.int32, (tq, tq), 0)
    col = lax.broadcasted_iota(jnp.int32, (tq, tq), 1)
    causal = row >= col
    neg = jnp.finfo(F32).min
    heads = q_ref.shape[1]
    for hh in range(heads):
        kf_ref[hh, :, 0:LANES] = kn_ref[0, hh]
        kf_ref[hh, :, LANES:2 * LANES] = kpe_ref[0]
        vf_ref[hh, :, 0:LANES] = v_ref[0, hh]
        vf_ref[hh, :, LANES:2 * LANES] = jnp.ones((seq, LANES), vf_ref.dtype)

    nt = (((1,), (1,)), ((), ()))
    def scores(pair, hh):
        q0 = pair * 2 * tq
        k1, k2 = q0 + tq, q0 + 2 * tq
        q = q_ref[0, hh, q0:k2, :]
        return (lax.dot_general(q, kf_ref[hh, 0:k1, :], nt, preferred_element_type=F32),
                lax.dot_general(q[tq:], kf_ref[hh, k1:k2, :], nt, preferred_element_type=F32))

    def finish(pair, hh, s_main, s_ext):
        q0 = pair * 2 * tq
        k1, k2 = q0 + tq, q0 + 2 * tq
        top = s_main[0:tq]
        top_diag = jnp.where(causal, top[:, q0:k1], neg)
        top = jnp.concatenate([top[:, 0:q0], top_diag], axis=1) if pair else top_diag
        bot = s_main[tq:]
        ext = jnp.where(causal, s_ext, neg)
        m_top = jnp.max(top, axis=-1, keepdims=True)
        m_bot = jnp.maximum(jnp.max(bot, axis=-1, keepdims=True),
                            jnp.max(ext, axis=-1, keepdims=True))
        p_main = jnp.concatenate([jnp.exp2(top - m_top), jnp.exp2(bot - m_bot)], axis=0)
        acc = _dot(p_main.astype(BF16), vf_ref[hh, 0:k1, :])
        acc_bot = acc[tq:] + _dot(jnp.exp2(ext - m_bot).astype(BF16), vf_ref[hh, k1:k2, :])
        lanes = slice(hh * LANES, (hh + 1) * LANES)
        o_ref[0, q0:k1, lanes] = (acc[0:tq, 0:LANES] / acc[0:tq, LANES:]).astype(o_ref.dtype)
        o_ref[0, k1:k2, lanes] = (acc_bot[:, 0:LANES] / acc_bot[:, LANES:]).astype(o_ref.dtype)

    work = [(pair, hh) for pair in reversed(range(seq // (2 * tq))) for hh in range(heads)]
    lead = 2
    pending = [scores(*w) for w in work[:lead]]
    for idx, (pair, hh) in enumerate(work):
        if idx + lead < len(work):
            pending.append(scores(*work[idx + lead]))
        finish(pair, hh, *pending.pop(0))


def _attention(q, k_nope, k_pe, v, casts, tq, hb):
    B, H, S, _ = q.shape
    groups = H // hb
    specs = [_cast_block_specs(n, w.shape[1], r0, B * groups, lambda b, g: b * groups + g)
             for w, r0, n in casts]
    outs = pl.pallas_call(
        functools.partial(_attn_kernel, tq=tq, n_cast=len(casts)),
        grid=(B, groups),
        in_specs=[
            pl.BlockSpec((1, hb, S, 2 * LANES), lambda b, g: (b, g, 0, 0)),
            pl.BlockSpec((1, hb, S, LANES), lambda b, g: (b, g, 0, 0)),
            pl.BlockSpec((1, S, LANES), lambda b, g: (b, 0, 0)),
            pl.BlockSpec((1, hb, S, LANES), lambda b, g: (b, g, 0, 0)),
        ] + [s_in for s_in, _ in specs],
        out_specs=[pl.BlockSpec((1, S, hb * LANES), lambda b, g: (b, 0, g))]
        + [s_out for _, s_out in specs],
        out_shape=[jax.ShapeDtypeStruct((B, S, H * V_HEAD_DIM), BF16)]
        + [jax.ShapeDtypeStruct((n, w.shape[1]), BF16) for w, _, n in casts],
        scratch_shapes=[pltpu.VMEM((hb, S, 2 * LANES), BF16), pltpu.VMEM((hb, S, 2 * LANES), BF16)],
        compiler_params=_params(2),
        name="mla_attention",
    )(q, k_nope, k_pe, v, *[w for w, _, _ in casts])
    return outs[0], outs[1:]


def _merge_kernel(attn_ref, a_ref, m_ref, x_ref, woa_ref, wg0_ref, bg0_ref, wout_ref, fg_ref,
                  cast_ref, h_ref, f_ref, cast_out_ref):
    _cast_rows([cast_ref], [cast_out_ref])
    y_attn = _dot(attn_ref[...], woa_ref[...])
    gate = jax.nn.sigmoid(_dot_t(a_ref[...], wg0_ref[...]) + bg0_ref[...])
    merged = (gate * y_attn + m_ref[...]).astype(BF16)
    h = x_ref[...] + _dot(merged, wout_ref[...])
    h_ref[...] = h
    f_ref[...] = _rms(h, fg_ref[...]).astype(BF16)


def _merge(attn, a, m_sgu, x, w_o_attn, w_g0, b_g0, w_out, ffn_g, cast_w, tm):
    T, D = x.shape
    row = lambda i: (i, 0)
    tile = pl.BlockSpec((tm, D), row)
    cast_in, cast_out = _cast_block_specs(*cast_w.shape, 0, T // tm, lambda i: i)
    return pl.pallas_call(
        _merge_kernel,
        grid=(T // tm,),
        in_specs=[tile, tile, tile, tile,
                  _resident(w_o_attn.shape), _resident(w_g0.shape), _resident(b_g0.shape),
                  _resident(w_out.shape), _resident(ffn_g.shape), cast_in],
        out_specs=[tile, tile, cast_out],
        out_shape=[jax.ShapeDtypeStruct((T, D), F32), jax.ShapeDtypeStruct((T, D), BF16),
                   jax.ShapeDtypeStruct(cast_w.shape, BF16)],
        compiler_params=_params(1),
        name="merge_outproj",
    )(attn, a, m_sgu, x, w_o_attn, w_g0, b_g0, w_out, ffn_g, cast_w)


def _ffn_kernel(f_ref, h_ref, wg_ref, wu_ref, wd_ref, ng_ref, o_ref, *, sub):
    j = pl.program_id(1)

    f = f_ref[...]
    starts = list(range(0, wg_ref.shape[1], sub))
    projected = [(_dot(f, wg_ref[:, c0:c0 + sub]), _dot(f, wu_ref[:, c0:c0 + sub]))
                 for c0 in starts]
    act = jnp.concatenate(
        [(jax.nn.silu(gate) * up).astype(BF16) for gate, up in projected], axis=1)
    o_ref[...] = jnp.where(j == 0, h_ref[...], o_ref[...]) + _dot(act, wd_ref[...])

    @pl.when(j == pl.num_programs(1) - 1)
    def _():
        o_ref[...] = _rms(o_ref[...], ng_ref[...])


def _ffn(f, h, w_gate, w_up, w_down, final_g, tm, tf, sub):
    T, D = h.shape
    d_ff = w_gate.shape[1]
    row = lambda i, j: (i, 0)
    return pl.pallas_call(
        functools.partial(_ffn_kernel, sub=sub),
        grid=(T // tm, d_ff // tf),
        in_specs=[
            pl.BlockSpec((tm, D), row),
            pl.BlockSpec((tm, D), row),
            pl.BlockSpec((D, tf), lambda i, j: (0, j)),
            pl.BlockSpec((D, tf), lambda i, j: (0, j)),
            pl.BlockSpec((tf, D), lambda i, j: (j, 0)),
            _resident((1, D)),
        ],
        out_specs=pl.BlockSpec((tm, D), row),
        out_shape=jax.ShapeDtypeStruct((T, D), F32),
        compiler_params=_params(2),
        name="swiglu_ffn",
    )(f, h, w_gate, w_up, w_down, final_g)


def kernel(x, positions, norm_mix_g, w_in, b_gate, q_norm_g, w_uq, kv_norm_g, w_ukv, w_o_attn,
           sgu_norm_g, w_sgu, b_sgu, w_o_sgu, w_out, norm_ffn_g, w_gate_ffn, w_up_ffn,
           w_down_ffn, norm_final_g):
    B, S, D = x.shape
    T = B * S
    depth = w_in.shape[0]
    assert depth == 1, "the final norm is fused into the FFN epilogue of a single layer"
    assert w_in.shape[1:] == (D, D_IN)

    cs = _rope_tables(positions).reshape(B, S, LANES)
    row_vec = lambda v: v.reshape(1, -1).astype(F32)

    h = x
    out = None
    for l in range(depth):
        w_in_t = jnp.swapaxes(w_in[l], 0, 1)
        w_lat_t, w_uq_p, w_ukv_p = _prep_inproj_weights(w_in_t, w_uq[l], w_ukv[l])
        b_full = jnp.repeat(b_sgu[l].T, SGU_GROUP_DIM, axis=1).astype(F32)

        a, q, k_nope, v, k_pe = _inproj(
            h, row_vec(norm_mix_g[l]), cs, w_lat_t, row_vec(q_norm_g[l]), row_vec(kv_norm_g[l]),
            w_uq_p, w_ukv_p, tm=INPROJ_ROWS, sub=ROW_SUB)
        a2 = a.reshape(T, D)
        whole = lambda w: (w, 0, w.shape[0])
        attn, (w_uv_t, w_g0_t, w_g1_t, w_os, w_oa, w_o, w_uf) = _attention(
            q, k_nope, k_pe, v,
            [(w_in_t, UV_OFF, 2 * SGU_WIDTH), (w_in_t, GATE_OFF, D), (w_in_t, GATE_OFF + D, D),
             whole(w_o_sgu[l]), whole(w_o_attn[l]), whole(w_out[l]), whole(w_up_ffn[l])],
            tq=ATTN_Q_ROWS, hb=ATTN_HEADS_PER_STEP)
        m_sgu, w_gf = _sgu_branch(
            a2, w_uv_t, row_vec(sgu_norm_g[l]), w_sgu[l], b_full, w_os, w_g1_t,
            row_vec(b_gate[l, D:]), w_gate_ffn[l], tm=SGU_ROWS, sub=ROW_SUB)
        h_mid, f, w_df = _merge(
            attn.reshape(T, D), a2, m_sgu, h.reshape(T, D), w_oa, w_g0_t,
            row_vec(b_gate[l, :D]), w_o, row_vec(norm_ffn_g[l]), w_down_ffn[l], tm=MERGE_ROWS)
        out = _ffn(f, h_mid, w_gf, w_uf, w_df, row_vec(norm_final_g), tm=FFN_ROWS,
                   tf=FFN_COLS, sub=FFN_COL_SUB)
        h = out.reshape(B, S, D)
    return h
```

```python
import functools

import jax
import jax.numpy as jnp
from jax import lax
from jax.experimental import pallas as pl
from jax.experimental.pallas import tpu as pltpu

D_MODEL = 2048
N_HEADS = 16
QK_NOPE_DIM = 128
QK_ROPE_DIM = 64
V_HEAD_DIM = 128
Q_LORA_RANK = 512
KV_LORA_RANK = 512
ROPE_THETA = 10000.0
SGU_GROUPS = 8
SGU_GROUP_DIM = 128
SGU_WIDTH = SGU_GROUPS * SGU_GROUP_DIM
CHUNK = 128
N_BRANCH = 2
RMS_EPS = 1e-6
KPE_OFF = Q_LORA_RANK + KV_LORA_RANK
UV_OFF = KPE_OFF + QK_ROPE_DIM
GATE_OFF = UV_OFF + 2 * SGU_WIDTH
D_IN = GATE_OFF + N_BRANCH * D_MODEL
LANES = 128
HALF_ROPE = QK_ROPE_DIM // 2
LOG2_E = 1.4426950408889634
QK_LOG2_SCALE = (QK_NOPE_DIM + QK_ROPE_DIM) ** -0.5 * LOG2_E
BF16_SUBLANES = 16

VMEM_LIMIT_BYTES = 60 * 1024 * 1024

INPROJ_ROWS = 512
SGU_ROWS = 512
MERGE_ROWS = 256
ROW_SUB = 256
ATTN_Q_ROWS = 256
ATTN_HEADS_PER_STEP = 2
FFN_ROWS = 1024
FFN_COLS = 512
FFN_COL_SUB = 256

F32 = jnp.float32
BF16 = jnp.bfloat16


def _rms(x, g):
    return x * lax.rsqrt(jnp.mean(x * x, axis=-1, keepdims=True) + RMS_EPS) * g


def _dot(a, b):
    return jnp.dot(a, b, preferred_element_type=F32)


def _dot_t(a, b_t):
    return lax.dot_general(a, b_t, (((1,), (1,)), ((), ())), preferred_element_type=F32)


def _resident(shape):
    return pl.BlockSpec(shape, lambda *_: (0,) * len(shape), pipeline_mode=pl.Buffered(1))


def _params(n_axes):
    return pltpu.CompilerParams(
        dimension_semantics=("arbitrary",) * n_axes, vmem_limit_bytes=VMEM_LIMIT_BYTES)


def _cast_block_specs(n_rows, n_cols, row0, n_steps, linear_step):
    share = 1 if (n_rows // n_steps) % BF16_SUBLANES == 0 else 2
    blk = n_rows * share // n_steps
    assert blk * n_steps == n_rows * share and blk % BF16_SUBLANES == 0 and row0 % blk == 0
    first = row0 // blk
    return (pl.BlockSpec((blk, n_cols), lambda *idx: (first + linear_step(*idx) // share, 0)),
            pl.BlockSpec((blk, n_cols), lambda *idx: (linear_step(*idx) // share, 0)))


def _cast_rows(srcs, dsts):
    for src, dst in zip(srcs, dsts):
        dst[...] = src[...].astype(dst.dtype)


def _rope_table_kernel(pos_ref, freq_ref, cs_ref):
    ang = pos_ref[...] * freq_ref[...]
    cos, sin = jnp.cos(ang), jnp.sin(ang)
    n_rows = ang.shape[0]
    per_row = LANES // HALF_ROPE
    lane = lax.broadcasted_iota(jnp.int32, ang.shape, 1)

    def lanes_from(x, src, dst):
        shift = (dst - src) % LANES
        return pltpu.roll(x, shift, 1) if shift else x

    for k in range(per_row):
        src = k * HALF_ROPE
        row = jnp.where(
            lane < HALF_ROPE, lanes_from(cos, src, 0),
            jnp.where(lane < 2 * HALF_ROPE, lanes_from(cos, src, HALF_ROPE),
                      jnp.where(lane < 3 * HALF_ROPE, -lanes_from(sin, src, 2 * HALF_ROPE),
                                lanes_from(sin, src, 3 * HALF_ROPE))))
        cs_ref[pl.ds(k, n_rows, stride=per_row), :] = row


def _rope_tables(positions):
    n_tok = positions.size
    per_row = LANES // HALF_ROPE
    inv_freq = ROPE_THETA ** (-jnp.arange(0, QK_ROPE_DIM, 2, dtype=F32) / QK_ROPE_DIM)
    pos_rep = jnp.repeat(positions.astype(F32).reshape(n_tok // per_row, per_row), HALF_ROPE, axis=1)
    freq = jnp.tile(inv_freq, per_row).reshape(1, LANES)
    return pl.pallas_call(
        _rope_table_kernel, out_shape=jax.ShapeDtypeStruct((n_tok, LANES), F32),
        name="rope_tables")(pos_rep, freq)


def _rope_dup(x, cs):
    y = x * cs
    return y + pltpu.roll(y, LANES // 2, 1)


def _rope_pair(p, cos4, sin4):
    lane = lax.broadcasted_iota(jnp.int32, p.shape, 1)
    first_half = lane % QK_ROPE_DIM < HALF_ROPE
    partner = jnp.where(first_half, pltpu.roll(p, LANES - HALF_ROPE, 1), pltpu.roll(p, HALF_ROPE, 1))
    return p * cos4 + partner * sin4


def _prep_kernel(wlat_ref, wuq_ref, wukv_ref, olat_ref, ouq_ref, oukv_ref):
    olat_ref[0:KPE_OFF] = wlat_ref[0:KPE_OFF].astype(BF16)
    x1 = wlat_ref[KPE_OFF:KPE_OFF + HALF_ROPE].astype(BF16)
    x2 = wlat_ref[KPE_OFF + HALF_ROPE:UV_OFF].astype(BF16)
    for k, part in enumerate((x1, x2, x2, x1)):
        olat_ref[KPE_OFF + k * HALF_ROPE:KPE_OFF + (k + 1) * HALF_ROPE] = part
    oukv_ref[...] = wukv_ref[...].astype(BF16)
    half = LANES // 2
    nope_cols = N_HEADS * QK_NOPE_DIM
    lane = lax.broadcasted_iota(jnp.int32, (wuq_ref.shape[0], LANES), 1)
    for pair in range(N_HEADS // 2):
        t0, t1, t2 = (wuq_ref[:, (3 * pair + k) * LANES:(3 * pair + k + 1) * LANES] for k in range(3))
        nope_odd = jnp.where(lane < half, pltpu.roll(t1, half, 1), pltpu.roll(t2, half, 1))
        for h, nope in ((2 * pair, t0), (2 * pair + 1, nope_odd)):
            ouq_ref[:, h * LANES:(h + 1) * LANES] = nope.astype(BF16)
        ouq_ref[:, nope_cols + pair * LANES:nope_cols + (pair + 1) * LANES] = (
            jnp.where(lane < half, t1, t2).astype(BF16))


def _prep_inproj_weights(w_in_t, w_uq, w_ukv, n_chunks=4):
    d = w_in_t.shape[1]
    r_q, r_kv = w_uq.shape[0], w_ukv.shape[0]
    lat_rows = UV_OFF + LANES - QK_ROPE_DIM
    uq_cols = N_HEADS * QK_NOPE_DIM + (N_HEADS // 2) * LANES
    return pl.pallas_call(
        _prep_kernel,
        grid=(n_chunks,),
        in_specs=[
            pl.BlockSpec((UV_OFF, d // n_chunks), lambda c: (0, c)),
            pl.BlockSpec((r_q // n_chunks, w_uq.shape[1]), lambda c: (c, 0)),
            pl.BlockSpec((r_kv // n_chunks, w_ukv.shape[1]), lambda c: (c, 0)),
        ],
        out_specs=[
            pl.BlockSpec((lat_rows, d // n_chunks), lambda c: (0, c)),
            pl.BlockSpec((r_q // n_chunks, uq_cols), lambda c: (c, 0)),
            pl.BlockSpec((r_kv // n_chunks, w_ukv.shape[1]), lambda c: (c, 0)),
        ],
        out_shape=[
            jax.ShapeDtypeStruct((lat_rows, d), BF16),
            jax.ShapeDtypeStruct((r_q, uq_cols), BF16),
            jax.ShapeDtypeStruct(w_ukv.shape, BF16),
        ],
        compiler_params=_params(1),
        name="prep_inproj_weights",
    )(w_in_t, w_uq, w_ukv)


def _inproj_kernel(x_ref, g_ref, cs_ref, wlat_ref, qg_ref, kvg_ref, wuq_ref, wukv_ref,
                   a_ref, qn_ref, qpe_ref, kn_ref, v_ref, kpe_ref, *, sub):
    def latents(r0):
        a = _rms(x_ref[0, r0:r0 + sub, :], g_ref[...]).astype(BF16)
        a_ref[0, r0:r0 + sub, :] = a
        return _dot_t(a, wlat_ref[...])

    starts = list(range(0, x_ref.shape[1], sub))
    for r0, z in zip(starts, [latents(r0) for r0 in starts]):
        rows = slice(r0, r0 + sub)
        qn = (_rms(z[:, :Q_LORA_RANK], qg_ref[...]) * QK_LOG2_SCALE).astype(BF16)
        kvn = _rms(z[:, Q_LORA_RANK:Q_LORA_RANK + KV_LORA_RANK], kvg_ref[...]).astype(BF16)
        cs = cs_ref[0, rows, :]
        kpe = _rope_dup(z[:, Q_LORA_RANK + KV_LORA_RANK:], cs)
        lane = lax.broadcasted_iota(jnp.int32, kpe.shape, 1)
        low = lane < QK_ROPE_DIM
        kpe_ref[0, rows, 0:LANES] = jnp.where(low, kpe, 0.0).astype(BF16)
        kpe_ref[0, rows, LANES:2 * LANES] = jnp.where(low, 0.0, kpe).astype(BF16)
        cs_swapped = pltpu.roll(cs, LANES // 2, 1)
        cos4 = jnp.where(low, cs, cs_swapped)
        sin4 = jnp.where(low, cs_swapped, cs)

        heads_per_dot = 4
        width = heads_per_dot * LANES
        nope_cols = N_HEADS * QK_NOPE_DIM
        for hg in range(N_HEADS // heads_per_dot):
            c0 = hg * width
            q_nope = _dot(qn, wuq_ref[:, c0:c0 + width])
            pe0 = nope_cols + c0 // 2
            q_pe = _dot(qn, wuq_ref[:, pe0:pe0 + width // 2])
            kv0 = 2 * c0
            kv_a = _dot(kvn, wukv_ref[:, kv0:kv0 + width])
            kv_b = _dot(kvn, wukv_ref[:, kv0 + width:kv0 + 2 * width])
            for pp in range(heads_per_dot // 2):
                pair = hg * (heads_per_dot // 2) + pp
                qpe_ref[0, pair, rows, :] = _rope_pair(
                    q_pe[:, pp * LANES:(pp + 1) * LANES], cos4, sin4).astype(BF16)
            for hh in range(heads_per_dot):
                h = hg * heads_per_dot + hh
                sl = slice(hh * LANES, (hh + 1) * LANES)
                qn_ref[0, h, rows, :] = q_nope[:, sl].astype(BF16)
                kv = kv_a if hh < heads_per_dot // 2 else kv_b
                k0 = (hh % (heads_per_dot // 2)) * 2 * LANES
                kn_ref[0, h, rows, :] = kv[:, k0:k0 + LANES].astype(BF16)
                v_ref[0, h, rows, :] = kv[:, k0 + LANES:k0 + 2 * LANES].astype(BF16)


def _inproj(x, norm_g, cs, w_lat_t, q_g, kv_g, w_uq, w_ukv, tm, sub):
    B, S, D = x.shape
    row = lambda b, i: (b, i, 0)
    head = lambda b, i: (b, 0, i, 0)
    return pl.pallas_call(
        functools.partial(_inproj_kernel, sub=sub),
        grid=(B, S // tm),
        in_specs=[
            pl.BlockSpec((1, tm, D), row),
            _resident((1, D)),
            pl.BlockSpec((1, tm, LANES), row),
            _resident(w_lat_t.shape),
            _resident((1, Q_LORA_RANK)),
            _resident((1, KV_LORA_RANK)),
            _resident(w_uq.shape),
            _resident(w_ukv.shape),
        ],
        out_specs=[
            pl.BlockSpec((1, tm, D), row),
            pl.BlockSpec((1, N_HEADS, tm, LANES), head),
            pl.BlockSpec((1, N_HEADS // 2, tm, LANES), head),
            pl.BlockSpec((1, N_HEADS, tm, LANES), head),
            pl.BlockSpec((1, N_HEADS, tm, LANES), head),
            pl.BlockSpec((1, tm, 2 * LANES), row),
        ],
        out_shape=[
            jax.ShapeDtypeStruct((B, S, D), BF16),
            jax.ShapeDtypeStruct((B, N_HEADS, S, LANES), BF16),
            jax.ShapeDtypeStruct((B, N_HEADS // 2, S, LANES), BF16),
            jax.ShapeDtypeStruct((B, N_HEADS, S, LANES), BF16),
            jax.ShapeDtypeStruct((B, N_HEADS, S, LANES), BF16),
            jax.ShapeDtypeStruct((B, S, 2 * LANES), BF16),
        ],
        compiler_params=_params(2),
        name="inproj",
    )(x, norm_g, cs, w_lat_t, q_g, kv_g, w_uq, w_ukv)


def _sgu_kernel(a_ref, wuv_ref, sg_ref, ws_ref, bfull_ref, wos_ref, wg1_ref, bg1_ref, cast_ref,
                m_ref, cast_out_ref, *, sub):
    _cast_rows([cast_ref], [cast_out_ref])
    n_chunks = sub // CHUNK
    t_idx = lax.broadcasted_iota(jnp.int32, (CHUNK, CHUNK), 0)
    s_idx = lax.broadcasted_iota(jnp.int32, (CHUNK, CHUNK), 1)
    causal = t_idx >= s_idx
    ws = [jnp.where(causal, ws_ref[g], 0.0).astype(BF16) for g in range(SGU_GROUPS)]
    bfull = bfull_ref[...]
    def gating_unit(uv_raw):
        uv = jax.nn.gelu(uv_raw)
        u = uv[:, :SGU_WIDTH]
        vn = _rms(uv[:, SGU_WIDTH:], sg_ref[...]).astype(BF16)
        mixed_cols = []
        for g in range(SGU_GROUPS):
            gs = slice(g * SGU_GROUP_DIM, (g + 1) * SGU_GROUP_DIM)
            rhs = jnp.concatenate(
                [vn[c * CHUNK:(c + 1) * CHUNK, gs] for c in range(n_chunks)], axis=1)
            mixed_cols.append(_dot(ws[g], rhs))
        rows = []
        for c in range(n_chunks):
            cs = slice(c * SGU_GROUP_DIM, (c + 1) * SGU_GROUP_DIM)
            mixed = jnp.concatenate([mixed_cols[g][:, cs] for g in range(SGU_GROUPS)], axis=1)
            rows.append(u[c * CHUNK:(c + 1) * CHUNK] * (mixed + bfull))
        return jnp.concatenate(rows, axis=0).astype(BF16)

    a = a_ref[...]
    uv_raw = _dot_t(a, wuv_ref[...])
    gate_raw = _dot_t(a, wg1_ref[...])
    for r0 in range(0, a_ref.shape[0], sub):
        rows = slice(r0, r0 + sub)
        y_sgu = _dot(gating_unit(uv_raw[rows]), wos_ref[...])
        m_ref[rows, :] = jax.nn.sigmoid(gate_raw[rows] + bg1_ref[...]) * y_sgu


def _sgu_branch(a, w_uv, sgu_g, w_s, b_full, w_o_sgu, w_g1, b_g1, cast_w, tm, sub):
    T, D = a.shape
    row = lambda i: (i, 0)
    cast_in, cast_out = _cast_block_specs(*cast_w.shape, 0, T // tm, lambda i: i)
    return pl.pallas_call(
        functools.partial(_sgu_kernel, sub=sub),
        grid=(T // tm,),
        in_specs=[
            pl.BlockSpec((tm, D), row),
            _resident(w_uv.shape),
            _resident(sgu_g.shape),
            _resident(w_s.shape),
            _resident(b_full.shape),
            _resident(w_o_sgu.shape),
            _resident(w_g1.shape),
            _resident(b_g1.shape),
            cast_in,
        ],
        out_specs=[pl.BlockSpec((tm, D), row), cast_out],
        out_shape=[jax.ShapeDtypeStruct((T, D), F32), jax.ShapeDtypeStruct(cast_w.shape, BF16)],
        compiler_params=_params(1),
        name="sgu_branch",
    )(a, w_uv, sgu_g, w_s, b_full, w_o_sgu, w_g1, b_g1, cast_w)


def _attn_kernel(qn_ref, qpe_ref, kn_ref, kpe_ref, v_ref, *rest, tq, n_cast):
    cast_in, (o_ref,), cast_out, (kf_ref, vf_ref) = (
        rest[:n_cast], rest[n_cast:n_cast + 1], rest[n_cast + 1:2 * n_cast + 1], rest[2 * n_cast + 1:])
    _cast_rows(cast_in, cast_out)
    seq = qn_ref.shape[2]
    row = lax.broadcasted_iota(jnp.int32, (tq, tq), 0)
    col = lax.broadcasted_iota(jnp.int32, (tq, tq), 1)
    causal = row >= col
    neg = jnp.finfo(F32).min
    heads = qn_ref.shape[1]
    for hh in range(heads):
        parity = hh % 2
        kf_ref[hh, :, 0:LANES] = kn_ref[0, hh]
        kf_ref[hh, :, LANES:2 * LANES] = kpe_ref[0, :, parity * LANES:(parity + 1) * LANES]
        vf_ref[hh, :, 0:LANES] = v_ref[0, hh]
        vf_ref[hh, :, LANES:2 * LANES] = jnp.ones((seq, LANES), vf_ref.dtype)

    nt = (((1,), (1,)), ((), ()))
    def scores(pair, hh):
        q0 = pair * 2 * tq
        k1, k2 = q0 + tq, q0 + 2 * tq
        q = jnp.concatenate([qn_ref[0, hh, q0:k2, :], qpe_ref[0, hh // 2, q0:k2, :]], axis=1)
        return (lax.dot_general(q, kf_ref[hh, 0:k1, :], nt, preferred_element_type=F32),
                lax.dot_general(q[tq:], kf_ref[hh, k1:k2, :], nt, preferred_element_type=F32))

    def finish(pair, hh, s_main, s_ext):
        q0 = pair * 2 * tq
        k1, k2 = q0 + tq, q0 + 2 * tq
        top = s_main[0:tq]
        top_diag = jnp.where(causal, top[:, q0:k1], neg)
        top = jnp.concatenate([top[:, 0:q0], top_diag], axis=1) if pair else top_diag
        bot = s_main[tq:]
        ext = jnp.where(causal, s_ext, neg)
        m_top = jnp.max(top, axis=-1, keepdims=True)
        m_bot = jnp.maximum(jnp.max(bot, axis=-1, keepdims=True),
                            jnp.max(ext, axis=-1, keepdims=True))
        p_main = jnp.concatenate([jnp.exp2(top - m_top), jnp.exp2(bot - m_bot)], axis=0)
        acc = _dot(p_main.astype(BF16), vf_ref[hh, 0:k1, :])
        acc_bot = acc[tq:] + _dot(jnp.exp2(ext - m_bot).astype(BF16), vf_ref[hh, k1:k2, :])
        lanes = slice(hh * LANES, (hh + 1) * LANES)
        o_ref[0, q0:k1, lanes] = (acc[0:tq, 0:LANES] / acc[0:tq, LANES:]).astype(o_ref.dtype)
        o_ref[0, k1:k2, lanes] = (acc_bot[:, 0:LANES] / acc_bot[:, LANES:]).astype(o_ref.dtype)

    work = [(pair, hh) for pair in reversed(range(seq // (2 * tq))) for hh in range(heads)]
    lead = 2
    pending = [scores(*w) for w in work[:lead]]
    for idx, (pair, hh) in enumerate(work):
        if idx + lead < len(work):
            pending.append(scores(*work[idx + lead]))
        finish(pair, hh, *pending.pop(0))


def _attention(q_nope, q_pe, k_nope, k_pe, v, casts, tq, hb):
    B, H, S, _ = q_nope.shape
    assert hb % 2 == 0, "head pairs share a rope tile"
    groups = H // hb
    specs = [_cast_block_specs(n, w.shape[1], r0, B * groups, lambda b, g: b * groups + g)
             for w, r0, n in casts]
    outs = pl.pallas_call(
        functools.partial(_attn_kernel, tq=tq, n_cast=len(casts)),
        grid=(B, groups),
        in_specs=[
            pl.BlockSpec((1, hb, S, LANES), lambda b, g: (b, g, 0, 0)),
            pl.BlockSpec((1, hb // 2, S, LANES), lambda b, g: (b, g, 0, 0)),
            pl.BlockSpec((1, hb, S, LANES), lambda b, g: (b, g, 0, 0)),
            pl.BlockSpec((1, S, 2 * LANES), lambda b, g: (b, 0, 0)),
            pl.BlockSpec((1, hb, S, LANES), lambda b, g: (b, g, 0, 0)),
        ] + [s_in for s_in, _ in specs],
        out_specs=[pl.BlockSpec((1, S, hb * LANES), lambda b, g: (b, 0, g))]
        + [s_out for _, s_out in specs],
        out_shape=[jax.ShapeDtypeStruct((B, S, H * V_HEAD_DIM), BF16)]
        + [jax.ShapeDtypeStruct((n, w.shape[1]), BF16) for w, _, n in casts],
        scratch_shapes=[pltpu.VMEM((hb, S, 2 * LANES), BF16), pltpu.VMEM((hb, S, 2 * LANES), BF16)],
        compiler_params=_params(2),
        name="mla_attention",
    )(q_nope, q_pe, k_nope, k_pe, v, *[w for w, _, _ in casts])
    return outs[0], outs[1:]


def _merge_kernel(attn_ref, a_ref, m_ref, x_ref, woa_ref, wg0_ref, bg0_ref, wout_ref, fg_ref,
                  cast_ref, h_ref, f_ref, cast_out_ref):
    _cast_rows([cast_ref], [cast_out_ref])
    y_attn = _dot(attn_ref[...], woa_ref[...])
    gate = jax.nn.sigmoid(_dot_t(a_ref[...], wg0_ref[...]) + bg0_ref[...])
    merged = (gate * y_attn + m_ref[...]).astype(BF16)
    h = x_ref[...] + _dot(merged, wout_ref[...])
    h_ref[...] = h
    f_ref[...] = _rms(h, fg_ref[...]).astype(BF16)


def _merge(attn, a, m_sgu, x, w_o_attn, w_g0, b_g0, w_out, ffn_g, cast_w, tm):
    T, D = x.shape
    row = lambda i: (i, 0)
    tile = pl.BlockSpec((tm, D), row)
    cast_in, cast_out = _cast_block_specs(*cast_w.shape, 0, T // tm, lambda i: i)
    return pl.pallas_call(
        _merge_kernel,
        grid=(T // tm,),
        in_specs=[tile, tile, tile, tile,
                  _resident(w_o_attn.shape), _resident(w_g0.shape), _resident(b_g0.shape),
                  _resident(w_out.shape), _resident(ffn_g.shape), cast_in],
        out_specs=[tile, tile, cast_out],
        out_shape=[jax.ShapeDtypeStruct((T, D), F32), jax.ShapeDtypeStruct((T, D), BF16),
                   jax.ShapeDtypeStruct(cast_w.shape, BF16)],
        compiler_params=_params(1),
        name="merge_outproj",
    )(attn, a, m_sgu, x, w_o_attn, w_g0, b_g0, w_out, ffn_g, cast_w)


def _ffn_kernel(f_ref, h_ref, wg_ref, wu_ref, wd_ref, ng_ref, o_ref, *, sub):
    j = pl.program_id(1)

    f = f_ref[...]
    starts = list(range(0, wg_ref.shape[1], sub))
    projected = [(_dot(f, wg_ref[:, c0:c0 + sub]), _dot(f, wu_ref[:, c0:c0 + sub]))
                 for c0 in starts]
    act = jnp.concatenate(
        [(jax.nn.silu(gate) * up).astype(BF16) for gate, up in projected], axis=1)
    o_ref[...] = jnp.where(j == 0, h_ref[...], o_ref[...]) + _dot(act, wd_ref[...])

    @pl.when(j == pl.num_programs(1) - 1)
    def _():
        o_ref[...] = _rms(o_ref[...], ng_ref[...])


def _ffn(f, h, w_gate, w_up, w_down, final_g, tm, tf, sub):
    T, D = h.shape
    d_ff = w_gate.shape[1]
    row = lambda i, j: (i, 0)
    n_i, n_j = T // tm, d_ff // tf
    h_row = lambda i, j: (jnp.minimum(i + (j >= n_j // 2).astype(jnp.int32), n_i - 1), 0)
    return pl.pallas_call(
        functools.partial(_ffn_kernel, sub=sub),
        grid=(n_i, n_j),
        in_specs=[
            pl.BlockSpec((tm, D), row),
            pl.BlockSpec((tm, D), h_row),
            pl.BlockSpec((D, tf), lambda i, j: (0, j)),
            pl.BlockSpec((D, tf), lambda i, j: (0, j)),
            pl.BlockSpec((tf, D), lambda i, j: (j, 0)),
            _resident((1, D)),
        ],
        out_specs=pl.BlockSpec((tm, D), row),
        out_shape=jax.ShapeDtypeStruct((T, D), F32),
        compiler_params=_params(2),
        name="swiglu_ffn",
    )(f, h, w_gate, w_up, w_down, final_g)


def kernel(x, positions, norm_mix_g, w_in, b_gate, q_norm_g, w_uq, kv_norm_g, w_ukv, w_o_attn,
           sgu_norm_g, w_sgu, b_sgu, w_o_sgu, w_out, norm_ffn_g, w_gate_ffn, w_up_ffn,
           w_down_ffn, norm_final_g):
    B, S, D = x.shape
    T = B * S
    depth = w_in.shape[0]
    assert depth == 1, "the final norm is fused into the FFN epilogue of a single layer"
    assert w_in.shape[1:] == (D, D_IN)

    cs = _rope_tables(positions).reshape(B, S, LANES)
    row_vec = lambda v: v.reshape(1, -1).astype(F32)

    h = x
    out = None
    for l in range(depth):
        w_in_t = jnp.swapaxes(w_in[l], 0, 1)
        w_lat_t, w_uq_p, w_ukv_p = _prep_inproj_weights(w_in_t, w_uq[l], w_ukv[l])
        b_full = jnp.repeat(b_sgu[l].T, SGU_GROUP_DIM, axis=1).astype(F32)

        a, q_nope, q_pe, k_nope, v, k_pe = _inproj(
            h, row_vec(norm_mix_g[l]), cs, w_lat_t, row_vec(q_norm_g[l]), row_vec(kv_norm_g[l]),
            w_uq_p, w_ukv_p, tm=INPROJ_ROWS, sub=ROW_SUB)
        a2 = a.reshape(T, D)
        whole = lambda w: (w, 0, w.shape[0])
        attn, (w_uv_t, w_g0_t, w_g1_t, w_os, w_oa, w_o, w_uf) = _attention(
            q_nope, q_pe, k_nope, k_pe, v,
            [(w_in_t, UV_OFF, 2 * SGU_WIDTH), (w_in_t, GATE_OFF, D), (w_in_t, GATE_OFF + D, D),
             whole(w_o_sgu[l]), whole(w_o_attn[l]), whole(w_out[l]), whole(w_up_ffn[l])],
            tq=ATTN_Q_ROWS, hb=ATTN_HEADS_PER_STEP)
        m_sgu, w_gf = _sgu_branch(
            a2, w_uv_t, row_vec(sgu_norm_g[l]), w_sgu[l], b_full, w_os, w_g1_t,
            row_vec(b_gate[l, D:]), w_gate_ffn[l], tm=SGU_ROWS, sub=ROW_SUB)
        h_mid, f, w_df = _merge(
            attn.reshape(T, D), a2, m_sgu, h.reshape(T, D), w_oa, w_g0_t,
            row_vec(b_gate[l, :D]), w_o, row_vec(norm_ffn_g[l]), w_down_ffn[l], tm=MERGE_ROWS)
        out = _ffn(f, h_mid, w_gf, w_uf, w_df, row_vec(norm_final_g), tm=FFN_ROWS,
                   tf=FFN_COLS, sub=FFN_COL_SUB)
        h = out.reshape(B, S, D)
    return h
```

```python
import functools

import jax
import jax.numpy as jnp
from jax import lax
from jax.experimental import pallas as pl
from jax.experimental.pallas import tpu as pltpu

D_MODEL = 2048
N_HEADS = 16
QK_NOPE_DIM = 128
QK_ROPE_DIM = 64
V_HEAD_DIM = 128
Q_LORA_RANK = 512
KV_LORA_RANK = 512
ROPE_THETA = 10000.0
SGU_GROUPS = 8
SGU_GROUP_DIM = 128
SGU_WIDTH = SGU_GROUPS * SGU_GROUP_DIM
CHUNK = 128
N_BRANCH = 2
RMS_EPS = 1e-6
KPE_OFF = Q_LORA_RANK + KV_LORA_RANK
UV_OFF = KPE_OFF + QK_ROPE_DIM
GATE_OFF = UV_OFF + 2 * SGU_WIDTH
D_IN = GATE_OFF + N_BRANCH * D_MODEL
LANES = 128
HALF_ROPE = QK_ROPE_DIM // 2
LOG2_E = 1.4426950408889634
QK_LOG2_SCALE = (QK_NOPE_DIM + QK_ROPE_DIM) ** -0.5 * LOG2_E
BF16_SUBLANES = 16

VMEM_LIMIT_BYTES = 60 * 1024 * 1024

INPROJ_ROWS = 512
SGU_ROWS = 512
MERGE_ROWS = 256
ROW_SUB = 256
ATTN_Q_ROWS = 256
ATTN_HEADS_PER_STEP = 2
FFN_ROWS = 1024
FFN_COLS = 512
FFN_COL_SUB = 256
PREP_STEPS = 4

F32 = jnp.float32
BF16 = jnp.bfloat16


def _rms(x, g):
    return x * lax.rsqrt(jnp.mean(x * x, axis=-1, keepdims=True) + RMS_EPS) * g


def _dot(a, b):
    return jnp.dot(a, b, preferred_element_type=F32)


def _dot_t(a, b_t):
    return lax.dot_general(a, b_t, (((1,), (1,)), ((), ())), preferred_element_type=F32)


def _resident(shape):
    return pl.BlockSpec(shape, lambda *_: (0,) * len(shape), pipeline_mode=pl.Buffered(1))


def _params(n_axes):
    return pltpu.CompilerParams(
        dimension_semantics=("arbitrary",) * n_axes, vmem_limit_bytes=VMEM_LIMIT_BYTES)


def _cast_block_specs(n_rows, n_cols, row0, n_steps, linear_step):
    share = 1 if (n_rows // n_steps) % BF16_SUBLANES == 0 else 2
    blk = n_rows * share // n_steps
    assert blk * n_steps == n_rows * share and blk % BF16_SUBLANES == 0 and row0 % blk == 0
    first = row0 // blk
    return (pl.BlockSpec((blk, n_cols), lambda *idx: (first + linear_step(*idx) // share, 0)),
            pl.BlockSpec((blk, n_cols), lambda *idx: (linear_step(*idx) // share, 0)))


def _cast_rows(srcs, dsts):
    for src, dst in zip(srcs, dsts):
        dst[...] = src[...].astype(dst.dtype)


def _rope_table_rows(pos_ref, freq_ref, cs_ref):
    ang = pos_ref[...] * freq_ref[...]
    cos, sin = jnp.cos(ang), jnp.sin(ang)
    n_rows = ang.shape[0]
    per_row = LANES // HALF_ROPE
    lane = lax.broadcasted_iota(jnp.int32, ang.shape, 1)

    def lanes_from(x, src, dst):
        shift = (dst - src) % LANES
        return pltpu.roll(x, shift, 1) if shift else x

    for k in range(per_row):
        src = k * HALF_ROPE
        row = jnp.where(
            lane < HALF_ROPE, lanes_from(cos, src, 0),
            jnp.where(lane < 2 * HALF_ROPE, lanes_from(cos, src, HALF_ROPE),
                      jnp.where(lane < 3 * HALF_ROPE, -lanes_from(sin, src, 2 * HALF_ROPE),
                                lanes_from(sin, src, 3 * HALF_ROPE))))
        cs_ref[pl.ds(k, n_rows, stride=per_row), :] = row


def _rope_table_inputs(positions):
    n_tok = positions.size
    per_row = LANES // HALF_ROPE
    inv_freq = ROPE_THETA ** (-jnp.arange(0, QK_ROPE_DIM, 2, dtype=F32) / QK_ROPE_DIM)
    pos_rep = jnp.repeat(positions.astype(F32).reshape(n_tok // per_row, per_row), HALF_ROPE, axis=1)
    return pos_rep, jnp.tile(inv_freq, per_row).reshape(1, LANES)


def _rope_dup(x, cs):
    y = x * cs
    return y + pltpu.roll(y, LANES // 2, 1)


def _rope_pair(p, cos4, sin4):
    lane = lax.broadcasted_iota(jnp.int32, p.shape, 1)
    first_half = lane % QK_ROPE_DIM < HALF_ROPE
    partner = jnp.where(first_half, pltpu.roll(p, LANES - HALF_ROPE, 1), pltpu.roll(p, HALF_ROPE, 1))
    return p * cos4 + partner * sin4


def _prep_kernel(pos_ref, freq_ref, wlat_ref, wuq_ref, wukv_ref,
                 cs_ref, olat_ref, ouq_ref, oukv_ref):
    _rope_table_rows(pos_ref, freq_ref, cs_ref)
    olat_ref[0:KPE_OFF] = wlat_ref[0:KPE_OFF].astype(BF16)
    x1 = wlat_ref[KPE_OFF:KPE_OFF + HALF_ROPE].astype(BF16)
    x2 = wlat_ref[KPE_OFF + HALF_ROPE:UV_OFF].astype(BF16)
    for k, part in enumerate((x1, x2, x2, x1)):
        olat_ref[KPE_OFF + k * HALF_ROPE:KPE_OFF + (k + 1) * HALF_ROPE] = part
    oukv_ref[...] = wukv_ref[...].astype(BF16)
    half = LANES // 2
    nope_cols = N_HEADS * QK_NOPE_DIM
    lane = lax.broadcasted_iota(jnp.int32, (wuq_ref.shape[0], LANES), 1)
    for pair in range(N_HEADS // 2):
        t0, t1, t2 = (wuq_ref[:, (3 * pair + k) * LANES:(3 * pair + k + 1) * LANES] for k in range(3))
        nope_odd = jnp.where(lane < half, pltpu.roll(t1, half, 1), pltpu.roll(t2, half, 1))
        for h, nope in ((2 * pair, t0), (2 * pair + 1, nope_odd)):
            ouq_ref[:, h * LANES:(h + 1) * LANES] = nope.astype(BF16)
        ouq_ref[:, nope_cols + pair * LANES:nope_cols + (pair + 1) * LANES] = (
            jnp.where(lane < half, t1, t2).astype(BF16))


def _prep_inproj(positions, w_in_t, w_uq, w_ukv):
    n_chunks = PREP_STEPS
    d = w_in_t.shape[1]
    r_q, r_kv = w_uq.shape[0], w_ukv.shape[0]
    lat_rows = UV_OFF + LANES - QK_ROPE_DIM
    uq_cols = N_HEADS * QK_NOPE_DIM + (N_HEADS // 2) * LANES
    pos_rep, freq = _rope_table_inputs(positions)
    n_tok = positions.size
    return pl.pallas_call(
        _prep_kernel,
        grid=(n_chunks,),
        in_specs=[
            pl.BlockSpec((pos_rep.shape[0] // n_chunks, LANES), lambda c: (c, 0)),
            _resident((1, LANES)),
            pl.BlockSpec((UV_OFF, d // n_chunks), lambda c: (0, c)),
            pl.BlockSpec((r_q // n_chunks, w_uq.shape[1]), lambda c: (c, 0)),
            pl.BlockSpec((r_kv // n_chunks, w_ukv.shape[1]), lambda c: (c, 0)),
        ],
        out_specs=[
            pl.BlockSpec((n_tok // n_chunks, LANES), lambda c: (c, 0)),
            pl.BlockSpec((lat_rows, d // n_chunks), lambda c: (0, c)),
            pl.BlockSpec((r_q // n_chunks, uq_cols), lambda c: (c, 0)),
            pl.BlockSpec((r_kv // n_chunks, w_ukv.shape[1]), lambda c: (c, 0)),
        ],
        out_shape=[
            jax.ShapeDtypeStruct((n_tok, LANES), F32),
            jax.ShapeDtypeStruct((lat_rows, d), BF16),
            jax.ShapeDtypeStruct((r_q, uq_cols), BF16),
            jax.ShapeDtypeStruct(w_ukv.shape, BF16),
        ],
        compiler_params=_params(1),
        name="prep_inproj",
    )(pos_rep, freq, w_in_t, w_uq, w_ukv)


def _inproj_kernel(x_ref, g_ref, cs_ref, wlat_ref, qg_ref, kvg_ref, wuq_ref, wukv_ref,
                   a_ref, qn_ref, qpe_ref, kn_ref, v_ref, kpe_ref, *, sub):
    def latents(r0):
        a = _rms(x_ref[0, r0:r0 + sub, :], g_ref[...]).astype(BF16)
        a_ref[0, r0:r0 + sub, :] = a
        return _dot_t(a, wlat_ref[...])

    starts = list(range(0, x_ref.shape[1], sub))
    for r0, z in zip(starts, [latents(r0) for r0 in starts]):
        rows = slice(r0, r0 + sub)
        qn = (_rms(z[:, :Q_LORA_RANK], qg_ref[...]) * QK_LOG2_SCALE).astype(BF16)
        kvn = _rms(z[:, Q_LORA_RANK:Q_LORA_RANK + KV_LORA_RANK], kvg_ref[...]).astype(BF16)
        cs = cs_ref[0, rows, :]
        kpe = _rope_dup(z[:, Q_LORA_RANK + KV_LORA_RANK:], cs)
        lane = lax.broadcasted_iota(jnp.int32, kpe.shape, 1)
        low = lane < QK_ROPE_DIM
        kpe_ref[0, rows, 0:LANES] = jnp.where(low, kpe, 0.0).astype(BF16)
        kpe_ref[0, rows, LANES:2 * LANES] = jnp.where(low, 0.0, kpe).astype(BF16)
        cs_swapped = pltpu.roll(cs, LANES // 2, 1)
        cos4 = jnp.where(low, cs, cs_swapped)
        sin4 = jnp.where(low, cs_swapped, cs)

        heads_per_dot = 4
        width = heads_per_dot * LANES
        nope_cols = N_HEADS * QK_NOPE_DIM
        for hg in range(N_HEADS // heads_per_dot):
            c0 = hg * width
            q_nope = _dot(qn, wuq_ref[:, c0:c0 + width])
            pe0 = nope_cols + c0 // 2
            q_pe = _dot(qn, wuq_ref[:, pe0:pe0 + width // 2])
            kv0 = 2 * c0
            kv_a = _dot(kvn, wukv_ref[:, kv0:kv0 + width])
            kv_b = _dot(kvn, wukv_ref[:, kv0 + width:kv0 + 2 * width])
            for pp in range(heads_per_dot // 2):
                pair = hg * (heads_per_dot // 2) + pp
                qpe_ref[0, pair, rows, :] = _rope_pair(
                    q_pe[:, pp * LANES:(pp + 1) * LANES], cos4, sin4).astype(BF16)
            for hh in range(heads_per_dot):
                h = hg * heads_per_dot + hh
                sl = slice(hh * LANES, (hh + 1) * LANES)
                qn_ref[0, h, rows, :] = q_nope[:, sl].astype(BF16)
                kv = kv_a if hh < heads_per_dot // 2 else kv_b
                k0 = (hh % (heads_per_dot // 2)) * 2 * LANES
                kn_ref[0, h, rows, :] = kv[:, k0:k0 + LANES].astype(BF16)
                v_ref[0, h, rows, :] = kv[:, k0 + LANES:k0 + 2 * LANES].astype(BF16)


def _inproj(x, norm_g, cs, w_lat_t, q_g, kv_g, w_uq, w_ukv, tm, sub):
    B, S, D = x.shape
    row = lambda b, i: (b, i, 0)
    head = lambda b, i: (b, 0, i, 0)
    return pl.pallas_call(
        functools.partial(_inproj_kernel, sub=sub),
        grid=(B, S // tm),
        in_specs=[
            pl.BlockSpec((1, tm, D), row),
            _resident((1, D)),
            pl.BlockSpec((1, tm, LANES), row),
            _resident(w_lat_t.shape),
            _resident((1, Q_LORA_RANK)),
            _resident((1, KV_LORA_RANK)),
            _resident(w_uq.shape),
            _resident(w_ukv.shape),
        ],
        out_specs=[
            pl.BlockSpec((1, tm, D), row),
            pl.BlockSpec((1, N_HEADS, tm, LANES), head),
            pl.BlockSpec((1, N_HEADS // 2, tm, LANES), head),
            pl.BlockSpec((1, N_HEADS, tm, LANES), head),
            pl.BlockSpec((1, N_HEADS, tm, LANES), head),
            pl.BlockSpec((1, tm, 2 * LANES), row),
        ],
        out_shape=[
            jax.ShapeDtypeStruct((B, S, D), BF16),
            jax.ShapeDtypeStruct((B, N_HEADS, S, LANES), BF16),
            jax.ShapeDtypeStruct((B, N_HEADS // 2, S, LANES), BF16),
            jax.ShapeDtypeStruct((B, N_HEADS, S, LANES), BF16),
            jax.ShapeDtypeStruct((B, N_HEADS, S, LANES), BF16),
            jax.ShapeDtypeStruct((B, S, 2 * LANES), BF16),
        ],
        compiler_params=_params(2),
        name="inproj",
    )(x, norm_g, cs, w_lat_t, q_g, kv_g, w_uq, w_ukv)


def _sgu_kernel(a_ref, wuv_ref, sg_ref, ws_ref, bfull_ref, wos_ref, wg1_ref, bg1_ref, cast_ref,
                m_ref, cast_out_ref, *, sub):
    _cast_rows([cast_ref], [cast_out_ref])
    n_chunks = sub // CHUNK
    t_idx = lax.broadcasted_iota(jnp.int32, (CHUNK, CHUNK), 0)
    s_idx = lax.broadcasted_iota(jnp.int32, (CHUNK, CHUNK), 1)
    causal = t_idx >= s_idx
    ws = [jnp.where(causal, ws_ref[g], 0.0).astype(BF16) for g in range(SGU_GROUPS)]
    bfull = bfull_ref[...]
    def gating_unit(uv_raw):
        uv = jax.nn.gelu(uv_raw)
        u = uv[:, :SGU_WIDTH]
        vn = _rms(uv[:, SGU_WIDTH:], sg_ref[...]).astype(BF16)
        mixed_cols = []
        for g in range(SGU_GROUPS):
            gs = slice(g * SGU_GROUP_DIM, (g + 1) * SGU_GROUP_DIM)
            rhs = jnp.concatenate(
                [vn[c * CHUNK:(c + 1) * CHUNK, gs] for c in range(n_chunks)], axis=1)
            mixed_cols.append(_dot(ws[g], rhs))
        rows = []
        for c in range(n_chunks):
            cs = slice(c * SGU_GROUP_DIM, (c + 1) * SGU_GROUP_DIM)
            mixed = jnp.concatenate([mixed_cols[g][:, cs] for g in range(SGU_GROUPS)], axis=1)
            rows.append(u[c * CHUNK:(c + 1) * CHUNK] * (mixed + bfull))
        return jnp.concatenate(rows, axis=0).astype(BF16)

    a = a_ref[...]
    uv_raw = _dot_t(a, wuv_ref[...])
    gate_raw = _dot_t(a, wg1_ref[...])
    for r0 in range(0, a_ref.shape[0], sub):
        rows = slice(r0, r0 + sub)
        y_sgu = _dot(gating_unit(uv_raw[rows]), wos_ref[...])
        m_ref[rows, :] = jax.nn.sigmoid(gate_raw[rows] + bg1_ref[...]) * y_sgu


def _sgu_branch(a, w_uv, sgu_g, w_s, b_full, w_o_sgu, w_g1, b_g1, cast_w, tm, sub):
    T, D = a.shape
    row = lambda i: (i, 0)
    cast_in, cast_out = _cast_block_specs(*cast_w.shape, 0, T // tm, lambda i: i)
    return pl.pallas_call(
        functools.partial(_sgu_kernel, sub=sub),
        grid=(T // tm,),
        in_specs=[
            pl.BlockSpec((tm, D), row),
            _resident(w_uv.shape),
            _resident(sgu_g.shape),
            _resident(w_s.shape),
            _resident(b_full.shape),
            _resident(w_o_sgu.shape),
            _resident(w_g1.shape),
            _resident(b_g1.shape),
            cast_in,
        ],
        out_specs=[pl.BlockSpec((tm, D), row), cast_out],
        out_shape=[jax.ShapeDtypeStruct((T, D), F32), jax.ShapeDtypeStruct(cast_w.shape, BF16)],
        compiler_params=_params(1),
        name="sgu_branch",
    )(a, w_uv, sgu_g, w_s, b_full, w_o_sgu, w_g1, b_g1, cast_w)


def _attn_kernel(qn_ref, qpe_ref, kn_ref, kpe_ref, v_ref, *rest, tq, n_cast):
    cast_in, (o_ref,), cast_out, (kf_ref, vf_ref) = (
        rest[:n_cast], rest[n_cast:n_cast + 1], rest[n_cast + 1:2 * n_cast + 1], rest[2 * n_cast + 1:])
    _cast_rows(cast_in, cast_out)
    seq = qn_ref.shape[2]
    row = lax.broadcasted_iota(jnp.int32, (tq, tq), 0)
    col = lax.broadcasted_iota(jnp.int32, (tq, tq), 1)
    causal = row >= col
    neg = jnp.finfo(F32).min
    heads = qn_ref.shape[1]
    for hh in range(heads):
        parity = hh % 2
        kf_ref[hh, :, 0:LANES] = kn_ref[0, hh]
        kf_ref[hh, :, LANES:2 * LANES] = kpe_ref[0, :, parity * LANES:(parity + 1) * LANES]
        vf_ref[hh, :, 0:LANES] = v_ref[0, hh]
        vf_ref[hh, :, LANES:2 * LANES] = jnp.ones((seq, LANES), vf_ref.dtype)

    nt = (((1,), (1,)), ((), ()))
    def scores(pair, hh):
        q0 = pair * 2 * tq
        k1, k2 = q0 + tq, q0 + 2 * tq
        q = jnp.concatenate([qn_ref[0, hh, q0:k2, :], qpe_ref[0, hh // 2, q0:k2, :]], axis=1)
        return (lax.dot_general(q, kf_ref[hh, 0:k1, :], nt, preferred_element_type=F32),
                lax.dot_general(q[tq:], kf_ref[hh, k1:k2, :], nt, preferred_element_type=F32))

    def finish(pair, hh, s_main, s_ext):
        q0 = pair * 2 * tq
        k1, k2 = q0 + tq, q0 + 2 * tq
        top = s_main[0:tq]
        top_diag = jnp.where(causal, top[:, q0:k1], neg)
        top = jnp.concatenate([top[:, 0:q0], top_diag], axis=1) if pair else top_diag
        bot = s_main[tq:]
        ext = jnp.where(causal, s_ext, neg)
        m_top = jnp.max(top, axis=-1, keepdims=True)
        m_bot = jnp.maximum(jnp.max(bot, axis=-1, keepdims=True),
                            jnp.max(ext, axis=-1, keepdims=True))
        p_main = jnp.concatenate([jnp.exp2(top - m_top), jnp.exp2(bot - m_bot)], axis=0)
        acc = _dot(p_main.astype(BF16), vf_ref[hh, 0:k1, :])
        acc_bot = acc[tq:] + _dot(jnp.exp2(ext - m_bot).astype(BF16), vf_ref[hh, k1:k2, :])
        lanes = slice(hh * LANES, (hh + 1) * LANES)
        o_ref[0, q0:k1, lanes] = (acc[0:tq, 0:LANES] / acc[0:tq, LANES:]).astype(o_ref.dtype)
        o_ref[0, k1:k2, lanes] = (acc_bot[:, 0:LANES] / acc_bot[:, LANES:]).astype(o_ref.dtype)

    work = [(pair, hh) for pair in reversed(range(seq // (2 * tq))) for hh in range(heads)]
    lead = 2
    pending = [scores(*w) for w in work[:lead]]
    for idx, (pair, hh) in enumerate(work):
        if idx + lead < len(work):
            pending.append(scores(*work[idx + lead]))
        finish(pair, hh, *pending.pop(0))


def _attention(q_nope, q_pe, k_nope, k_pe, v, casts, tq, hb):
    B, H, S, _ = q_nope.shape
    assert hb % 2 == 0, "head pairs share a rope tile"
    groups = H // hb
    specs = [_cast_block_specs(n, w.shape[1], r0, B * groups, lambda b, g: b * groups + g)
             for w, r0, n in casts]
    outs = pl.pallas_call(
        functools.partial(_attn_kernel, tq=tq, n_cast=len(casts)),
        grid=(B, groups),
        in_specs=[
            pl.BlockSpec((1, hb, S, LANES), lambda b, g: (b, g, 0, 0)),
            pl.BlockSpec((1, hb // 2, S, LANES), lambda b, g: (b, g, 0, 0)),
            pl.BlockSpec((1, hb, S, LANES), lambda b, g: (b, g, 0, 0)),
            pl.BlockSpec((1, S, 2 * LANES), lambda b, g: (b, 0, 0)),
            pl.BlockSpec((1, hb, S, LANES), lambda b, g: (b, g, 0, 0)),
        ] + [s_in for s_in, _ in specs],
        out_specs=[pl.BlockSpec((1, S, hb * LANES), lambda b, g: (b, 0, g))]
        + [s_out for _, s_out in specs],
        out_shape=[jax.ShapeDtypeStruct((B, S, H * V_HEAD_DIM), BF16)]
        + [jax.ShapeDtypeStruct((n, w.shape[1]), BF16) for w, _, n in casts],
        scratch_shapes=[pltpu.VMEM((hb, S, 2 * LANES), BF16), pltpu.VMEM((hb, S, 2 * LANES), BF16)],
        compiler_params=_params(2),
        name="mla_attention",
    )(q_nope, q_pe, k_nope, k_pe, v, *[w for w, _, _ in casts])
    return outs[0], outs[1:]


def _merge_kernel(attn_ref, a_ref, m_ref, x_ref, woa_ref, wg0_ref, bg0_ref, wout_ref, fg_ref,
                  cast_ref, h_ref, f_ref, cast_out_ref):
    _cast_rows([cast_ref], [cast_out_ref])
    y_attn = _dot(attn_ref[...], woa_ref[...])
    gate = jax.nn.sigmoid(_dot_t(a_ref[...], wg0_ref[...]) + bg0_ref[...])
    merged = (gate * y_attn + m_ref[...]).astype(BF16)
    h = x_ref[...] + _dot(merged, wout_ref[...])
    h_ref[...] = h
    f_ref[...] = _rms(h, fg_ref[...]).astype(BF16)


def _merge(attn, a, m_sgu, x, w_o_attn, w_g0, b_g0, w_out, ffn_g, cast_w, tm):
    T, D = x.shape
    row = lambda i: (i, 0)
    tile = pl.BlockSpec((tm, D), row)
    cast_in, cast_out = _cast_block_specs(*cast_w.shape, 0, T // tm, lambda i: i)
    return pl.pallas_call(
        _merge_kernel,
        grid=(T // tm,),
        in_specs=[tile, tile, tile, tile,
                  _resident(w_o_attn.shape), _resident(w_g0.shape), _resident(b_g0.shape),
                  _resident(w_out.shape), _resident(ffn_g.shape), cast_in],
        out_specs=[tile, tile, cast_out],
        out_shape=[jax.ShapeDtypeStruct((T, D), F32), jax.ShapeDtypeStruct((T, D), BF16),
                   jax.ShapeDtypeStruct(cast_w.shape, BF16)],
        compiler_params=_params(1),
        name="merge_outproj",
    )(attn, a, m_sgu, x, w_o_attn, w_g0, b_g0, w_out, ffn_g, cast_w)


def _ffn_kernel(f_ref, h_ref, wg_ref, wu_ref, wd_ref, ng_ref, o_ref, *, sub):
    j = pl.program_id(1)

    f = f_ref[...]
    starts = list(range(0, wg_ref.shape[1], sub))
    projected = [(_dot(f, wg_ref[:, c0:c0 + sub]), _dot(f, wu_ref[:, c0:c0 + sub]))
                 for c0 in starts]
    act = jnp.concatenate(
        [(jax.nn.silu(gate) * up).astype(BF16) for gate, up in projected], axis=1)
    o_ref[...] = jnp.where(j == 0, h_ref[...], o_ref[...]) + _dot(act, wd_ref[...])

    @pl.when(j == pl.num_programs(1) - 1)
    def _():
        o_ref[...] = _rms(o_ref[...], ng_ref[...])


def _ffn(f, h, w_gate, w_up, w_down, final_g, tm, tf, sub):
    T, D = h.shape
    d_ff = w_gate.shape[1]
    row = lambda i, j: (i, 0)
    n_i, n_j = T // tm, d_ff // tf
    h_row = lambda i, j: (jnp.minimum(i + (j >= n_j // 2).astype(jnp.int32), n_i - 1), 0)
    return pl.pallas_call(
        functools.partial(_ffn_kernel, sub=sub),
        grid=(n_i, n_j),
        in_specs=[
            pl.BlockSpec((tm, D), row),
            pl.BlockSpec((tm, D), h_row),
            pl.BlockSpec((D, tf), lambda i, j: (0, j)),
            pl.BlockSpec((D, tf), lambda i, j: (0, j)),
            pl.BlockSpec((tf, D), lambda i, j: (j, 0)),
            _resident((1, D)),
        ],
        out_specs=pl.BlockSpec((tm, D), row),
        out_shape=jax.ShapeDtypeStruct((T, D), F32),
        compiler_params=_params(2),
        name="swiglu_ffn",
    )(f, h, w_gate, w_up, w_down, final_g)


def kernel(x, positions, norm_mix_g, w_in, b_gate, q_norm_g, w_uq, kv_norm_g, w_ukv, w_o_attn,
           sgu_norm_g, w_sgu, b_sgu, w_o_sgu, w_out, norm_ffn_g, w_gate_ffn, w_up_ffn,
           w_down_ffn, norm_final_g):
    B, S, D = x.shape
    T = B * S
    depth = w_in.shape[0]
    assert depth == 1, "the final norm is fused into the FFN epilogue of a single layer"
    assert w_in.shape[1:] == (D, D_IN)

    row_vec = lambda v: v.reshape(1, -1).astype(F32)

    h = x
    out = None
    for l in range(depth):
        w_in_t = jnp.swapaxes(w_in[l], 0, 1)
        cs, w_lat_t, w_uq_p, w_ukv_p = _prep_inproj(positions, w_in_t, w_uq[l], w_ukv[l])
        cs = cs.reshape(B, S, LANES)
        b_full = jnp.repeat(b_sgu[l].T, SGU_GROUP_DIM, axis=1).astype(F32)

        a, q_nope, q_pe, k_nope, v, k_pe = _inproj(
            h, row_vec(norm_mix_g[l]), cs, w_lat_t, row_vec(q_norm_g[l]), row_vec(kv_norm_g[l]),
            w_uq_p, w_ukv_p, tm=INPROJ_ROWS, sub=ROW_SUB)
        a2 = a.reshape(T, D)
        whole = lambda w: (w, 0, w.shape[0])
        attn, (w_uv_t, w_g0_t, w_g1_t, w_os, w_oa, w_o, w_uf) = _attention(
            q_nope, q_pe, k_nope, k_pe, v,
            [(w_in_t, UV_OFF, 2 * SGU_WIDTH), (w_in_t, GATE_OFF, D), (w_in_t, GATE_OFF + D, D),
             whole(w_o_sgu[l]), whole(w_o_attn[l]), whole(w_out[l]), whole(w_up_ffn[l])],
            tq=ATTN_Q_ROWS, hb=ATTN_HEADS_PER_STEP)
        m_sgu, w_gf = _sgu_branch(
            a2, w_uv_t, row_vec(sgu_norm_g[l]), w_sgu[l], b_full, w_os, w_g1_t,
            row_vec(b_gate[l, D:]), w_gate_ffn[l], tm=SGU_ROWS, sub=ROW_SUB)
        h_mid, f, w_df = _merge(
            attn.reshape(T, D), a2, m_sgu, h.reshape(T, D), w_oa, w_g0_t,
            row_vec(b_gate[l, :D]), w_o, row_vec(norm_ffn_g[l]), w_down_ffn[l], tm=MERGE_ROWS)
        out = _ffn(f, h_mid, w_gf, w_uf, w_df, row_vec(norm_final_g), tm=FFN_ROWS,
                   tf=FFN_COLS, sub=FFN_COL_SUB)
        h = out.reshape(B, S, D)
    return h
```

```python
import functools

import jax
import jax.numpy as jnp
from jax import lax
from jax.experimental import pallas as pl
from jax.experimental.pallas import tpu as pltpu

D_MODEL = 2048
N_HEADS = 16
QK_NOPE_DIM = 128
QK_ROPE_DIM = 64
V_HEAD_DIM = 128
Q_LORA_RANK = 512
KV_LORA_RANK = 512
ROPE_THETA = 10000.0
SGU_GROUPS = 8
SGU_GROUP_DIM = 128
SGU_WIDTH = SGU_GROUPS * SGU_GROUP_DIM
CHUNK = 128
N_BRANCH = 2
RMS_EPS = 1e-6
KPE_OFF = Q_LORA_RANK + KV_LORA_RANK
UV_OFF = KPE_OFF + QK_ROPE_DIM
GATE_OFF = UV_OFF + 2 * SGU_WIDTH
D_IN = GATE_OFF + N_BRANCH * D_MODEL
LANES = 128
HALF_ROPE = QK_ROPE_DIM // 2
LOG2_E = 1.4426950408889634
QK_LOG2_SCALE = (QK_NOPE_DIM + QK_ROPE_DIM) ** -0.5 * LOG2_E
BF16_SUBLANES = 16

VMEM_LIMIT_BYTES = 60 * 1024 * 1024

INPROJ_ROWS = 512
SGU_ROWS = 512
MERGE_ROWS = 256
ROW_SUB = 256
ATTN_Q_ROWS = 256
ATTN_HEADS_PER_STEP = 2
FFN_ROWS = 1024
FFN_COLS = 512
FFN_COL_SUB = 256
PREP_STEPS = 4

F32 = jnp.float32
BF16 = jnp.bfloat16


def _rms(x, g):
    return x * lax.rsqrt(jnp.mean(x * x, axis=-1, keepdims=True) + RMS_EPS) * g


def _dot(a, b):
    return jnp.dot(a, b, preferred_element_type=F32)


def _dot_t(a, b_t):
    return lax.dot_general(a, b_t, (((1,), (1,)), ((), ())), preferred_element_type=F32)


def _resident(shape):
    return pl.BlockSpec(shape, lambda *_: (0,) * len(shape), pipeline_mode=pl.Buffered(1))


def _params(n_axes):
    return pltpu.CompilerParams(
        dimension_semantics=("arbitrary",) * n_axes, vmem_limit_bytes=VMEM_LIMIT_BYTES)


def _cast_block_specs(n_rows, n_cols, row0, n_steps, linear_step, col_tile=None):
    share = 1 if (n_rows // n_steps) % BF16_SUBLANES == 0 else 2
    blk = n_rows * share // n_steps
    assert blk * n_steps == n_rows * share and blk % BF16_SUBLANES == 0 and row0 % blk == 0
    first = row0 // blk
    in_spec = pl.BlockSpec((blk, n_cols), lambda *idx: (first + linear_step(*idx) // share, 0))
    if col_tile is None:
        return (in_spec, pl.BlockSpec((blk, n_cols), lambda *idx: (linear_step(*idx) // share, 0)),
                jax.ShapeDtypeStruct((n_rows, n_cols), BF16))
    n_tiles = n_cols // col_tile
    return (in_spec,
            pl.BlockSpec((n_tiles, blk, col_tile), lambda *idx: (0, linear_step(*idx) // share, 0)),
            jax.ShapeDtypeStruct((n_tiles, n_rows, col_tile), BF16))


def _cast_rows(srcs, dsts):
    for src, dst in zip(srcs, dsts):
        if len(dst.shape) == 2:
            dst[...] = src[...].astype(dst.dtype)
        else:
            width = dst.shape[2]
            for t in range(dst.shape[0]):
                dst[t] = src[:, t * width:(t + 1) * width].astype(dst.dtype)


def _rope_table_rows(pos_ref, freq_ref, cs_ref):
    ang = pos_ref[...] * freq_ref[...]
    cos, sin = jnp.cos(ang), jnp.sin(ang)
    n_rows = ang.shape[0]
    per_row = LANES // HALF_ROPE
    lane = lax.broadcasted_iota(jnp.int32, ang.shape, 1)

    def lanes_from(x, src, dst):
        shift = (dst - src) % LANES
        return pltpu.roll(x, shift, 1) if shift else x

    for k in range(per_row):
        src = k * HALF_ROPE
        row = jnp.where(
            lane < HALF_ROPE, lanes_from(cos, src, 0),
            jnp.where(lane < 2 * HALF_ROPE, lanes_from(cos, src, HALF_ROPE),
                      jnp.where(lane < 3 * HALF_ROPE, -lanes_from(sin, src, 2 * HALF_ROPE),
                                lanes_from(sin, src, 3 * HALF_ROPE))))
        cs_ref[pl.ds(k, n_rows, stride=per_row), :] = row


def _rope_table_inputs(positions):
    n_tok = positions.size
    per_row = LANES // HALF_ROPE
    inv_freq = ROPE_THETA ** (-jnp.arange(0, QK_ROPE_DIM, 2, dtype=F32) / QK_ROPE_DIM)
    pos_rep = jnp.repeat(positions.astype(F32).reshape(n_tok // per_row, per_row), HALF_ROPE, axis=1)
    return pos_rep, jnp.tile(inv_freq, per_row).reshape(1, LANES)


def _rope_dup(x, cs):
    y = x * cs
    return y + pltpu.roll(y, LANES // 2, 1)


def _rope_pair(p, cos4, sin4):
    lane = lax.broadcasted_iota(jnp.int32, p.shape, 1)
    first_half = lane % QK_ROPE_DIM < HALF_ROPE
    partner = jnp.where(first_half, pltpu.roll(p, LANES - HALF_ROPE, 1), pltpu.roll(p, HALF_ROPE, 1))
    return p * cos4 + partner * sin4


def _prep_kernel(pos_ref, freq_ref, wlat_ref, wuq_ref, wukv_ref,
                 cs_ref, olat_ref, ouq_ref, oukv_ref):
    _rope_table_rows(pos_ref, freq_ref, cs_ref)
    olat_ref[0:KPE_OFF] = wlat_ref[0:KPE_OFF].astype(BF16)
    x1 = wlat_ref[KPE_OFF:KPE_OFF + HALF_ROPE].astype(BF16)
    x2 = wlat_ref[KPE_OFF + HALF_ROPE:UV_OFF].astype(BF16)
    for k, part in enumerate((x1, x2, x2, x1)):
        olat_ref[KPE_OFF + k * HALF_ROPE:KPE_OFF + (k + 1) * HALF_ROPE] = part
    oukv_ref[...] = wukv_ref[...].astype(BF16)
    half = LANES // 2
    nope_cols = N_HEADS * QK_NOPE_DIM
    lane = lax.broadcasted_iota(jnp.int32, (wuq_ref.shape[0], LANES), 1)
    for pair in range(N_HEADS // 2):
        t0, t1, t2 = (wuq_ref[:, (3 * pair + k) * LANES:(3 * pair + k + 1) * LANES] for k in range(3))
        nope_odd = jnp.where(lane < half, pltpu.roll(t1, half, 1), pltpu.roll(t2, half, 1))
        for h, nope in ((2 * pair, t0), (2 * pair + 1, nope_odd)):
            ouq_ref[:, h * LANES:(h + 1) * LANES] = nope.astype(BF16)
        ouq_ref[:, nope_cols + pair * LANES:nope_cols + (pair + 1) * LANES] = (
            jnp.where(lane < half, t1, t2).astype(BF16))


def _prep_inproj(positions, w_in_t, w_uq, w_ukv):
    n_chunks = PREP_STEPS
    d = w_in_t.shape[1]
    r_q, r_kv = w_uq.shape[0], w_ukv.shape[0]
    lat_rows = UV_OFF + LANES - QK_ROPE_DIM
    uq_cols = N_HEADS * QK_NOPE_DIM + (N_HEADS // 2) * LANES
    pos_rep, freq = _rope_table_inputs(positions)
    n_tok = positions.size
    return pl.pallas_call(
        _prep_kernel,
        grid=(n_chunks,),
        in_specs=[
            pl.BlockSpec((pos_rep.shape[0] // n_chunks, LANES), lambda c: (c, 0)),
            _resident((1, LANES)),
            pl.BlockSpec((UV_OFF, d // n_chunks), lambda c: (0, c)),
            pl.BlockSpec((r_q // n_chunks, w_uq.shape[1]), lambda c: (c, 0)),
            pl.BlockSpec((r_kv // n_chunks, w_ukv.shape[1]), lambda c: (c, 0)),
        ],
        out_specs=[
            pl.BlockSpec((n_tok // n_chunks, LANES), lambda c: (c, 0)),
            pl.BlockSpec((lat_rows, d // n_chunks), lambda c: (0, c)),
            pl.BlockSpec((r_q // n_chunks, uq_cols), lambda c: (c, 0)),
            pl.BlockSpec((r_kv // n_chunks, w_ukv.shape[1]), lambda c: (c, 0)),
        ],
        out_shape=[
            jax.ShapeDtypeStruct((n_tok, LANES), F32),
            jax.ShapeDtypeStruct((lat_rows, d), BF16),
            jax.ShapeDtypeStruct((r_q, uq_cols), BF16),
            jax.ShapeDtypeStruct(w_ukv.shape, BF16),
        ],
        compiler_params=_params(1),
        name="prep_inproj",
    )(pos_rep, freq, w_in_t, w_uq, w_ukv)


def _inproj_kernel(x_ref, g_ref, cs_ref, wlat_ref, qg_ref, kvg_ref, wuq_ref, wukv_ref,
                   a_ref, qn_ref, qpe_ref, kn_ref, v_ref, kpe_ref, *, sub):
    def latents(r0):
        a = _rms(x_ref[0, r0:r0 + sub, :], g_ref[...]).astype(BF16)
        a_ref[0, r0:r0 + sub, :] = a
        return _dot_t(a, wlat_ref[...])

    starts = list(range(0, x_ref.shape[1], sub))
    for r0, z in zip(starts, [latents(r0) for r0 in starts]):
        rows = slice(r0, r0 + sub)
        qn = (_rms(z[:, :Q_LORA_RANK], qg_ref[...]) * QK_LOG2_SCALE).astype(BF16)
        kvn = _rms(z[:, Q_LORA_RANK:Q_LORA_RANK + KV_LORA_RANK], kvg_ref[...]).astype(BF16)
        cs = cs_ref[0, rows, :]
        kpe = _rope_dup(z[:, Q_LORA_RANK + KV_LORA_RANK:], cs)
        lane = lax.broadcasted_iota(jnp.int32, kpe.shape, 1)
        low = lane < QK_ROPE_DIM
        kpe_ref[0, rows, 0:LANES] = jnp.where(low, kpe, 0.0).astype(BF16)
        kpe_ref[0, rows, LANES:2 * LANES] = jnp.where(low, 0.0, kpe).astype(BF16)
        cs_swapped = pltpu.roll(cs, LANES // 2, 1)
        cos4 = jnp.where(low, cs, cs_swapped)
        sin4 = jnp.where(low, cs_swapped, cs)

        heads_per_dot = 4
        width = heads_per_dot * LANES
        nope_cols = N_HEADS * QK_NOPE_DIM
        for hg in range(N_HEADS // heads_per_dot):
            c0 = hg * width
            q_nope = _dot(qn, wuq_ref[:, c0:c0 + width])
            pe0 = nope_cols + c0 // 2
            q_pe = _dot(qn, wuq_ref[:, pe0:pe0 + width // 2])
            kv0 = 2 * c0
            kv_a = _dot(kvn, wukv_ref[:, kv0:kv0 + width])
            kv_b = _dot(kvn, wukv_ref[:, kv0 + width:kv0 + 2 * width])
            for pp in range(heads_per_dot // 2):
                pair = hg * (heads_per_dot // 2) + pp
                qpe_ref[0, pair, rows, :] = _rope_pair(
                    q_pe[:, pp * LANES:(pp + 1) * LANES], cos4, sin4).astype(BF16)
            for hh in range(heads_per_dot):
                h = hg * heads_per_dot + hh
                sl = slice(hh * LANES, (hh + 1) * LANES)
                qn_ref[0, h, rows, :] = q_nope[:, sl].astype(BF16)
                kv = kv_a if hh < heads_per_dot // 2 else kv_b
                k0 = (hh % (heads_per_dot // 2)) * 2 * LANES
                kn_ref[0, h, rows, :] = kv[:, k0:k0 + LANES].astype(BF16)
                v_ref[0, h, rows, :] = kv[:, k0 + LANES:k0 + 2 * LANES].astype(BF16)


def _inproj(x, norm_g, cs, w_lat_t, q_g, kv_g, w_uq, w_ukv, tm, sub):
    B, S, D = x.shape
    row = lambda b, i: (b, i, 0)
    head = lambda b, i: (b, 0, i, 0)
    return pl.pallas_call(
        functools.partial(_inproj_kernel, sub=sub),
        grid=(B, S // tm),
        in_specs=[
            pl.BlockSpec((1, tm, D), row),
            _resident((1, D)),
            pl.BlockSpec((1, tm, LANES), row),
            _resident(w_lat_t.shape),
            _resident((1, Q_LORA_RANK)),
            _resident((1, KV_LORA_RANK)),
            _resident(w_uq.shape),
            _resident(w_ukv.shape),
        ],
        out_specs=[
            pl.BlockSpec((1, tm, D), row),
            pl.BlockSpec((1, N_HEADS, tm, LANES), head),
            pl.BlockSpec((1, N_HEADS // 2, tm, LANES), head),
            pl.BlockSpec((1, N_HEADS, tm, LANES), head),
            pl.BlockSpec((1, N_HEADS, tm, LANES), head),
            pl.BlockSpec((1, tm, 2 * LANES), row),
        ],
        out_shape=[
            jax.ShapeDtypeStruct((B, S, D), BF16),
            jax.ShapeDtypeStruct((B, N_HEADS, S, LANES), BF16),
            jax.ShapeDtypeStruct((B, N_HEADS // 2, S, LANES), BF16),
            jax.ShapeDtypeStruct((B, N_HEADS, S, LANES), BF16),
            jax.ShapeDtypeStruct((B, N_HEADS, S, LANES), BF16),
            jax.ShapeDtypeStruct((B, S, 2 * LANES), BF16),
        ],
        compiler_params=_params(2),
        name="inproj",
    )(x, norm_g, cs, w_lat_t, q_g, kv_g, w_uq, w_ukv)


def _sgu_kernel(a_ref, wuv_ref, sg_ref, ws_ref, bfull_ref, wos_ref, wg1_ref, bg1_ref, cast_ref,
                m_ref, cast_out_ref, *, sub):
    _cast_rows([cast_ref], [cast_out_ref])
    n_chunks = sub // CHUNK
    t_idx = lax.broadcasted_iota(jnp.int32, (CHUNK, CHUNK), 0)
    s_idx = lax.broadcasted_iota(jnp.int32, (CHUNK, CHUNK), 1)
    causal = t_idx >= s_idx
    ws = [jnp.where(causal, ws_ref[g], 0.0).astype(BF16) for g in range(SGU_GROUPS)]
    bfull = bfull_ref[...]
    def gating_unit(uv_raw):
        uv = jax.nn.gelu(uv_raw)
        u = uv[:, :SGU_WIDTH]
        vn = _rms(uv[:, SGU_WIDTH:], sg_ref[...]).astype(BF16)
        mixed_cols = []
        for g in range(SGU_GROUPS):
            gs = slice(g * SGU_GROUP_DIM, (g + 1) * SGU_GROUP_DIM)
            rhs = jnp.concatenate(
                [vn[c * CHUNK:(c + 1) * CHUNK, gs] for c in range(n_chunks)], axis=1)
            mixed_cols.append(_dot(ws[g], rhs))
        rows = []
        for c in range(n_chunks):
            cs = slice(c * SGU_GROUP_DIM, (c + 1) * SGU_GROUP_DIM)
            mixed = jnp.concatenate([mixed_cols[g][:, cs] for g in range(SGU_GROUPS)], axis=1)
            rows.append(u[c * CHUNK:(c + 1) * CHUNK] * (mixed + bfull))
        return jnp.concatenate(rows, axis=0).astype(BF16)

    a = a_ref[...]
    uv_raw = _dot_t(a, wuv_ref[...])
    gate_raw = _dot_t(a, wg1_ref[...])
    for r0 in range(0, a_ref.shape[0], sub):
        rows = slice(r0, r0 + sub)
        y_sgu = _dot(gating_unit(uv_raw[rows]), wos_ref[...])
        m_ref[rows, :] = jax.nn.sigmoid(gate_raw[rows] + bg1_ref[...]) * y_sgu


def _sgu_branch(a, w_uv, sgu_g, w_s, b_full, w_o_sgu, w_g1, b_g1, cast_w, cast_col_tile, tm, sub):
    T, D = a.shape
    row = lambda i: (i, 0)
    cast_in, cast_out, cast_shape = _cast_block_specs(
        *cast_w.shape, 0, T // tm, lambda i: i, col_tile=cast_col_tile)
    return pl.pallas_call(
        functools.partial(_sgu_kernel, sub=sub),
        grid=(T // tm,),
        in_specs=[
            pl.BlockSpec((tm, D), row),
            _resident(w_uv.shape),
            _resident(sgu_g.shape),
            _resident(w_s.shape),
            _resident(b_full.shape),
            _resident(w_o_sgu.shape),
            _resident(w_g1.shape),
            _resident(b_g1.shape),
            cast_in,
        ],
        out_specs=[pl.BlockSpec((tm, D), row), cast_out],
        out_shape=[jax.ShapeDtypeStruct((T, D), F32), cast_shape],
        compiler_params=_params(1),
        name="sgu_branch",
    )(a, w_uv, sgu_g, w_s, b_full, w_o_sgu, w_g1, b_g1, cast_w)


def _attn_kernel(qn_ref, qpe_ref, kn_ref, kpe_ref, v_ref, *rest, tq, n_cast):
    cast_in, (o_ref,), cast_out, (kf_ref, vf_ref) = (
        rest[:n_cast], rest[n_cast:n_cast + 1], rest[n_cast + 1:2 * n_cast + 1], rest[2 * n_cast + 1:])
    _cast_rows(cast_in, cast_out)
    seq = qn_ref.shape[2]
    row = lax.broadcasted_iota(jnp.int32, (tq, tq), 0)
    col = lax.broadcasted_iota(jnp.int32, (tq, tq), 1)
    causal = row >= col
    neg = jnp.finfo(F32).min
    heads = qn_ref.shape[1]
    for hh in range(heads):
        parity = hh % 2
        kf_ref[hh, :, 0:LANES] = kn_ref[0, hh]
        kf_ref[hh, :, LANES:2 * LANES] = kpe_ref[0, :, parity * LANES:(parity + 1) * LANES]
        vf_ref[hh, :, 0:LANES] = v_ref[0, hh]
        vf_ref[hh, :, LANES:2 * LANES] = jnp.ones((seq, LANES), vf_ref.dtype)

    nt = (((1,), (1,)), ((), ()))
    def scores(pair, hh):
        q0 = pair * 2 * tq
        k1, k2 = q0 + tq, q0 + 2 * tq
        q = jnp.concatenate([qn_ref[0, hh, q0:k2, :], qpe_ref[0, hh // 2, q0:k2, :]], axis=1)
        return (lax.dot_general(q, kf_ref[hh, 0:k1, :], nt, preferred_element_type=F32),
                lax.dot_general(q[tq:], kf_ref[hh, k1:k2, :], nt, preferred_element_type=F32))

    def finish(pair, hh, s_main, s_ext):
        q0 = pair * 2 * tq
        k1, k2 = q0 + tq, q0 + 2 * tq
        top = s_main[0:tq]
        top_diag = jnp.where(causal, top[:, q0:k1], neg)
        top = jnp.concatenate([top[:, 0:q0], top_diag], axis=1) if pair else top_diag
        bot = s_main[tq:]
        ext = jnp.where(causal, s_ext, neg)
        m_top = jnp.max(top, axis=-1, keepdims=True)
        m_bot = jnp.maximum(jnp.max(bot, axis=-1, keepdims=True),
                            jnp.max(ext, axis=-1, keepdims=True))
        p_main = jnp.concatenate([jnp.exp2(top - m_top), jnp.exp2(bot - m_bot)], axis=0)
        acc = _dot(p_main.astype(BF16), vf_ref[hh, 0:k1, :])
        acc_bot = acc[tq:] + _dot(jnp.exp2(ext - m_bot).astype(BF16), vf_ref[hh, k1:k2, :])
        lanes = slice(hh * LANES, (hh + 1) * LANES)
        o_ref[0, q0:k1, lanes] = (acc[0:tq, 0:LANES] / acc[0:tq, LANES:]).astype(o_ref.dtype)
        o_ref[0, k1:k2, lanes] = (acc_bot[:, 0:LANES] / acc_bot[:, LANES:]).astype(o_ref.dtype)

    work = [(pair, hh) for pair in reversed(range(seq // (2 * tq))) for hh in range(heads)]
    lead = 2
    pending = [scores(*w) for w in work[:lead]]
    for idx, (pair, hh) in enumerate(work):
        if idx + lead < len(work):
            pending.append(scores(*work[idx + lead]))
        finish(pair, hh, *pending.pop(0))


def _attention(q_nope, q_pe, k_nope, k_pe, v, casts, tq, hb):
    B, H, S, _ = q_nope.shape
    assert hb % 2 == 0, "head pairs share a rope tile"
    groups = H // hb
    specs = [_cast_block_specs(n, w.shape[1], r0, B * groups, lambda b, g: b * groups + g, ct)
             for w, r0, n, ct in casts]
    outs = pl.pallas_call(
        functools.partial(_attn_kernel, tq=tq, n_cast=len(casts)),
        grid=(B, groups),
        in_specs=[
            pl.BlockSpec((1, hb, S, LANES), lambda b, g: (b, g, 0, 0)),
            pl.BlockSpec((1, hb // 2, S, LANES), lambda b, g: (b, g, 0, 0)),
            pl.BlockSpec((1, hb, S, LANES), lambda b, g: (b, g, 0, 0)),
            pl.BlockSpec((1, S, 2 * LANES), lambda b, g: (b, 0, 0)),
            pl.BlockSpec((1, hb, S, LANES), lambda b, g: (b, g, 0, 0)),
        ] + [s_in for s_in, _, _ in specs],
        out_specs=[pl.BlockSpec((1, S, hb * LANES), lambda b, g: (b, 0, g))]
        + [s_out for _, s_out, _ in specs],
        out_shape=[jax.ShapeDtypeStruct((B, S, H * V_HEAD_DIM), BF16)]
        + [shape for _, _, shape in specs],
        scratch_shapes=[pltpu.VMEM((hb, S, 2 * LANES), BF16), pltpu.VMEM((hb, S, 2 * LANES), BF16)],
        compiler_params=_params(2),
        name="mla_attention",
    )(q_nope, q_pe, k_nope, k_pe, v, *[w for w, _, _, _ in casts])
    return outs[0], outs[1:]


def _merge_kernel(attn_ref, a_ref, m_ref, x_ref, woa_ref, wg0_ref, bg0_ref, wout_ref, fg_ref,
                  cast_ref, h_ref, f_ref, cast_out_ref):
    _cast_rows([cast_ref], [cast_out_ref])
    y_attn = _dot(attn_ref[...], woa_ref[...])
    gate = jax.nn.sigmoid(_dot_t(a_ref[...], wg0_ref[...]) + bg0_ref[...])
    merged = (gate * y_attn + m_ref[...]).astype(BF16)
    h = x_ref[...] + _dot(merged, wout_ref[...])
    h_ref[...] = h
    f_ref[...] = _rms(h, fg_ref[...]).astype(BF16)


def _merge(attn, a, m_sgu, x, w_o_attn, w_g0, b_g0, w_out, ffn_g, cast_w, tm):
    T, D = x.shape
    row = lambda i: (i, 0)
    tile = pl.BlockSpec((tm, D), row)
    cast_in, cast_out, cast_shape = _cast_block_specs(*cast_w.shape, 0, T // tm, lambda i: i)
    return pl.pallas_call(
        _merge_kernel,
        grid=(T // tm,),
        in_specs=[tile, tile, tile, tile,
                  _resident(w_o_attn.shape), _resident(w_g0.shape), _resident(b_g0.shape),
                  _resident(w_out.shape), _resident(ffn_g.shape), cast_in],
        out_specs=[tile, tile, cast_out],
        out_shape=[jax.ShapeDtypeStruct((T, D), F32), jax.ShapeDtypeStruct((T, D), BF16),
                   cast_shape],
        compiler_params=_params(1),
        name="merge_outproj",
    )(attn, a, m_sgu, x, w_o_attn, w_g0, b_g0, w_out, ffn_g, cast_w)


def _ffn_kernel(f_ref, h_ref, wg_ref, wu_ref, wd_ref, ng_ref, o_ref, *, sub):
    j = pl.program_id(1)

    f = f_ref[...]
    starts = list(range(0, wg_ref.shape[2], sub))
    projected = [(_dot(f, wg_ref[0, :, c0:c0 + sub]), _dot(f, wu_ref[0, :, c0:c0 + sub]))
                 for c0 in starts]
    act = jnp.concatenate(
        [(jax.nn.silu(gate) * up).astype(BF16) for gate, up in projected], axis=1)
    o_ref[...] = jnp.where(j == 0, h_ref[...], o_ref[...]) + _dot(act, wd_ref[...])

    @pl.when(j == pl.num_programs(1) - 1)
    def _():
        o_ref[...] = _rms(o_ref[...], ng_ref[...])


def _ffn(f, h, w_gate, w_up, w_down, final_g, tm, sub):
    T, D = h.shape
    n_j, _, tf = w_gate.shape
    row = lambda i, j: (i, 0)
    n_i = T // tm
    h_row = lambda i, j: (jnp.minimum(i + (j >= n_j // 2).astype(jnp.int32), n_i - 1), 0)
    return pl.pallas_call(
        functools.partial(_ffn_kernel, sub=sub),
        grid=(n_i, n_j),
        in_specs=[
            pl.BlockSpec((tm, D), row),
            pl.BlockSpec((tm, D), h_row),
            pl.BlockSpec((1, D, tf), lambda i, j: (j, 0, 0)),
            pl.BlockSpec((1, D, tf), lambda i, j: (j, 0, 0)),
            pl.BlockSpec((tf, D), lambda i, j: (j, 0)),
            _resident((1, D)),
        ],
        out_specs=pl.BlockSpec((tm, D), row),
        out_shape=jax.ShapeDtypeStruct((T, D), F32),
        compiler_params=_params(2),
        name="swiglu_ffn",
    )(f, h, w_gate, w_up, w_down, final_g)


def kernel(x, positions, norm_mix_g, w_in, b_gate, q_norm_g, w_uq, kv_norm_g, w_ukv, w_o_attn,
           sgu_norm_g, w_sgu, b_sgu, w_o_sgu, w_out, norm_ffn_g, w_gate_ffn, w_up_ffn,
           w_down_ffn, norm_final_g):
    B, S, D = x.shape
    T = B * S
    depth = w_in.shape[0]
    assert depth == 1, "the final norm is fused into the FFN epilogue of a single layer"
    assert w_in.shape[1:] == (D, D_IN)

    row_vec = lambda v: v.reshape(1, -1).astype(F32)

    h = x
    out = None
    for l in range(depth):
        w_in_t = jnp.swapaxes(w_in[l], 0, 1)
        cs, w_lat_t, w_uq_p, w_ukv_p = _prep_inproj(positions, w_in_t, w_uq[l], w_ukv[l])
        cs = cs.reshape(B, S, LANES)
        b_full = jnp.repeat(b_sgu[l].T, SGU_GROUP_DIM, axis=1).astype(F32)

        a, q_nope, q_pe, k_nope, v, k_pe = _inproj(
            h, row_vec(norm_mix_g[l]), cs, w_lat_t, row_vec(q_norm_g[l]), row_vec(kv_norm_g[l]),
            w_uq_p, w_ukv_p, tm=INPROJ_ROWS, sub=ROW_SUB)
        a2 = a.reshape(T, D)
        whole = lambda w, col_tile=None: (w, 0, w.shape[0], col_tile)
        attn, (w_uv_t, w_g0_t, w_g1_t, w_os, w_oa, w_o, w_uf) = _attention(
            q_nope, q_pe, k_nope, k_pe, v,
            [(w_in_t, UV_OFF, 2 * SGU_WIDTH, None), (w_in_t, GATE_OFF, D, None),
             (w_in_t, GATE_OFF + D, D, None), whole(w_o_sgu[l]), whole(w_o_attn[l]),
             whole(w_out[l]), whole(w_up_ffn[l], FFN_COLS)],
            tq=ATTN_Q_ROWS, hb=ATTN_HEADS_PER_STEP)
        m_sgu, w_gf = _sgu_branch(
            a2, w_uv_t, row_vec(sgu_norm_g[l]), w_sgu[l], b_full, w_os, w_g1_t,
            row_vec(b_gate[l, D:]), w_gate_ffn[l], FFN_COLS, tm=SGU_ROWS, sub=ROW_SUB)
        h_mid, f, w_df = _merge(
            attn.reshape(T, D), a2, m_sgu, h.reshape(T, D), w_oa, w_g0_t,
            row_vec(b_gate[l, :D]), w_o, row_vec(norm_ffn_g[l]), w_down_ffn[l], tm=MERGE_ROWS)
        out = _ffn(f, h_mid, w_gf, w_uf, w_df, row_vec(norm_final_g), tm=FFN_ROWS,
                   sub=FFN_COL_SUB)
        h = out.reshape(B, S, D)
    return h
```

```python
import functools

import jax
import jax.numpy as jnp
from jax import lax
from jax.experimental import pallas as pl
from jax.experimental.pallas import tpu as pltpu

D_MODEL = 2048
N_HEADS = 16
QK_NOPE_DIM = 128
QK_ROPE_DIM = 64
V_HEAD_DIM = 128
Q_LORA_RANK = 512
KV_LORA_RANK = 512
ROPE_THETA = 10000.0
SGU_GROUPS = 8
SGU_GROUP_DIM = 128
SGU_WIDTH = SGU_GROUPS * SGU_GROUP_DIM
CHUNK = 128
N_BRANCH = 2
RMS_EPS = 1e-6
KPE_OFF = Q_LORA_RANK + KV_LORA_RANK
UV_OFF = KPE_OFF + QK_ROPE_DIM
GATE_OFF = UV_OFF + 2 * SGU_WIDTH
D_IN = GATE_OFF + N_BRANCH * D_MODEL
LANES = 128
HALF_ROPE = QK_ROPE_DIM // 2
LOG2_E = 1.4426950408889634
QK_LOG2_SCALE = (QK_NOPE_DIM + QK_ROPE_DIM) ** -0.5 * LOG2_E
BF16_SUBLANES = 16

VMEM_LIMIT_BYTES = 60 * 1024 * 1024

INPROJ_ROWS = 512
SGU_ROWS = 512
MERGE_ROWS = 256
ROW_SUB = 256
ATTN_Q_ROWS = 256
ATTN_HEADS_PER_STEP = 2
FFN_ROWS = 1024
FFN_COLS = 512
FFN_COL_SUB = 256
PREP_STEPS = 4

F32 = jnp.float32
BF16 = jnp.bfloat16


def _rms(x, g):
    return x * lax.rsqrt(jnp.mean(x * x, axis=-1, keepdims=True) + RMS_EPS) * g


def _dot(a, b):
    return jnp.dot(a, b, preferred_element_type=F32)


def _dot_t(a, b_t):
    return lax.dot_general(a, b_t, (((1,), (1,)), ((), ())), preferred_element_type=F32)


def _resident(shape):
    return pl.BlockSpec(shape, lambda *_: (0,) * len(shape), pipeline_mode=pl.Buffered(1))


def _params(n_axes):
    return pltpu.CompilerParams(
        dimension_semantics=("arbitrary",) * n_axes, vmem_limit_bytes=VMEM_LIMIT_BYTES)


def _cast_block_specs(n_rows, n_cols, row0, n_steps, linear_step):
    share = 1 if (n_rows // n_steps) % BF16_SUBLANES == 0 else 2
    blk = n_rows * share // n_steps
    assert blk * n_steps == n_rows * share and blk % BF16_SUBLANES == 0 and row0 % blk == 0
    first = row0 // blk
    return (pl.BlockSpec((blk, n_cols), lambda *idx: (first + linear_step(*idx) // share, 0)),
            pl.BlockSpec((blk, n_cols), lambda *idx: (linear_step(*idx) // share, 0)))


def _cast_rows(srcs, dsts):
    for src, dst in zip(srcs, dsts):
        dst[...] = src[...].astype(dst.dtype)


def _rope_table_rows(pos_ref, freq_ref, cs_ref):
    ang = pos_ref[...] * freq_ref[...]
    cos, sin = jnp.cos(ang), jnp.sin(ang)
    n_rows = ang.shape[0]
    per_row = LANES // HALF_ROPE
    lane = lax.broadcasted_iota(jnp.int32, ang.shape, 1)

    def lanes_from(x, src, dst):
        shift = (dst - src) % LANES
        return pltpu.roll(x, shift, 1) if shift else x

    for k in range(per_row):
        src = k * HALF_ROPE
        row = jnp.where(
            lane < HALF_ROPE, lanes_from(cos, src, 0),
            jnp.where(lane < 2 * HALF_ROPE, lanes_from(cos, src, HALF_ROPE),
                      jnp.where(lane < 3 * HALF_ROPE, -lanes_from(sin, src, 2 * HALF_ROPE),
                                lanes_from(sin, src, 3 * HALF_ROPE))))
        cs_ref[pl.ds(k, n_rows, stride=per_row), :] = row


def _rope_table_inputs(positions):
    n_tok = positions.size
    per_row = LANES // HALF_ROPE
    inv_freq = ROPE_THETA ** (-jnp.arange(0, QK_ROPE_DIM, 2, dtype=F32) / QK_ROPE_DIM)
    pos_rep = jnp.repeat(positions.astype(F32).reshape(n_tok // per_row, per_row), HALF_ROPE, axis=1)
    return pos_rep, jnp.tile(inv_freq, per_row).reshape(1, LANES)


def _rope_dup(x, cs):
    y = x * cs
    return y + pltpu.roll(y, LANES // 2, 1)


def _rope_pair(p, cos4, sin4):
    lane = lax.broadcasted_iota(jnp.int32, p.shape, 1)
    first_half = lane % QK_ROPE_DIM < HALF_ROPE
    partner = jnp.where(first_half, pltpu.roll(p, LANES - HALF_ROPE, 1), pltpu.roll(p, HALF_ROPE, 1))
    return p * cos4 + partner * sin4


def _prep_kernel(pos_ref, freq_ref, wlat_ref, wuq_ref, wukv_ref,
                 cs_ref, olat_ref, ouq_ref, oukv_ref):
    _rope_table_rows(pos_ref, freq_ref, cs_ref)
    olat_ref[0:KPE_OFF] = wlat_ref[0:KPE_OFF].astype(BF16)
    x1 = wlat_ref[KPE_OFF:KPE_OFF + HALF_ROPE].astype(BF16)
    x2 = wlat_ref[KPE_OFF + HALF_ROPE:UV_OFF].astype(BF16)
    for k, part in enumerate((x1, x2, x2, x1)):
        olat_ref[KPE_OFF + k * HALF_ROPE:KPE_OFF + (k + 1) * HALF_ROPE] = part
    oukv_ref[...] = wukv_ref[...].astype(BF16)
    half = LANES // 2
    nope_cols = N_HEADS * QK_NOPE_DIM
    lane = lax.broadcasted_iota(jnp.int32, (wuq_ref.shape[0], LANES), 1)
    for pair in range(N_HEADS // 2):
        t0, t1, t2 = (wuq_ref[:, (3 * pair + k) * LANES:(3 * pair + k + 1) * LANES] for k in range(3))
        nope_odd = jnp.where(lane < half, pltpu.roll(t1, half, 1), pltpu.roll(t2, half, 1))
        for h, nope in ((2 * pair, t0), (2 * pair + 1, nope_odd)):
            ouq_ref[:, h * LANES:(h + 1) * LANES] = nope.astype(BF16)
        ouq_ref[:, nope_cols + pair * LANES:nope_cols + (pair + 1) * LANES] = (
            jnp.where(lane < half, t1, t2).astype(BF16))


def _prep_inproj(positions, w_in_t, w_uq, w_ukv):
    n_chunks = PREP_STEPS
    d = w_in_t.shape[1]
    r_q, r_kv = w_uq.shape[0], w_ukv.shape[0]
    lat_rows = UV_OFF + LANES - QK_ROPE_DIM
    uq_cols = N_HEADS * QK_NOPE_DIM + (N_HEADS // 2) * LANES
    pos_rep, freq = _rope_table_inputs(positions)
    n_tok = positions.size
    return pl.pallas_call(
        _prep_kernel,
        grid=(n_chunks,),
        in_specs=[
            pl.BlockSpec((pos_rep.shape[0] // n_chunks, LANES), lambda c: (c, 0)),
            _resident((1, LANES)),
            pl.BlockSpec((UV_OFF, d // n_chunks), lambda c: (0, c)),
            pl.BlockSpec((r_q // n_chunks, w_uq.shape[1]), lambda c: (c, 0)),
            pl.BlockSpec((r_kv // n_chunks, w_ukv.shape[1]), lambda c: (c, 0)),
        ],
        out_specs=[
            pl.BlockSpec((n_tok // n_chunks, LANES), lambda c: (c, 0)),
            pl.BlockSpec((lat_rows, d // n_chunks), lambda c: (0, c)),
            pl.BlockSpec((r_q // n_chunks, uq_cols), lambda c: (c, 0)),
            pl.BlockSpec((r_kv // n_chunks, w_ukv.shape[1]), lambda c: (c, 0)),
        ],
        out_shape=[
            jax.ShapeDtypeStruct((n_tok, LANES), F32),
            jax.ShapeDtypeStruct((lat_rows, d), BF16),
            jax.ShapeDtypeStruct((r_q, uq_cols), BF16),
            jax.ShapeDtypeStruct(w_ukv.shape, BF16),
        ],
        compiler_params=_params(1),
        name="prep_inproj",
    )(pos_rep, freq, w_in_t, w_uq, w_ukv)


def _inproj_kernel(x_ref, g_ref, cs_ref, wlat_ref, qg_ref, kvg_ref, wuq_ref, wukv_ref,
                   a_ref, qn_ref, qpe_ref, kn_ref, v_ref, kpe_ref, *, sub):
    def latents(r0):
        a = _rms(x_ref[0, r0:r0 + sub, :], g_ref[...]).astype(BF16)
        a_ref[0, r0:r0 + sub, :] = a
        return _dot_t(a, wlat_ref[...])

    starts = list(range(0, x_ref.shape[1], sub))
    for r0, z in zip(starts, [latents(r0) for r0 in starts]):
        rows = slice(r0, r0 + sub)
        qn = (_rms(z[:, :Q_LORA_RANK], qg_ref[...]) * QK_LOG2_SCALE).astype(BF16)
        kvn = _rms(z[:, Q_LORA_RANK:Q_LORA_RANK + KV_LORA_RANK], kvg_ref[...]).astype(BF16)
        cs = cs_ref[0, rows, :]
        kpe = _rope_dup(z[:, Q_LORA_RANK + KV_LORA_RANK:], cs)
        lane = lax.broadcasted_iota(jnp.int32, kpe.shape, 1)
        low = lane < QK_ROPE_DIM
        kpe_ref[0, rows, 0:LANES] = jnp.where(low, kpe, 0.0).astype(BF16)
        kpe_ref[0, rows, LANES:2 * LANES] = jnp.where(low, 0.0, kpe).astype(BF16)
        cs_swapped = pltpu.roll(cs, LANES // 2, 1)
        cos4 = jnp.where(low, cs, cs_swapped)
        sin4 = jnp.where(low, cs_swapped, cs)

        heads_per_dot = 4
        width = heads_per_dot * LANES
        nope_cols = N_HEADS * QK_NOPE_DIM
        for hg in range(N_HEADS // heads_per_dot):
            c0 = hg * width
            q_nope = _dot(qn, wuq_ref[:, c0:c0 + width])
            pe0 = nope_cols + c0 // 2
            q_pe = _dot(qn, wuq_ref[:, pe0:pe0 + width // 2])
            kv0 = 2 * c0
            kv_a = _dot(kvn, wukv_ref[:, kv0:kv0 + width])
            kv_b = _dot(kvn, wukv_ref[:, kv0 + width:kv0 + 2 * width])
            for pp in range(heads_per_dot // 2):
                pair = hg * (heads_per_dot // 2) + pp
                qpe_ref[0, pair, rows, :] = _rope_pair(
                    q_pe[:, pp * LANES:(pp + 1) * LANES], cos4, sin4).astype(BF16)
            for hh in range(heads_per_dot):
                h = hg * heads_per_dot + hh
                sl = slice(hh * LANES, (hh + 1) * LANES)
                qn_ref[0, h, rows, :] = q_nope[:, sl].astype(BF16)
                kv = kv_a if hh < heads_per_dot // 2 else kv_b
                k0 = (hh % (heads_per_dot // 2)) * 2 * LANES
                kn_ref[0, h, rows, :] = kv[:, k0:k0 + LANES].astype(BF16)
                v_ref[0, h, rows, :] = kv[:, k0 + LANES:k0 + 2 * LANES].astype(BF16)


def _inproj(x, norm_g, cs, w_lat_t, q_g, kv_g, w_uq, w_ukv, tm, sub):
    B, S, D = x.shape
    row = lambda b, i: (b, i, 0)
    head = lambda b, i: (b, 0, i, 0)
    return pl.pallas_call(
        functools.partial(_inproj_kernel, sub=sub),
        grid=(B, S // tm),
        in_specs=[
            pl.BlockSpec((1, tm, D), row),
            _resident((1, D)),
            pl.BlockSpec((1, tm, LANES), row),
            _resident(w_lat_t.shape),
            _resident((1, Q_LORA_RANK)),
            _resident((1, KV_LORA_RANK)),
            _resident(w_uq.shape),
            _resident(w_ukv.shape),
        ],
        out_specs=[
            pl.BlockSpec((1, tm, D), row),
            pl.BlockSpec((1, N_HEADS, tm, LANES), head),
            pl.BlockSpec((1, N_HEADS // 2, tm, LANES), head),
            pl.BlockSpec((1, N_HEADS, tm, LANES), head),
            pl.BlockSpec((1, N_HEADS, tm, LANES), head),
            pl.BlockSpec((1, tm, 2 * LANES), row),
        ],
        out_shape=[
            jax.ShapeDtypeStruct((B, S, D), BF16),
            jax.ShapeDtypeStruct((B, N_HEADS, S, LANES), BF16),
            jax.ShapeDtypeStruct((B, N_HEADS // 2, S, LANES), BF16),
            jax.ShapeDtypeStruct((B, N_HEADS, S, LANES), BF16),
            jax.ShapeDtypeStruct((B, N_HEADS, S, LANES), BF16),
            jax.ShapeDtypeStruct((B, S, 2 * LANES), BF16),
        ],
        compiler_params=_params(2),
        name="inproj",
    )(x, norm_g, cs, w_lat_t, q_g, kv_g, w_uq, w_ukv)


def _sgu_kernel(a_ref, wuv_ref, sg_ref, ws_ref, bfull_ref, wos_ref, wg1_ref, bg1_ref, cast_ref,
                m_ref, cast_out_ref, *, sub):
    _cast_rows([cast_ref], [cast_out_ref])
    n_chunks = sub // CHUNK
    t_idx = lax.broadcasted_iota(jnp.int32, (CHUNK, CHUNK), 0)
    s_idx = lax.broadcasted_iota(jnp.int32, (CHUNK, CHUNK), 1)
    causal = t_idx >= s_idx
    ws = [jnp.where(causal, ws_ref[g], 0.0).astype(BF16) for g in range(SGU_GROUPS)]
    bfull = bfull_ref[...]
    def gating_unit(uv_raw):
        uv = jax.nn.gelu(uv_raw)
        u = uv[:, :SGU_WIDTH]
        vn = _rms(uv[:, SGU_WIDTH:], sg_ref[...]).astype(BF16)
        mixed_cols = []
        for g in range(SGU_GROUPS):
            gs = slice(g * SGU_GROUP_DIM, (g + 1) * SGU_GROUP_DIM)
            rhs = jnp.concatenate(
                [vn[c * CHUNK:(c + 1) * CHUNK, gs] for c in range(n_chunks)], axis=1)
            mixed_cols.append(_dot(ws[g], rhs))
        rows = []
        for c in range(n_chunks):
            cs = slice(c * SGU_GROUP_DIM, (c + 1) * SGU_GROUP_DIM)
            mixed = jnp.concatenate([mixed_cols[g][:, cs] for g in range(SGU_GROUPS)], axis=1)
            rows.append(u[c * CHUNK:(c + 1) * CHUNK] * (mixed + bfull))
        return jnp.concatenate(rows, axis=0).astype(BF16)

    a = a_ref[...]
    uv_raw = _dot_t(a, wuv_ref[...])
    gate_raw = _dot_t(a, wg1_ref[...])
    for r0 in range(0, a_ref.shape[0], sub):
        rows = slice(r0, r0 + sub)
        y_sgu = _dot(gating_unit(uv_raw[rows]), wos_ref[...])
        m_ref[rows, :] = jax.nn.sigmoid(gate_raw[rows] + bg1_ref[...]) * y_sgu


def _sgu_branch(a, w_uv, sgu_g, w_s, b_full, w_o_sgu, w_g1, b_g1, cast_w, tm, sub):
    T, D = a.shape
    row = lambda i: (i, 0)
    cast_in, cast_out = _cast_block_specs(*cast_w.shape, 0, T // tm, lambda i: i)
    return pl.pallas_call(
        functools.partial(_sgu_kernel, sub=sub),
        grid=(T // tm,),
        in_specs=[
            pl.BlockSpec((tm, D), row),
            _resident(w_uv.shape),
            _resident(sgu_g.shape),
            _resident(w_s.shape),
            _resident(b_full.shape),
            _resident(w_o_sgu.shape),
            _resident(w_g1.shape),
            _resident(b_g1.shape),
            cast_in,
        ],
        out_specs=[pl.BlockSpec((tm, D), row), cast_out],
        out_shape=[jax.ShapeDtypeStruct((T, D), F32), jax.ShapeDtypeStruct(cast_w.shape, BF16)],
        compiler_params=_params(1),
        name="sgu_branch",
    )(a, w_uv, sgu_g, w_s, b_full, w_o_sgu, w_g1, b_g1, cast_w)


def _attn_kernel(qn_ref, qpe_ref, kn_ref, kpe_ref, v_ref, *rest, tq, n_cast):
    cast_in, (o_ref,), cast_out, (kf_ref, vf_ref) = (
        rest[:n_cast], rest[n_cast:n_cast + 1], rest[n_cast + 1:2 * n_cast + 1], rest[2 * n_cast + 1:])
    _cast_rows(cast_in, cast_out)
    seq = qn_ref.shape[2]
    row = lax.broadcasted_iota(jnp.int32, (tq, tq), 0)
    col = lax.broadcasted_iota(jnp.int32, (tq, tq), 1)
    causal = row >= col
    neg = jnp.finfo(F32).min
    heads = qn_ref.shape[1]
    for hh in range(heads):
        parity = hh % 2
        kf_ref[hh, :, 0:LANES] = kn_ref[0, hh]
        kf_ref[hh, :, LANES:2 * LANES] = kpe_ref[0, :, parity * LANES:(parity + 1) * LANES]
        vf_ref[hh, :, 0:LANES] = v_ref[0, hh]
        vf_ref[hh, :, LANES:2 * LANES] = jnp.ones((seq, LANES), vf_ref.dtype)

    nt = (((1,), (1,)), ((), ()))
    def scores(pair, hh):
        q0 = pair * 2 * tq
        k1, k2 = q0 + tq, q0 + 2 * tq
        q = jnp.concatenate([qn_ref[0, hh, q0:k2, :], qpe_ref[0, hh // 2, q0:k2, :]], axis=1)
        return (lax.dot_general(q, kf_ref[hh, 0:k1, :], nt, preferred_element_type=F32),
                lax.dot_general(q[tq:], kf_ref[hh, k1:k2, :], nt, preferred_element_type=F32))

    def finish(pair, hh, s_main, s_ext):
        q0 = pair * 2 * tq
        k1, k2 = q0 + tq, q0 + 2 * tq
        top = s_main[0:tq]
        top_diag = jnp.where(causal, top[:, q0:k1], neg)
        top = jnp.concatenate([top[:, 0:q0], top_diag], axis=1) if pair else top_diag
        bot = s_main[tq:]
        ext = jnp.where(causal, s_ext, neg)
        m_top = jnp.max(top, axis=-1, keepdims=True)
        m_bot = jnp.maximum(jnp.max(bot, axis=-1, keepdims=True),
                            jnp.max(ext, axis=-1, keepdims=True))
        p_main = jnp.concatenate([jnp.exp2(top - m_top), jnp.exp2(bot - m_bot)], axis=0)
        acc = _dot(p_main.astype(BF16), vf_ref[hh, 0:k1, :])
        acc_bot = acc[tq:] + _dot(jnp.exp2(ext - m_bot).astype(BF16), vf_ref[hh, k1:k2, :])
        lanes = slice(hh * LANES, (hh + 1) * LANES)
        o_ref[0, q0:k1, lanes] = (acc[0:tq, 0:LANES] / acc[0:tq, LANES:]).astype(o_ref.dtype)
        o_ref[0, k1:k2, lanes] = (acc_bot[:, 0:LANES] / acc_bot[:, LANES:]).astype(o_ref.dtype)

    work = [(pair, hh) for pair in reversed(range(seq // (2 * tq))) for hh in range(heads)]
    lead = 2
    pending = [scores(*w) for w in work[:lead]]
    for idx, (pair, hh) in enumerate(work):
        if idx + lead < len(work):
            pending.append(scores(*work[idx + lead]))
        finish(pair, hh, *pending.pop(0))


def _attention(q_nope, q_pe, k_nope, k_pe, v, casts, tq, hb):
    B, H, S, _ = q_nope.shape
    assert hb % 2 == 0, "head pairs share a rope tile"
    groups = H // hb
    specs = [_cast_block_specs(n, w.shape[1], r0, B * groups, lambda b, g: b * groups + g)
             for w, r0, n in casts]
    outs = pl.pallas_call(
        functools.partial(_attn_kernel, tq=tq, n_cast=len(casts)),
        grid=(B, groups),
        in_specs=[
            pl.BlockSpec((1, hb, S, LANES), lambda b, g: (b, g, 0, 0)),
            pl.BlockSpec((1, hb // 2, S, LANES), lambda b, g: (b, g, 0, 0)),
            pl.BlockSpec((1, hb, S, LANES), lambda b, g: (b, g, 0, 0)),
            pl.BlockSpec((1, S, 2 * LANES), lambda b, g: (b, 0, 0)),
            pl.BlockSpec((1, hb, S, LANES), lambda b, g: (b, g, 0, 0)),
        ] + [s_in for s_in, _ in specs],
        out_specs=[pl.BlockSpec((1, S, hb * LANES), lambda b, g: (b, 0, g))]
        + [s_out for _, s_out in specs],
        out_shape=[jax.ShapeDtypeStruct((B, S, H * V_HEAD_DIM), BF16)]
        + [jax.ShapeDtypeStruct((n, w.shape[1]), BF16) for w, _, n in casts],
        scratch_shapes=[pltpu.VMEM((hb, S, 2 * LANES), BF16), pltpu.VMEM((hb, S, 2 * LANES), BF16)],
        compiler_params=_params(2),
        name="mla_attention",
    )(q_nope, q_pe, k_nope, k_pe, v, *[w for w, _, _ in casts])
    return outs[0], outs[1:]


def _merge_kernel(attn_ref, a_ref, m_ref, x_ref, woa_hbm, wg0_hbm, bg0_ref, wout_hbm, fg_ref,
                  cast_ref, h_ref, f_ref, cast_out_ref, woa_ref, wg0_ref, wout_ref, sems):
    def weight_copies():
        pairs = ((woa_hbm, woa_ref), (wg0_hbm, wg0_ref), (wout_hbm, wout_ref))
        return [pltpu.make_async_copy(src, dst, sems.at[k]) for k, (src, dst) in enumerate(pairs)]

    def body(first_step):
        copies = weight_copies() if first_step else None
        if first_step:
            for c in copies:
                c.start()
        _cast_rows([cast_ref], [cast_out_ref])
        if first_step:
            copies[0].wait()
        y_attn = _dot(attn_ref[...], woa_ref[...])
        if first_step:
            copies[1].wait()
        gate = jax.nn.sigmoid(_dot_t(a_ref[...], wg0_ref[...]) + bg0_ref[...])
        merged = (gate * y_attn + m_ref[...]).astype(BF16)
        if first_step:
            copies[2].wait()
        h = x_ref[...] + _dot(merged, wout_ref[...])
        h_ref[...] = h
        f_ref[...] = _rms(h, fg_ref[...]).astype(BF16)

    i = pl.program_id(0)
    pl.when(i == 0)(functools.partial(body, True))
    pl.when(i > 0)(functools.partial(body, False))


def _merge(attn, a, m_sgu, x, w_o_attn, w_g0, b_g0, w_out, ffn_g, cast_w, tm):
    T, D = x.shape
    row = lambda i: (i, 0)
    tile = pl.BlockSpec((tm, D), row)
    cast_in, cast_out = _cast_block_specs(*cast_w.shape, 0, T // tm, lambda i: i)
    in_hbm = pl.BlockSpec(memory_space=pl.ANY)
    weights = (w_o_attn, w_g0, w_out)
    return pl.pallas_call(
        _merge_kernel,
        grid=(T // tm,),
        in_specs=[tile, tile, tile, tile,
                  in_hbm, in_hbm, _resident(b_g0.shape),
                  in_hbm, _resident(ffn_g.shape), cast_in],
        out_specs=[tile, tile, cast_out],
        out_shape=[jax.ShapeDtypeStruct((T, D), F32), jax.ShapeDtypeStruct((T, D), BF16),
                   jax.ShapeDtypeStruct(cast_w.shape, BF16)],
        scratch_shapes=[pltpu.VMEM(w.shape, w.dtype) for w in weights]
        + [pltpu.SemaphoreType.DMA((len(weights),))],
        compiler_params=_params(1),
        name="merge_outproj",
    )(attn, a, m_sgu, x, w_o_attn, w_g0, b_g0, w_out, ffn_g, cast_w)


def _ffn_kernel(f_ref, h_ref, wg_ref, wu_ref, wd_ref, ng_ref, o_ref, *, sub):
    j = pl.program_id(1)

    f = f_ref[...]
    starts = list(range(0, wg_ref.shape[1], sub))
    projected = [(_dot(f, wg_ref[:, c0:c0 + sub]), _dot(f, wu_ref[:, c0:c0 + sub]))
                 for c0 in starts]
    act = jnp.concatenate(
        [(jax.nn.silu(gate) * up).astype(BF16) for gate, up in projected], axis=1)
    o_ref[...] = jnp.where(j == 0, h_ref[...], o_ref[...]) + _dot(act, wd_ref[...])

    @pl.when(j == pl.num_programs(1) - 1)
    def _():
        o_ref[...] = _rms(o_ref[...], ng_ref[...])


def _ffn(f, h, w_gate, w_up, w_down, final_g, tm, tf, sub):
    T, D = h.shape
    d_ff = w_gate.shape[1]
    row = lambda i, j: (i, 0)
    n_i, n_j = T // tm, d_ff // tf
    h_row = lambda i, j: (jnp.minimum(i + (j >= n_j // 2).astype(jnp.int32), n_i - 1), 0)
    return pl.pallas_call(
        functools.partial(_ffn_kernel, sub=sub),
        grid=(n_i, n_j),
        in_specs=[
            pl.BlockSpec((tm, D), row),
            pl.BlockSpec((tm, D), h_row),
            pl.BlockSpec((D, tf), lambda i, j: (0, j)),
            pl.BlockSpec((D, tf), lambda i, j: (0, j)),
            pl.BlockSpec((tf, D), lambda i, j: (j, 0)),
            _resident((1, D)),
        ],
        out_specs=pl.BlockSpec((tm, D), row),
        out_shape=jax.ShapeDtypeStruct((T, D), F32),
        compiler_params=_params(2),
        name="swiglu_ffn",
    )(f, h, w_gate, w_up, w_down, final_g)


def kernel(x, positions, norm_mix_g, w_in, b_gate, q_norm_g, w_uq, kv_norm_g, w_ukv, w_o_attn,
           sgu_norm_g, w_sgu, b_sgu, w_o_sgu, w_out, norm_ffn_g, w_gate_ffn, w_up_ffn,
           w_down_ffn, norm_final_g):
    B, S, D = x.shape
    T = B * S
    depth = w_in.shape[0]
    assert depth == 1, "the final norm is fused into the FFN epilogue of a single layer"
    assert w_in.shape[1:] == (D, D_IN)

    row_vec = lambda v: v.reshape(1, -1).astype(F32)

    h = x
    out = None
    for l in range(depth):
        w_in_t = jnp.swapaxes(w_in[l], 0, 1)
        cs, w_lat_t, w_uq_p, w_ukv_p = _prep_inproj(positions, w_in_t, w_uq[l], w_ukv[l])
        cs = cs.reshape(B, S, LANES)
        b_full = jnp.repeat(b_sgu[l].T, SGU_GROUP_DIM, axis=1).astype(F32)

        a, q_nope, q_pe, k_nope, v, k_pe = _inproj(
            h, row_vec(norm_mix_g[l]), cs, w_lat_t, row_vec(q_norm_g[l]), row_vec(kv_norm_g[l]),
            w_uq_p, w_ukv_p, tm=INPROJ_ROWS, sub=ROW_SUB)
        a2 = a.reshape(T, D)
        whole = lambda w: (w, 0, w.shape[0])
        attn, (w_uv_t, w_g0_t, w_g1_t, w_os, w_oa, w_o, w_uf) = _attention(
            q_nope, q_pe, k_nope, k_pe, v,
            [(w_in_t, UV_OFF, 2 * SGU_WIDTH), (w_in_t, GATE_OFF, D), (w_in_t, GATE_OFF + D, D),
             whole(w_o_sgu[l]), whole(w_o_attn[l]), whole(w_out[l]), whole(w_up_ffn[l])],
            tq=ATTN_Q_ROWS, hb=ATTN_HEADS_PER_STEP)
        m_sgu, w_gf = _sgu_branch(
            a2, w_uv_t, row_vec(sgu_norm_g[l]), w_sgu[l], b_full, w_os, w_g1_t,
            row_vec(b_gate[l, D:]), w_gate_ffn[l], tm=SGU_ROWS, sub=ROW_SUB)
        h_mid, f, w_df = _merge(
            attn.reshape(T, D), a2, m_sgu, h.reshape(T, D), w_oa, w_g0_t,
            row_vec(b_gate[l, :D]), w_o, row_vec(norm_ffn_g[l]), w_down_ffn[l], tm=MERGE_ROWS)
        out = _ffn(f, h_mid, w_gf, w_uf, w_df, row_vec(norm_final_g), tm=FFN_ROWS,
                   tf=FFN_COLS, sub=FFN_COL_SUB)
        h = out.reshape(B, S, D)
    return h
```

```python
import functools

import jax
import jax.numpy as jnp
from jax import lax
from jax.experimental import pallas as pl
from jax.experimental.pallas import tpu as pltpu

D_MODEL = 2048
N_HEADS = 16
QK_NOPE_DIM = 128
QK_ROPE_DIM = 64
V_HEAD_DIM = 128
Q_LORA_RANK = 512
KV_LORA_RANK = 512
ROPE_THETA = 10000.0
SGU_GROUPS = 8
SGU_GROUP_DIM = 128
SGU_WIDTH = SGU_GROUPS * SGU_GROUP_DIM
CHUNK = 128
N_BRANCH = 2
RMS_EPS = 1e-6
KPE_OFF = Q_LORA_RANK + KV_LORA_RANK
UV_OFF = KPE_OFF + QK_ROPE_DIM
GATE_OFF = UV_OFF + 2 * SGU_WIDTH
D_IN = GATE_OFF + N_BRANCH * D_MODEL
LANES = 128
HALF_ROPE = QK_ROPE_DIM // 2
LOG2_E = 1.4426950408889634
QK_LOG2_SCALE = (QK_NOPE_DIM + QK_ROPE_DIM) ** -0.5 * LOG2_E
BF16_SUBLANES = 16

VMEM_LIMIT_BYTES = 60 * 1024 * 1024

INPROJ_ROWS = 512
SGU_ROWS = 512
MERGE_ROWS = 256
ROW_SUB = 256
ATTN_Q_ROWS = 256
ATTN_HEADS_PER_STEP = 2
FFN_ROWS = 1024
FFN_COLS = 512
FFN_COL_SUB = 256
PREP_STEPS = 4

F32 = jnp.float32
BF16 = jnp.bfloat16


def _rms(x, g):
    return x * lax.rsqrt(jnp.mean(x * x, axis=-1, keepdims=True) + RMS_EPS) * g


def _dot(a, b):
    return jnp.dot(a, b, preferred_element_type=F32)


def _dot_t(a, b_t):
    return lax.dot_general(a, b_t, (((1,), (1,)), ((), ())), preferred_element_type=F32)


def _resident(shape):
    return pl.BlockSpec(shape, lambda *_: (0,) * len(shape), pipeline_mode=pl.Buffered(1))


def _params(n_axes):
    return pltpu.CompilerParams(
        dimension_semantics=("arbitrary",) * n_axes, vmem_limit_bytes=VMEM_LIMIT_BYTES)


def _cast_block_specs(n_rows, n_cols, row0, n_steps, linear_step):
    share = 1 if (n_rows // n_steps) % BF16_SUBLANES == 0 else 2
    blk = n_rows * share // n_steps
    assert blk * n_steps == n_rows * share and blk % BF16_SUBLANES == 0 and row0 % blk == 0
    first = row0 // blk
    return (pl.BlockSpec((blk, n_cols), lambda *idx: (first + linear_step(*idx) // share, 0)),
            pl.BlockSpec((blk, n_cols), lambda *idx: (linear_step(*idx) // share, 0)))


def _cast_rows(srcs, dsts):
    for src, dst in zip(srcs, dsts):
        dst[...] = src[...].astype(dst.dtype)


def _rope_table_rows(pos_ref, freq_ref, cs_ref):
    ang = pos_ref[...] * freq_ref[...]
    cos, sin = jnp.cos(ang), jnp.sin(ang)
    n_rows = ang.shape[0]
    per_row = LANES // HALF_ROPE
    lane = lax.broadcasted_iota(jnp.int32, ang.shape, 1)

    def lanes_from(x, src, dst):
        shift = (dst - src) % LANES
        return pltpu.roll(x, shift, 1) if shift else x

    for k in range(per_row):
        src = k * HALF_ROPE
        row = jnp.where(
            lane < HALF_ROPE, lanes_from(cos, src, 0),
            jnp.where(lane < 2 * HALF_ROPE, lanes_from(cos, src, HALF_ROPE),
                      jnp.where(lane < 3 * HALF_ROPE, -lanes_from(sin, src, 2 * HALF_ROPE),
                                lanes_from(sin, src, 3 * HALF_ROPE))))
        cs_ref[pl.ds(k, n_rows, stride=per_row), :] = row


def _rope_table_inputs(positions):
    n_tok = positions.size
    per_row = LANES // HALF_ROPE
    inv_freq = ROPE_THETA ** (-jnp.arange(0, QK_ROPE_DIM, 2, dtype=F32) / QK_ROPE_DIM)
    pos_rep = jnp.repeat(positions.astype(F32).reshape(n_tok // per_row, per_row), HALF_ROPE, axis=1)
    return pos_rep, jnp.tile(inv_freq, per_row).reshape(1, LANES)


def _rope_dup(x, cs):
    y = x * cs
    return y + pltpu.roll(y, LANES // 2, 1)


def _rope_pair(p, cos4, sin4):
    lane = lax.broadcasted_iota(jnp.int32, p.shape, 1)
    first_half = lane % QK_ROPE_DIM < HALF_ROPE
    partner = jnp.where(first_half, pltpu.roll(p, LANES - HALF_ROPE, 1), pltpu.roll(p, HALF_ROPE, 1))
    return p * cos4 + partner * sin4


def _prep_kernel(pos_ref, freq_ref, wlat_ref, wuq_ref, wukv_ref,
                 cs_ref, olat_ref, ouq_ref, oukv_ref):
    _rope_table_rows(pos_ref, freq_ref, cs_ref)
    olat_ref[0:KPE_OFF] = wlat_ref[0:KPE_OFF].astype(BF16)
    x1 = wlat_ref[KPE_OFF:KPE_OFF + HALF_ROPE].astype(BF16)
    x2 = wlat_ref[KPE_OFF + HALF_ROPE:UV_OFF].astype(BF16)
    for k, part in enumerate((x1, x2, x2, x1)):
        olat_ref[KPE_OFF + k * HALF_ROPE:KPE_OFF + (k + 1) * HALF_ROPE] = part
    oukv_ref[...] = wukv_ref[...].astype(BF16)
    half = LANES // 2
    nope_cols = N_HEADS * QK_NOPE_DIM
    lane = lax.broadcasted_iota(jnp.int32, (wuq_ref.shape[0], LANES), 1)
    for pair in range(N_HEADS // 2):
        t0, t1, t2 = (wuq_ref[:, (3 * pair + k) * LANES:(3 * pair + k + 1) * LANES] for k in range(3))
        nope_odd = jnp.where(lane < half, pltpu.roll(t1, half, 1), pltpu.roll(t2, half, 1))
        for h, nope in ((2 * pair, t0), (2 * pair + 1, nope_odd)):
            ouq_ref[:, h * LANES:(h + 1) * LANES] = nope.astype(BF16)
        ouq_ref[:, nope_cols + pair * LANES:nope_cols + (pair + 1) * LANES] = (
            jnp.where(lane < half, t1, t2).astype(BF16))


def _prep_inproj(positions, w_in_t, w_uq, w_ukv):
    n_chunks = PREP_STEPS
    d = w_in_t.shape[1]
    r_q, r_kv = w_uq.shape[0], w_ukv.shape[0]
    lat_rows = UV_OFF + LANES - QK_ROPE_DIM
    uq_cols = N_HEADS * QK_NOPE_DIM + (N_HEADS // 2) * LANES
    pos_rep, freq = _rope_table_inputs(positions)
    n_tok = positions.size
    return pl.pallas_call(
        _prep_kernel,
        grid=(n_chunks,),
        in_specs=[
            pl.BlockSpec((pos_rep.shape[0] // n_chunks, LANES), lambda c: (c, 0)),
            _resident((1, LANES)),
            pl.BlockSpec((UV_OFF, d // n_chunks), lambda c: (0, c)),
            pl.BlockSpec((r_q // n_chunks, w_uq.shape[1]), lambda c: (c, 0)),
            pl.BlockSpec((r_kv // n_chunks, w_ukv.shape[1]), lambda c: (c, 0)),
        ],
        out_specs=[
            pl.BlockSpec((n_tok // n_chunks, LANES), lambda c: (c, 0)),
            pl.BlockSpec((lat_rows, d // n_chunks), lambda c: (0, c)),
            pl.BlockSpec((r_q // n_chunks, uq_cols), lambda c: (c, 0)),
            pl.BlockSpec((r_kv // n_chunks, w_ukv.shape[1]), lambda c: (c, 0)),
        ],
        out_shape=[
            jax.ShapeDtypeStruct((n_tok, LANES), F32),
            jax.ShapeDtypeStruct((lat_rows, d), BF16),
            jax.ShapeDtypeStruct((r_q, uq_cols), BF16),
            jax.ShapeDtypeStruct(w_ukv.shape, BF16),
        ],
        compiler_params=_params(1),
        name="prep_inproj",
    )(pos_rep, freq, w_in_t, w_uq, w_ukv)


def _inproj_kernel(x_ref, g_ref, cs_ref, wlat_ref, qg_ref, kvg_ref, wuq_ref, wukv_ref,
                   a_ref, qn_ref, qpe_ref, kn_ref, v_ref, kpe_ref, *, sub):
    def latents(r0):
        a = _rms(x_ref[0, r0:r0 + sub, :], g_ref[...]).astype(BF16)
        a_ref[0, r0:r0 + sub, :] = a
        return _dot_t(a, wlat_ref[...])

    starts = list(range(0, x_ref.shape[1], sub))
    for r0, z in zip(starts, [latents(r0) for r0 in starts]):
        rows = slice(r0, r0 + sub)
        qn = (_rms(z[:, :Q_LORA_RANK], qg_ref[...]) * QK_LOG2_SCALE).astype(BF16)
        kvn = _rms(z[:, Q_LORA_RANK:Q_LORA_RANK + KV_LORA_RANK], kvg_ref[...]).astype(BF16)
        cs = cs_ref[0, rows, :]
        kpe = _rope_dup(z[:, Q_LORA_RANK + KV_LORA_RANK:], cs)
        lane = lax.broadcasted_iota(jnp.int32, kpe.shape, 1)
        low = lane < QK_ROPE_DIM
        kpe_ref[0, rows, 0:LANES] = jnp.where(low, kpe, 0.0).astype(BF16)
        kpe_ref[0, rows, LANES:2 * LANES] = jnp.where(low, 0.0, kpe).astype(BF16)
        cs_swapped = pltpu.roll(cs, LANES // 2, 1)
        cos4 = jnp.where(low, cs, cs_swapped)
        sin4 = jnp.where(low, cs_swapped, cs)

        heads_per_dot = 4
        width = heads_per_dot * LANES
        nope_cols = N_HEADS * QK_NOPE_DIM
        for hg in range(N_HEADS // heads_per_dot):
            c0 = hg * width
            q_nope = _dot(qn, wuq_ref[:, c0:c0 + width])
            pe0 = nope_cols + c0 // 2
            q_pe = _dot(qn, wuq_ref[:, pe0:pe0 + width // 2])
            kv0 = 2 * c0
            kv_a = _dot(kvn, wukv_ref[:, kv0:kv0 + width])
            kv_b = _dot(kvn, wukv_ref[:, kv0 + width:kv0 + 2 * width])
            for pp in range(heads_per_dot // 2):
                pair = hg * (heads_per_dot // 2) + pp
                qpe_ref[0, pair, rows, :] = _rope_pair(
                    q_pe[:, pp * LANES:(pp + 1) * LANES], cos4, sin4).astype(BF16)
            for hh in range(heads_per_dot):
                h = hg * heads_per_dot + hh
                sl = slice(hh * LANES, (hh + 1) * LANES)
                qn_ref[0, h, rows, :] = q_nope[:, sl].astype(BF16)
                kv = kv_a if hh < heads_per_dot // 2 else kv_b
                k0 = (hh % (heads_per_dot // 2)) * 2 * LANES
                kn_ref[0, h, rows, :] = kv[:, k0:k0 + LANES].astype(BF16)
                v_ref[0, h, rows, :] = kv[:, k0 + LANES:k0 + 2 * LANES].astype(BF16)


def _inproj(x, norm_g, cs, w_lat_t, q_g, kv_g, w_uq, w_ukv, tm, sub):
    B, S, D = x.shape
    row = lambda b, i: (b, i, 0)
    head = lambda b, i: (b, 0, i, 0)
    return pl.pallas_call(
        functools.partial(_inproj_kernel, sub=sub),
        grid=(B, S // tm),
        in_specs=[
            pl.BlockSpec((1, tm, D), row),
            _resident((1, D)),
            pl.BlockSpec((1, tm, LANES), row),
            _resident(w_lat_t.shape),
            _resident((1, Q_LORA_RANK)),
            _resident((1, KV_LORA_RANK)),
            _resident(w_uq.shape),
            _resident(w_ukv.shape),
        ],
        out_specs=[
            pl.BlockSpec((1, tm, D), row),
            pl.BlockSpec((1, N_HEADS, tm, LANES), head),
            pl.BlockSpec((1, N_HEADS // 2, tm, LANES), head),
            pl.BlockSpec((1, N_HEADS, tm, LANES), head),
            pl.BlockSpec((1, N_HEADS, tm, LANES), head),
            pl.BlockSpec((1, tm, 2 * LANES), row),
        ],
        out_shape=[
            jax.ShapeDtypeStruct((B, S, D), BF16),
            jax.ShapeDtypeStruct((B, N_HEADS, S, LANES), BF16),
            jax.ShapeDtypeStruct((B, N_HEADS // 2, S, LANES), BF16),
            jax.ShapeDtypeStruct((B, N_HEADS, S, LANES), BF16),
            jax.ShapeDtypeStruct((B, N_HEADS, S, LANES), BF16),
            jax.ShapeDtypeStruct((B, S, 2 * LANES), BF16),
        ],
        compiler_params=_params(2),
        name="inproj",
    )(x, norm_g, cs, w_lat_t, q_g, kv_g, w_uq, w_ukv)


def _sgu_kernel(a_ref, wuv_ref, sg_ref, ws_ref, bfull_ref, wos_ref, wg1_ref, bg1_ref, cast_ref,
                m_ref, cast_out_ref, *, sub):
    _cast_rows([cast_ref], [cast_out_ref])
    n_chunks = sub // CHUNK
    t_idx = lax.broadcasted_iota(jnp.int32, (CHUNK, CHUNK), 0)
    s_idx = lax.broadcasted_iota(jnp.int32, (CHUNK, CHUNK), 1)
    causal = t_idx >= s_idx
    ws = [jnp.where(causal, ws_ref[g], 0.0).astype(BF16) for g in range(SGU_GROUPS)]
    bfull = bfull_ref[...]
    def gating_unit(uv_raw):
        uv = jax.nn.gelu(uv_raw)
        u = uv[:, :SGU_WIDTH]
        vn = _rms(uv[:, SGU_WIDTH:], sg_ref[...]).astype(BF16)
        mixed_cols = []
        for g in range(SGU_GROUPS):
            gs = slice(g * SGU_GROUP_DIM, (g + 1) * SGU_GROUP_DIM)
            rhs = jnp.concatenate(
                [vn[c * CHUNK:(c + 1) * CHUNK, gs] for c in range(n_chunks)], axis=1)
            mixed_cols.append(_dot(ws[g], rhs))
        rows = []
        for c in range(n_chunks):
            cs = slice(c * SGU_GROUP_DIM, (c + 1) * SGU_GROUP_DIM)
            mixed = jnp.concatenate([mixed_cols[g][:, cs] for g in range(SGU_GROUPS)], axis=1)
            rows.append(u[c * CHUNK:(c + 1) * CHUNK] * (mixed + bfull))
        return jnp.concatenate(rows, axis=0).astype(BF16)

    a = a_ref[...]
    uv_raw = _dot_t(a, wuv_ref[...])
    gate_raw = _dot_t(a, wg1_ref[...])
    for r0 in range(0, a_ref.shape[0], sub):
        rows = slice(r0, r0 + sub)
        y_sgu = _dot(gating_unit(uv_raw[rows]), wos_ref[...])
        m_ref[rows, :] = jax.nn.sigmoid(gate_raw[rows] + bg1_ref[...]) * y_sgu


def _sgu_branch(a, w_uv, sgu_g, w_s, b_full, w_o_sgu, w_g1, b_g1, cast_w, tm, sub):
    T, D = a.shape
    row = lambda i: (i, 0)
    cast_in, cast_out = _cast_block_specs(*cast_w.shape, 0, T // tm, lambda i: i)
    return pl.pallas_call(
        functools.partial(_sgu_kernel, sub=sub),
        grid=(T // tm,),
        in_specs=[
            pl.BlockSpec((tm, D), row),
            _resident(w_uv.shape),
            _resident(sgu_g.shape),
            _resident(w_s.shape),
            _resident(b_full.shape),
            _resident(w_o_sgu.shape),
            _resident(w_g1.shape),
            _resident(b_g1.shape),
            cast_in,
        ],
        out_specs=[pl.BlockSpec((tm, D), row), cast_out],
        out_shape=[jax.ShapeDtypeStruct((T, D), F32), jax.ShapeDtypeStruct(cast_w.shape, BF16)],
        compiler_params=_params(1),
        name="sgu_branch",
    )(a, w_uv, sgu_g, w_s, b_full, w_o_sgu, w_g1, b_g1, cast_w)


def _attn_kernel(qn_ref, qpe_ref, kn_ref, kpe_ref, v_ref, *rest, tq, n_cast):
    cast_in, (o_ref,), cast_out, (kf_ref, vf_ref) = (
        rest[:n_cast], rest[n_cast:n_cast + 1], rest[n_cast + 1:2 * n_cast + 1], rest[2 * n_cast + 1:])
    _cast_rows(cast_in, cast_out)
    seq = qn_ref.shape[2]
    row = lax.broadcasted_iota(jnp.int32, (tq, tq), 0)
    col = lax.broadcasted_iota(jnp.int32, (tq, tq), 1)
    causal = row >= col
    neg = jnp.finfo(F32).min
    heads = qn_ref.shape[1]
    for hh in range(heads):
        parity = hh % 2
        kf_ref[hh, :, 0:LANES] = kn_ref[0, hh]
        kf_ref[hh, :, LANES:2 * LANES] = kpe_ref[0, :, parity * LANES:(parity + 1) * LANES]
        vf_ref[hh, :, 0:LANES] = v_ref[0, hh]
        vf_ref[hh, :, LANES:2 * LANES] = jnp.ones((seq, LANES), vf_ref.dtype)

    nt = (((1,), (1,)), ((), ()))
    def scores(pair, hh):
        q0 = pair * 2 * tq
        k1, k2 = q0 + tq, q0 + 2 * tq
        q = jnp.concatenate([qn_ref[0, hh, q0:k2, :], qpe_ref[0, hh // 2, q0:k2, :]], axis=1)
        return (lax.dot_general(q, kf_ref[hh, 0:k1, :], nt, preferred_element_type=F32),
                lax.dot_general(q[tq:], kf_ref[hh, k1:k2, :], nt, preferred_element_type=F32))

    def finish(pair, hh, s_main, s_ext):
        q0 = pair * 2 * tq
        k1, k2 = q0 + tq, q0 + 2 * tq
        top = s_main[0:tq]
        top_diag = jnp.where(causal, top[:, q0:k1], neg)
        top = jnp.concatenate([top[:, 0:q0], top_diag], axis=1) if pair else top_diag
        bot = s_main[tq:]
        ext = jnp.where(causal, s_ext, neg)
        m_top = jnp.max(top, axis=-1, keepdims=True)
        m_bot = jnp.maximum(jnp.max(bot, axis=-1, keepdims=True),
                            jnp.max(ext, axis=-1, keepdims=True))
        p_main = jnp.concatenate([jnp.exp2(top - m_top), jnp.exp2(bot - m_bot)], axis=0)
        acc = _dot(p_main.astype(BF16), vf_ref[hh, 0:k1, :])
        acc_bot = acc[tq:] + _dot(jnp.exp2(ext - m_bot).astype(BF16), vf_ref[hh, k1:k2, :])
        lanes = slice(hh * LANES, (hh + 1) * LANES)
        o_ref[0, q0:k1, lanes] = (acc[0:tq, 0:LANES] / acc[0:tq, LANES:]).astype(o_ref.dtype)
        o_ref[0, k1:k2, lanes] = (acc_bot[:, 0:LANES] / acc_bot[:, LANES:]).astype(o_ref.dtype)

    work = [(pair, hh) for pair in reversed(range(seq // (2 * tq))) for hh in range(heads)]
    lead = 1
    pending = [scores(*w) for w in work[:lead]]
    for idx, (pair, hh) in enumerate(work):
        if idx + lead < len(work):
            pending.append(scores(*work[idx + lead]))
        finish(pair, hh, *pending.pop(0))


def _attention(q_nope, q_pe, k_nope, k_pe, v, casts, tq, hb):
    B, H, S, _ = q_nope.shape
    assert hb % 2 == 0, "head pairs share a rope tile"
    groups = H // hb
    specs = [_cast_block_specs(n, w.shape[1], r0, B * groups, lambda b, g: b * groups + g)
             for w, r0, n in casts]
    outs = pl.pallas_call(
        functools.partial(_attn_kernel, tq=tq, n_cast=len(casts)),
        grid=(B, groups),
        in_specs=[
            pl.BlockSpec((1, hb, S, LANES), lambda b, g: (b, g, 0, 0)),
            pl.BlockSpec((1, hb // 2, S, LANES), lambda b, g: (b, g, 0, 0)),
            pl.BlockSpec((1, hb, S, LANES), lambda b, g: (b, g, 0, 0)),
            pl.BlockSpec((1, S, 2 * LANES), lambda b, g: (b, 0, 0)),
            pl.BlockSpec((1, hb, S, LANES), lambda b, g: (b, g, 0, 0)),
        ] + [s_in for s_in, _ in specs],
        out_specs=[pl.BlockSpec((1, S, hb * LANES), lambda b, g: (b, 0, g))]
        + [s_out for _, s_out in specs],
        out_shape=[jax.ShapeDtypeStruct((B, S, H * V_HEAD_DIM), BF16)]
        + [jax.ShapeDtypeStruct((n, w.shape[1]), BF16) for w, _, n in casts],
        scratch_shapes=[pltpu.VMEM((hb, S, 2 * LANES), BF16), pltpu.VMEM((hb, S, 2 * LANES), BF16)],
        compiler_params=_params(2),
        name="mla_attention",
    )(q_nope, q_pe, k_nope, k_pe, v, *[w for w, _, _ in casts])
    return outs[0], outs[1:]


def _merge_kernel(attn_ref, a_ref, m_ref, x_ref, woa_ref, wg0_ref, bg0_ref, wout_ref, fg_ref,
                  cast_ref, h_ref, f_ref, cast_out_ref):
    _cast_rows([cast_ref], [cast_out_ref])
    y_attn = _dot(attn_ref[...], woa_ref[...])
    gate = jax.nn.sigmoid(_dot_t(a_ref[...], wg0_ref[...]) + bg0_ref[...])
    merged = (gate * y_attn + m_ref[...]).astype(BF16)
    h = x_ref[...] + _dot(merged, wout_ref[...])
    h_ref[...] = h
    f_ref[...] = _rms(h, fg_ref[...]).astype(BF16)


def _merge(attn, a, m_sgu, x, w_o_attn, w_g0, b_g0, w_out, ffn_g, cast_w, tm):
    T, D = x.shape
    row = lambda i: (i, 0)
    tile = pl.BlockSpec((tm, D), row)
    cast_in, cast_out = _cast_block_specs(*cast_w.shape, 0, T // tm, lambda i: i)
    return pl.pallas_call(
        _merge_kernel,
        grid=(T // tm,),
        in_specs=[tile, tile, tile, tile,
                  _resident(w_o_attn.shape), _resident(w_g0.shape), _resident(b_g0.shape),
                  _resident(w_out.shape), _resident(ffn_g.shape), cast_in],
        out_specs=[tile, tile, cast_out],
        out_shape=[jax.ShapeDtypeStruct((T, D), F32), jax.ShapeDtypeStruct((T, D), BF16),
                   jax.ShapeDtypeStruct(cast_w.shape, BF16)],
        compiler_params=_params(1),
        name="merge_outproj",
    )(attn, a, m_sgu, x, w_o_attn, w_g0, b_g0, w_out, ffn_g, cast_w)


def _ffn_kernel(f_ref, h_ref, wg_ref, wu_ref, wd_ref, ng_ref, o_ref, *, sub):
    j = pl.program_id(1)

    f = f_ref[...]
    starts = list(range(0, wg_ref.shape[1], sub))
    projected = [(_dot(f, wg_ref[:, c0:c0 + sub]), _dot(f, wu_ref[:, c0:c0 + sub]))
                 for c0 in starts]
    act = jnp.concatenate(
        [(jax.nn.silu(gate) * up).astype(BF16) for gate, up in projected], axis=1)
    o_ref[...] = jnp.where(j == 0, h_ref[...], o_ref[...]) + _dot(act, wd_ref[...])

    @pl.when(j == pl.num_programs(1) - 1)
    def _():
        o_ref[...] = _rms(o_ref[...], ng_ref[...])


def _ffn(f, h, w_gate, w_up, w_down, final_g, tm, tf, sub):
    T, D = h.shape
    d_ff = w_gate.shape[1]
    row = lambda i, j: (i, 0)
    n_i, n_j = T // tm, d_ff // tf
    h_row = lambda i, j: (jnp.minimum(i + (j >= n_j // 2).astype(jnp.int32), n_i - 1), 0)
    return pl.pallas_call(
        functools.partial(_ffn_kernel, sub=sub),
        grid=(n_i, n_j),
        in_specs=[
            pl.BlockSpec((tm, D), row),
            pl.BlockSpec((tm, D), h_row),
            pl.BlockSpec((D, tf), lambda i, j: (0, j)),
            pl.BlockSpec((D, tf), lambda i, j: (0, j)),
            pl.BlockSpec((tf, D), lambda i, j: (j, 0)),
            _resident((1, D)),
        ],
        out_specs=pl.BlockSpec((tm, D), row),
        out_shape=jax.ShapeDtypeStruct((T, D), F32),
        compiler_params=_params(2),
        name="swiglu_ffn",
    )(f, h, w_gate, w_up, w_down, final_g)


def kernel(x, positions, norm_mix_g, w_in, b_gate, q_norm_g, w_uq, kv_norm_g, w_ukv, w_o_attn,
           sgu_norm_g, w_sgu, b_sgu, w_o_sgu, w_out, norm_ffn_g, w_gate_ffn, w_up_ffn,
           w_down_ffn, norm_final_g):
    B, S, D = x.shape
    T = B * S
    depth = w_in.shape[0]
    assert depth == 1, "the final norm is fused into the FFN epilogue of a single layer"
    assert w_in.shape[1:] == (D, D_IN)

    row_vec = lambda v: v.reshape(1, -1).astype(F32)

    h = x
    out = None
    for l in range(depth):
        w_in_t = jnp.swapaxes(w_in[l], 0, 1)
        cs, w_lat_t, w_uq_p, w_ukv_p = _prep_inproj(positions, w_in_t, w_uq[l], w_ukv[l])
        cs = cs.reshape(B, S, LANES)
        b_full = jnp.repeat(b_sgu[l].T, SGU_GROUP_DIM, axis=1).astype(F32)

        a, q_nope, q_pe, k_nope, v, k_pe = _inproj(
            h, row_vec(norm_mix_g[l]), cs, w_lat_t, row_vec(q_norm_g[l]), row_vec(kv_norm_g[l]),
            w_uq_p, w_ukv_p, tm=INPROJ_ROWS, sub=ROW_SUB)
        a2 = a.reshape(T, D)
        whole = lambda w: (w, 0, w.shape[0])
        attn, (w_uv_t, w_g0_t, w_g1_t, w_os, w_oa, w_o, w_uf) = _attention(
            q_nope, q_pe, k_nope, k_pe, v,
            [(w_in_t, UV_OFF, 2 * SGU_WIDTH), (w_in_t, GATE_OFF, D), (w_in_t, GATE_OFF + D, D),
             whole(w_o_sgu[l]), whole(w_o_attn[l]), whole(w_out[l]), whole(w_up_ffn[l])],
            tq=ATTN_Q_ROWS, hb=ATTN_HEADS_PER_STEP)
        m_sgu, w_gf = _sgu_branch(
            a2, w_uv_t, row_vec(sgu_norm_g[l]), w_sgu[l], b_full, w_os, w_g1_t,
            row_vec(b_gate[l, D:]), w_gate_ffn[l], tm=SGU_ROWS, sub=ROW_SUB)
        h_mid, f, w_df = _merge(
            attn.reshape(T, D), a2, m_sgu, h.reshape(T, D), w_oa, w_g0_t,
            row_vec(b_gate[l, :D]), w_o, row_vec(norm_ffn_g[l]), w_down_ffn[l], tm=MERGE_ROWS)
        out = _ffn(f, h_mid, w_gf, w_uf, w_df, row_vec(norm_final_g), tm=FFN_ROWS,
                   tf=FFN_COLS, sub=FFN_COL_SUB)
        h = out.reshape(B, S, D)
    return h
```

```python
import functools

import jax
import jax.numpy as jnp
from jax import lax
from jax.experimental import pallas as pl
from jax.experimental.pallas import tpu as pltpu

D_MODEL = 2048
N_HEADS = 16
QK_NOPE_DIM = 128
QK_ROPE_DIM = 64
V_HEAD_DIM = 128
Q_LORA_RANK = 512
KV_LORA_RANK = 512
ROPE_THETA = 10000.0
SGU_GROUPS = 8
SGU_GROUP_DIM = 128
SGU_WIDTH = SGU_GROUPS * SGU_GROUP_DIM
CHUNK = 128
N_BRANCH = 2
RMS_EPS = 1e-6
KPE_OFF = Q_LORA_RANK + KV_LORA_RANK
UV_OFF = KPE_OFF + QK_ROPE_DIM
GATE_OFF = UV_OFF + 2 * SGU_WIDTH
D_IN = GATE_OFF + N_BRANCH * D_MODEL
LANES = 128
HALF_ROPE = QK_ROPE_DIM // 2
LOG2_E = 1.4426950408889634
QK_LOG2_SCALE = (QK_NOPE_DIM + QK_ROPE_DIM) ** -0.5 * LOG2_E
BF16_SUBLANES = 16

VMEM_LIMIT_BYTES = 60 * 1024 * 1024

INPROJ_ROWS = 512
SGU_ROWS = 512
MERGE_ROWS = 512
ROW_SUB = 256
ATTN_Q_ROWS = 256
ATTN_HEADS_PER_STEP = 2
FFN_ROWS = 1024
FFN_COLS = 512
FFN_COL_SUB = 256
PREP_STEPS = 8
PREP_RING_SLOTS = 4

F32 = jnp.float32
BF16 = jnp.bfloat16


def _rms(x, g):
    return x * lax.rsqrt(jnp.mean(x * x, axis=-1, keepdims=True) + RMS_EPS) * g


def _dot(a, b):
    return jnp.dot(a, b, preferred_element_type=F32)


def _dot_t(a, b_t):
    return lax.dot_general(a, b_t, (((1,), (1,)), ((), ())), preferred_element_type=F32)


def _resident(shape):
    return pl.BlockSpec(shape, lambda *_: (0,) * len(shape), pipeline_mode=pl.Buffered(1))


def _params(n_axes):
    return pltpu.CompilerParams(
        dimension_semantics=("arbitrary",) * n_axes, vmem_limit_bytes=VMEM_LIMIT_BYTES)


def _cast_block_specs(n_rows, n_cols, row0, n_steps, linear_step):
    share = 1 if (n_rows // n_steps) % BF16_SUBLANES == 0 else 2
    blk = n_rows * share // n_steps
    assert blk * n_steps == n_rows * share and blk % BF16_SUBLANES == 0 and row0 % blk == 0
    first = row0 // blk
    return (pl.BlockSpec((blk, n_cols), lambda *idx: (first + linear_step(*idx) // share, 0)),
            pl.BlockSpec((blk, n_cols), lambda *idx: (linear_step(*idx) // share, 0)))


def _cast_rows(srcs, dsts):
    for src, dst in zip(srcs, dsts):
        dst[...] = src[...].astype(dst.dtype)


def _rope_table_rows(pos_ref, freq_ref, cs_ref):
    ang = pos_ref[...] * freq_ref[...]
    cos, sin = jnp.cos(ang), jnp.sin(ang)
    n_rows = ang.shape[0]
    per_row = LANES // HALF_ROPE
    lane = lax.broadcasted_iota(jnp.int32, ang.shape, 1)

    def lanes_from(x, src, dst):
        shift = (dst - src) % LANES
        return pltpu.roll(x, shift, 1) if shift else x

    for k in range(per_row):
        src = k * HALF_ROPE
        row = jnp.where(
            lane < HALF_ROPE, lanes_from(cos, src, 0),
            jnp.where(lane < 2 * HALF_ROPE, lanes_from(cos, src, HALF_ROPE),
                      jnp.where(lane < 3 * HALF_ROPE, -lanes_from(sin, src, 2 * HALF_ROPE),
                                lanes_from(sin, src, 3 * HALF_ROPE))))
        cs_ref[pl.ds(k, n_rows, stride=per_row), :] = row


def _rope_table_inputs(positions):
    n_tok = positions.size
    per_row = LANES // HALF_ROPE
    inv_freq = ROPE_THETA ** (-jnp.arange(0, QK_ROPE_DIM, 2, dtype=F32) / QK_ROPE_DIM)
    pos_rep = jnp.repeat(positions.astype(F32).reshape(n_tok // per_row, per_row), HALF_ROPE, axis=1)
    return pos_rep, jnp.tile(inv_freq, per_row).reshape(1, LANES)


def _rope_dup(x, cs):
    y = x * cs
    return y + pltpu.roll(y, LANES // 2, 1)


def _rope_pair(p, cos4, sin4):
    lane = lax.broadcasted_iota(jnp.int32, p.shape, 1)
    first_half = lane % QK_ROPE_DIM < HALF_ROPE
    partner = jnp.where(first_half, pltpu.roll(p, LANES - HALF_ROPE, 1), pltpu.roll(p, HALF_ROPE, 1))
    return p * cos4 + partner * sin4


def _prep_kernel(pos_ref, freq_ref, wlat_ref, wuq_ref, wukv_ref,
                 cs_ref, olat_ref, ouq_ref, oukv_ref):
    _rope_table_rows(pos_ref, freq_ref, cs_ref)
    olat_ref[0:KPE_OFF] = wlat_ref[0:KPE_OFF].astype(BF16)
    x1 = wlat_ref[KPE_OFF:KPE_OFF + HALF_ROPE].astype(BF16)
    x2 = wlat_ref[KPE_OFF + HALF_ROPE:UV_OFF].astype(BF16)
    for k, part in enumerate((x1, x2, x2, x1)):
        olat_ref[KPE_OFF + k * HALF_ROPE:KPE_OFF + (k + 1) * HALF_ROPE] = part
    oukv_ref[...] = wukv_ref[...].astype(BF16)
    half = LANES // 2
    nope_cols = N_HEADS * QK_NOPE_DIM
    lane = lax.broadcasted_iota(jnp.int32, (wuq_ref.shape[0], LANES), 1)
    for pair in range(N_HEADS // 2):
        t0, t1, t2 = (wuq_ref[:, (3 * pair + k) * LANES:(3 * pair + k + 1) * LANES] for k in range(3))
        nope_odd = jnp.where(lane < half, pltpu.roll(t1, half, 1), pltpu.roll(t2, half, 1))
        for h, nope in ((2 * pair, t0), (2 * pair + 1, nope_odd)):
            ouq_ref[:, h * LANES:(h + 1) * LANES] = nope.astype(BF16)
        ouq_ref[:, nope_cols + pair * LANES:nope_cols + (pair + 1) * LANES] = (
            jnp.where(lane < half, t1, t2).astype(BF16))


def _inproj_kernel(pos_hbm, freq_ref, win_hbm, wuq_hbm, wukv_hbm, x_ref, g_ref, qg_ref, kvg_ref,
                   a_ref, qn_ref, qpe_ref, kn_ref, v_ref, kpe_ref,
                   cs_scr, wlat_scr, wuq_scr, wukv_scr, pos_st, wlat_st, wuq_st, wukv_st, sems,
                   *, sub, prep_steps):
    s = pl.program_id(0)
    tm = x_ref.shape[1]
    n_slots = pos_st.shape[0]
    n_pos = pos_st.shape[1]
    n_tab = cs_scr.shape[0] // prep_steps
    n_lat = wlat_scr.shape[1] // prep_steps
    n_q = wuq_scr.shape[0] // prep_steps
    n_kv = wukv_scr.shape[0] // prep_steps

    def chunk_copies(c):
        slot = c % n_slots
        pairs = (
            (pos_hbm.at[c * n_pos:(c + 1) * n_pos], pos_st.at[slot]),
            (win_hbm.at[0:UV_OFF, c * n_lat:(c + 1) * n_lat], wlat_st.at[slot]),
            (wuq_hbm.at[c * n_q:(c + 1) * n_q], wuq_st.at[slot]),
            (wukv_hbm.at[c * n_kv:(c + 1) * n_kv], wukv_st.at[slot]),
        )
        return [pltpu.make_async_copy(src, dst, sems.at[k, slot])
                for k, (src, dst) in enumerate(pairs)]

    for c in range(prep_steps):
        @pl.when(s == c)
        def _(c=c):
            if c == 0:
                for ahead in range(min(n_slots, prep_steps)):
                    for copy in chunk_copies(ahead):
                        copy.start()
            for copy in chunk_copies(c):
                copy.wait()
            slot = c % n_slots
            _prep_kernel(pos_st.at[slot], freq_ref, wlat_st.at[slot], wuq_st.at[slot],
                         wukv_st.at[slot],
                         cs_scr.at[c * n_tab:(c + 1) * n_tab],
                         wlat_scr.at[:, c * n_lat:(c + 1) * n_lat],
                         wuq_scr.at[c * n_q:(c + 1) * n_q],
                         wukv_scr.at[c * n_kv:(c + 1) * n_kv])
            if c + n_slots < prep_steps:
                for copy in chunk_copies(c + n_slots):
                    copy.start()

    @pl.when(s >= prep_steps)
    def _():
        row0 = pl.multiple_of((s - prep_steps) * tm, tm)
        _inproj_tile(x_ref, g_ref, cs_scr.at[pl.ds(row0, tm)], wlat_scr, qg_ref, kvg_ref,
                     wuq_scr, wukv_scr, a_ref, qn_ref, qpe_ref, kn_ref, v_ref, kpe_ref, sub=sub)


def _inproj_tile(x_ref, g_ref, cs_ref, wlat_ref, qg_ref, kvg_ref, wuq_ref, wukv_ref,
                 a_ref, qn_ref, qpe_ref, kn_ref, v_ref, kpe_ref, *, sub):
    def latents(r0):
        a = _rms(x_ref[0, r0:r0 + sub, :], g_ref[...]).astype(BF16)
        a_ref[0, r0:r0 + sub, :] = a
        return _dot_t(a, wlat_ref[...])

    starts = list(range(0, x_ref.shape[1], sub))
    for r0, z in zip(starts, [latents(r0) for r0 in starts]):
        rows = slice(r0, r0 + sub)
        qn = (_rms(z[:, :Q_LORA_RANK], qg_ref[...]) * QK_LOG2_SCALE).astype(BF16)
        kvn = _rms(z[:, Q_LORA_RANK:Q_LORA_RANK + KV_LORA_RANK], kvg_ref[...]).astype(BF16)
        cs = cs_ref[rows, :]
        kpe = _rope_dup(z[:, Q_LORA_RANK + KV_LORA_RANK:], cs)
        lane = lax.broadcasted_iota(jnp.int32, kpe.shape, 1)
        low = lane < QK_ROPE_DIM
        kpe_ref[0, rows, 0:LANES] = jnp.where(low, kpe, 0.0).astype(BF16)
        kpe_ref[0, rows, LANES:2 * LANES] = jnp.where(low, 0.0, kpe).astype(BF16)
        cs_swapped = pltpu.roll(cs, LANES // 2, 1)
        cos4 = jnp.where(low, cs, cs_swapped)
        sin4 = jnp.where(low, cs_swapped, cs)

        heads_per_dot = 4
        width = heads_per_dot * LANES
        nope_cols = N_HEADS * QK_NOPE_DIM
        for hg in range(N_HEADS // heads_per_dot):
            c0 = hg * width
            q_nope = _dot(qn, wuq_ref[:, c0:c0 + width])
            pe0 = nope_cols + c0 // 2
            q_pe = _dot(qn, wuq_ref[:, pe0:pe0 + width // 2])
            kv0 = 2 * c0
            kv_a = _dot(kvn, wukv_ref[:, kv0:kv0 + width])
            kv_b = _dot(kvn, wukv_ref[:, kv0 + width:kv0 + 2 * width])
            for pp in range(heads_per_dot // 2):
                pair = hg * (heads_per_dot // 2) + pp
                qpe_ref[0, pair, rows, :] = _rope_pair(
                    q_pe[:, pp * LANES:(pp + 1) * LANES], cos4, sin4).astype(BF16)
            for hh in range(heads_per_dot):
                h = hg * heads_per_dot + hh
                sl = slice(hh * LANES, (hh + 1) * LANES)
                qn_ref[0, h, rows, :] = q_nope[:, sl].astype(BF16)
                kv = kv_a if hh < heads_per_dot // 2 else kv_b
                k0 = (hh % (heads_per_dot // 2)) * 2 * LANES
                kn_ref[0, h, rows, :] = kv[:, k0:k0 + LANES].astype(BF16)
                v_ref[0, h, rows, :] = kv[:, k0 + LANES:k0 + 2 * LANES].astype(BF16)


def _inproj(x, norm_g, positions, w_in_t, q_g, kv_g, w_uq, w_ukv, tm, sub):
    B, S, D = x.shape
    P = PREP_STEPS
    tiles = S // tm
    r_q, r_kv = w_uq.shape[0], w_ukv.shape[0]
    lat_rows = UV_OFF + LANES - QK_ROPE_DIM
    uq_cols = N_HEADS * QK_NOPE_DIM + (N_HEADS // 2) * LANES
    pos_rep, freq = _rope_table_inputs(positions)
    tile = lambda s: jnp.maximum(s - P, 0)
    row = lambda s: (tile(s) // tiles, tile(s) % tiles, 0)
    head = lambda s: (tile(s) // tiles, 0, tile(s) % tiles, 0)
    in_hbm = pl.BlockSpec(memory_space=pl.ANY)
    slots = PREP_RING_SLOTS
    return pl.pallas_call(
        functools.partial(_inproj_kernel, sub=sub, prep_steps=P),
        grid=(P + B * tiles,),
        in_specs=[
            in_hbm,
            _resident((1, LANES)),
            in_hbm,
            in_hbm,
            in_hbm,
            pl.BlockSpec((1, tm, D), row),
            _resident((1, D)),
            _resident((1, Q_LORA_RANK)),
            _resident((1, KV_LORA_RANK)),
        ],
        out_specs=[
            pl.BlockSpec((1, tm, D), row),
            pl.BlockSpec((1, N_HEADS, tm, LANES), head),
            pl.BlockSpec((1, N_HEADS // 2, tm, LANES), head),
            pl.BlockSpec((1, N_HEADS, tm, LANES), head),
            pl.BlockSpec((1, N_HEADS, tm, LANES), head),
            pl.BlockSpec((1, tm, 2 * LANES), row),
        ],
        out_shape=[
            jax.ShapeDtypeStruct((B, S, D), BF16),
            jax.ShapeDtypeStruct((B, N_HEADS, S, LANES), BF16),
            jax.ShapeDtypeStruct((B, N_HEADS // 2, S, LANES), BF16),
            jax.ShapeDtypeStruct((B, N_HEADS, S, LANES), BF16),
            jax.ShapeDtypeStruct((B, N_HEADS, S, LANES), BF16),
            jax.ShapeDtypeStruct((B, S, 2 * LANES), BF16),
        ],
        scratch_shapes=[
            pltpu.VMEM((positions.size, LANES), F32),
            pltpu.VMEM((lat_rows, D), BF16),
            pltpu.VMEM((r_q, uq_cols), BF16),
            pltpu.VMEM(w_ukv.shape, BF16),
            pltpu.VMEM((slots, pos_rep.shape[0] // P, LANES), F32),
            pltpu.VMEM((slots, UV_OFF, D // P), F32),
            pltpu.VMEM((slots, r_q // P, w_uq.shape[1]), F32),
            pltpu.VMEM((slots, r_kv // P, w_ukv.shape[1]), F32),
            pltpu.SemaphoreType.DMA((4, slots)),
        ],
        compiler_params=_params(1),
        name="inproj",
    )(pos_rep, freq, w_in_t, w_uq, w_ukv, x, norm_g, q_g, kv_g)


def _sgu_kernel(a_ref, wuv_ref, sg_ref, ws_ref, bfull_ref, wos_ref, wg1_ref, bg1_ref, cast_ref,
                m_ref, cast_out_ref, *, sub):
    _cast_rows([cast_ref], [cast_out_ref])
    n_chunks = sub // CHUNK
    t_idx = lax.broadcasted_iota(jnp.int32, (CHUNK, CHUNK), 0)
    s_idx = lax.broadcasted_iota(jnp.int32, (CHUNK, CHUNK), 1)
    causal = t_idx >= s_idx
    ws = [jnp.where(causal, ws_ref[g], 0.0).astype(BF16) for g in range(SGU_GROUPS)]
    bfull = bfull_ref[...]
    def gating_unit(uv_raw):
        uv = jax.nn.gelu(uv_raw)
        u = uv[:, :SGU_WIDTH]
        vn = _rms(uv[:, SGU_WIDTH:], sg_ref[...]).astype(BF16)
        mixed_cols = []
        for g in range(SGU_GROUPS):
            gs = slice(g * SGU_GROUP_DIM, (g + 1) * SGU_GROUP_DIM)
            rhs = jnp.concatenate(
                [vn[c * CHUNK:(c + 1) * CHUNK, gs] for c in range(n_chunks)], axis=1)
            mixed_cols.append(_dot(ws[g], rhs))
        rows = []
        for c in range(n_chunks):
            cs = slice(c * SGU_GROUP_DIM, (c + 1) * SGU_GROUP_DIM)
            mixed = jnp.concatenate([mixed_cols[g][:, cs] for g in range(SGU_GROUPS)], axis=1)
            rows.append(u[c * CHUNK:(c + 1) * CHUNK] * (mixed + bfull))
        return jnp.concatenate(rows, axis=0).astype(BF16)

    a = a_ref[...]
    uv_raw = _dot_t(a, wuv_ref[...])
    gate_raw = _dot_t(a, wg1_ref[...])
    for r0 in range(0, a_ref.shape[0], sub):
        rows = slice(r0, r0 + sub)
        y_sgu = _dot(gating_unit(uv_raw[rows]), wos_ref[...])
        m_ref[rows, :] = jax.nn.sigmoid(gate_raw[rows] + bg1_ref[...]) * y_sgu


def _sgu_branch(a, w_uv, sgu_g, w_s, b_full, w_o_sgu, w_g1, b_g1, cast_w, tm, sub):
    T, D = a.shape
    row = lambda i: (i, 0)
    cast_in, cast_out = _cast_block_specs(*cast_w.shape, 0, T // tm, lambda i: i)
    return pl.pallas_call(
        functools.partial(_sgu_kernel, sub=sub),
        grid=(T // tm,),
        in_specs=[
            pl.BlockSpec((tm, D), row),
            _resident(w_uv.shape),
            _resident(sgu_g.shape),
            _resident(w_s.shape),
            _resident(b_full.shape),
            _resident(w_o_sgu.shape),
            _resident(w_g1.shape),
            _resident(b_g1.shape),
            cast_in,
        ],
        out_specs=[pl.BlockSpec((tm, D), row), cast_out],
        out_shape=[jax.ShapeDtypeStruct((T, D), F32), jax.ShapeDtypeStruct(cast_w.shape, BF16)],
        compiler_params=_params(1),
        name="sgu_branch",
    )(a, w_uv, sgu_g, w_s, b_full, w_o_sgu, w_g1, b_g1, cast_w)


def _attn_kernel(qn_ref, qpe_ref, kn_ref, kpe_ref, v_ref, *rest, tq, n_cast):
    cast_in, (o_ref,), cast_out, (kf_ref, vf_ref) = (
        rest[:n_cast], rest[n_cast:n_cast + 1], rest[n_cast + 1:2 * n_cast + 1], rest[2 * n_cast + 1:])
    _cast_rows(cast_in, cast_out)
    seq = qn_ref.shape[2]
    row = lax.broadcasted_iota(jnp.int32, (tq, tq), 0)
    col = lax.broadcasted_iota(jnp.int32, (tq, tq), 1)
    causal = row >= col
    neg = jnp.finfo(F32).min
    heads = qn_ref.shape[1]
    for hh in range(heads):
        parity = hh % 2
        kf_ref[hh, :, 0:LANES] = kn_ref[0, hh]
        kf_ref[hh, :, LANES:2 * LANES] = kpe_ref[0, :, parity * LANES:(parity + 1) * LANES]
        vf_ref[hh, :, 0:LANES] = v_ref[0, hh]
        vf_ref[hh, :, LANES:2 * LANES] = jnp.ones((seq, LANES), vf_ref.dtype)

    nt = (((1,), (1,)), ((), ()))
    def scores(pair, hh):
        q0 = pair * 2 * tq
        k1, k2 = q0 + tq, q0 + 2 * tq
        q = jnp.concatenate([qn_ref[0, hh, q0:k2, :], qpe_ref[0, hh // 2, q0:k2, :]], axis=1)
        return (lax.dot_general(q, kf_ref[hh, 0:k1, :], nt, preferred_element_type=F32),
                lax.dot_general(q[tq:], kf_ref[hh, k1:k2, :], nt, preferred_element_type=F32))

    def finish(pair, hh, s_main, s_ext):
        q0 = pair * 2 * tq
        k1, k2 = q0 + tq, q0 + 2 * tq
        top = s_main[0:tq]
        top_diag = jnp.where(causal, top[:, q0:k1], neg)
        top = jnp.concatenate([top[:, 0:q0], top_diag], axis=1) if pair else top_diag
        bot = s_main[tq:]
        ext = jnp.where(causal, s_ext, neg)
        m_top = jnp.max(top, axis=-1, keepdims=True)
        m_bot = jnp.maximum(jnp.max(bot, axis=-1, keepdims=True),
                            jnp.max(ext, axis=-1, keepdims=True))
        p_main = jnp.concatenate([jnp.exp2(top - m_top), jnp.exp2(bot - m_bot)], axis=0)
        acc = _dot(p_main.astype(BF16), vf_ref[hh, 0:k1, :])
        acc_bot = acc[tq:] + _dot(jnp.exp2(ext - m_bot).astype(BF16), vf_ref[hh, k1:k2, :])
        lanes = slice(hh * LANES, (hh + 1) * LANES)
        o_ref[0, q0:k1, lanes] = (acc[0:tq, 0:LANES] / acc[0:tq, LANES:]).astype(o_ref.dtype)
        o_ref[0, k1:k2, lanes] = (acc_bot[:, 0:LANES] / acc_bot[:, LANES:]).astype(o_ref.dtype)

    work = [(pair, hh) for pair in reversed(range(seq // (2 * tq))) for hh in range(heads)]
    lead = 2
    pending = [scores(*w) for w in work[:lead]]
    for idx, (pair, hh) in enumerate(work):
        if idx + lead < len(work):
            pending.append(scores(*work[idx + lead]))
        finish(pair, hh, *pending.pop(0))


def _attention(q_nope, q_pe, k_nope, k_pe, v, casts, tq, hb):
    B, H, S, _ = q_nope.shape
    assert hb % 2 == 0, "head pairs share a rope tile"
    groups = H // hb
    specs = [_cast_block_specs(n, w.shape[1], r0, B * groups, lambda b, g: b * groups + g)
             for w, r0, n in casts]
    outs = pl.pallas_call(
        functools.partial(_attn_kernel, tq=tq, n_cast=len(casts)),
        grid=(B, groups),
        in_specs=[
            pl.BlockSpec((1, hb, S, LANES), lambda b, g: (b, g, 0, 0)),
            pl.BlockSpec((1, hb // 2, S, LANES), lambda b, g: (b, g, 0, 0)),
            pl.BlockSpec((1, hb, S, LANES), lambda b, g: (b, g, 0, 0)),
            pl.BlockSpec((1, S, 2 * LANES), lambda b, g: (b, 0, 0)),
            pl.BlockSpec((1, hb, S, LANES), lambda b, g: (b, g, 0, 0)),
        ] + [s_in for s_in, _ in specs],
        out_specs=[pl.BlockSpec((1, S, hb * LANES), lambda b, g: (b, 0, g))]
        + [s_out for _, s_out in specs],
        out_shape=[jax.ShapeDtypeStruct((B, S, H * V_HEAD_DIM), BF16)]
        + [jax.ShapeDtypeStruct((n, w.shape[1]), BF16) for w, _, n in casts],
        scratch_shapes=[pltpu.VMEM((hb, S, 2 * LANES), BF16), pltpu.VMEM((hb, S, 2 * LANES), BF16)],
        compiler_params=_params(2),
        name="mla_attention",
    )(q_nope, q_pe, k_nope, k_pe, v, *[w for w, _, _ in casts])
    return outs[0], outs[1:]


def _gate_merge_kernel(attn_ref, a_ref, m_ref, woa_ref, wg0_ref, bg0_ref, cast_ref,
                       merged_ref, cast_out_ref):
    _cast_rows([cast_ref], [cast_out_ref])
    y_attn = _dot(attn_ref[...], woa_ref[...])
    gate = jax.nn.sigmoid(_dot_t(a_ref[...], wg0_ref[...]) + bg0_ref[...])
    merged_ref[...] = (gate * y_attn + m_ref[...]).astype(BF16)


def _outproj_kernel(merged_ref, x_ref, wout_ref, fg_ref, h_ref, f_ref, *, sub):
    for r0 in range(0, x_ref.shape[0], sub):
        rows = slice(r0, r0 + sub)
        h = x_ref[rows, :] + _dot(merged_ref[rows, :], wout_ref[...])
        h_ref[rows, :] = h
        f_ref[rows, :] = _rms(h, fg_ref[...]).astype(BF16)


def _merge(attn, a, m_sgu, x, w_o_attn, w_g0, b_g0, w_out, ffn_g, cast_w, tm, sub):
    T, D = x.shape
    row = lambda i: (i, 0)
    tile = pl.BlockSpec((tm, D), row)
    cast_in, cast_out = _cast_block_specs(*cast_w.shape, 0, T // tm, lambda i: i)
    merged, cast_copy = pl.pallas_call(
        _gate_merge_kernel,
        grid=(T // tm,),
        in_specs=[tile, tile, tile, _resident(w_o_attn.shape), _resident(w_g0.shape),
                  _resident(b_g0.shape), cast_in],
        out_specs=[tile, cast_out],
        out_shape=[jax.ShapeDtypeStruct((T, D), BF16), jax.ShapeDtypeStruct(cast_w.shape, BF16)],
        compiler_params=_params(1),
        name="gate_merge",
    )(attn, a, m_sgu, w_o_attn, w_g0, b_g0, cast_w)
    h, f = pl.pallas_call(
        functools.partial(_outproj_kernel, sub=sub),
        grid=(T // tm,),
        in_specs=[tile, tile, _resident(w_out.shape), _resident(ffn_g.shape)],
        out_specs=[tile, tile],
        out_shape=[jax.ShapeDtypeStruct((T, D), F32), jax.ShapeDtypeStruct((T, D), BF16)],
        compiler_params=_params(1),
        name="outproj",
    )(merged, x, w_out, ffn_g)
    return h, f, cast_copy


def _ffn_kernel(f_ref, h_ref, wg_ref, wu_ref, wd_ref, ng_ref, o_ref, *, sub):
    j = pl.program_id(1)

    f = f_ref[...]
    starts = list(range(0, wg_ref.shape[1], sub))
    projected = [(_dot(f, wg_ref[:, c0:c0 + sub]), _dot(f, wu_ref[:, c0:c0 + sub]))
                 for c0 in starts]
    act = jnp.concatenate(
        [(jax.nn.silu(gate) * up).astype(BF16) for gate, up in projected], axis=1)
    o_ref[...] = jnp.where(j == 0, h_ref[...], o_ref[...]) + _dot(act, wd_ref[...])

    @pl.when(j == pl.num_programs(1) - 1)
    def _():
        o_ref[...] = _rms(o_ref[...], ng_ref[...])


def _ffn(f, h, w_gate, w_up, w_down, final_g, tm, tf, sub):
    T, D = h.shape
    d_ff = w_gate.shape[1]
    row = lambda i, j: (i, 0)
    n_i, n_j = T // tm, d_ff // tf
    h_row = lambda i, j: (jnp.minimum(i + (j >= n_j // 2).astype(jnp.int32), n_i - 1), 0)
    return pl.pallas_call(
        functools.partial(_ffn_kernel, sub=sub),
        grid=(n_i, n_j),
        in_specs=[
            pl.BlockSpec((tm, D), row),
            pl.BlockSpec((tm, D), h_row),
            pl.BlockSpec((D, tf), lambda i, j: (0, j)),
            pl.BlockSpec((D, tf), lambda i, j: (0, j)),
            pl.BlockSpec((tf, D), lambda i, j: (j, 0)),
            _resident((1, D)),
        ],
        out_specs=pl.BlockSpec((tm, D), row),
        out_shape=jax.ShapeDtypeStruct((T, D), F32),
        compiler_params=_params(2),
        name="swiglu_ffn",
    )(f, h, w_gate, w_up, w_down, final_g)


def kernel(x, positions, norm_mix_g, w_in, b_gate, q_norm_g, w_uq, kv_norm_g, w_ukv, w_o_attn,
           sgu_norm_g, w_sgu, b_sgu, w_o_sgu, w_out, norm_ffn_g, w_gate_ffn, w_up_ffn,
           w_down_ffn, norm_final_g):
    B, S, D = x.shape
    T = B * S
    depth = w_in.shape[0]
    assert depth == 1, "the final norm is fused into the FFN epilogue of a single layer"
    assert w_in.shape[1:] == (D, D_IN)

    row_vec = lambda v: v.reshape(1, -1).astype(F32)

    h = x
    out = None
    for l in range(depth):
        w_in_t = jnp.swapaxes(w_in[l], 0, 1)
        b_full = jnp.repeat(b_sgu[l].T, SGU_GROUP_DIM, axis=1).astype(F32)

        a, q_nope, q_pe, k_nope, v, k_pe = _inproj(
            h, row_vec(norm_mix_g[l]), positions, w_in_t, row_vec(q_norm_g[l]),
            row_vec(kv_norm_g[l]), w_uq[l], w_ukv[l], tm=INPROJ_ROWS, sub=ROW_SUB)
        a2 = a.reshape(T, D)
        whole = lambda w: (w, 0, w.shape[0])
        attn, (w_uv_t, w_g0_t, w_g1_t, w_os, w_oa, w_o, w_uf) = _attention(
            q_nope, q_pe, k_nope, k_pe, v,
            [(w_in_t, UV_OFF, 2 * SGU_WIDTH), (w_in_t, GATE_OFF, D), (w_in_t, GATE_OFF + D, D),
             whole(w_o_sgu[l]), whole(w_o_attn[l]), whole(w_out[l]), whole(w_up_ffn[l])],
            tq=ATTN_Q_ROWS, hb=ATTN_HEADS_PER_STEP)
        m_sgu, w_gf = _sgu_branch(
            a2, w_uv_t, row_vec(sgu_norm_g[l]), w_sgu[l], b_full, w_os, w_g1_t,
            row_vec(b_gate[l, D:]), w_gate_ffn[l], tm=SGU_ROWS, sub=ROW_SUB)
        h_mid, f, w_df = _merge(
            attn.reshape(T, D), a2, m_sgu, h.reshape(T, D), w_oa, w_g0_t,
            row_vec(b_gate[l, :D]), w_o, row_vec(norm_ffn_g[l]), w_down_ffn[l], tm=MERGE_ROWS,
            sub=ROW_SUB)
        out = _ffn(f, h_mid, w_gf, w_uf, w_df, row_vec(norm_final_g), tm=FFN_ROWS,
                   tf=FFN_COLS, sub=FFN_COL_SUB)
        h = out.reshape(B, S, D)
    return h
```

```python
import functools

import jax
import jax.numpy as jnp
from jax import lax
from jax.experimental import pallas as pl
from jax.experimental.pallas import tpu as pltpu

D_MODEL = 2048
N_HEADS = 16
QK_NOPE_DIM = 128
QK_ROPE_DIM = 64
V_HEAD_DIM = 128
Q_LORA_RANK = 512
KV_LORA_RANK = 512
ROPE_THETA = 10000.0
SGU_GROUPS = 8
SGU_GROUP_DIM = 128
SGU_WIDTH = SGU_GROUPS * SGU_GROUP_DIM
CHUNK = 128
N_BRANCH = 2
RMS_EPS = 1e-6
KPE_OFF = Q_LORA_RANK + KV_LORA_RANK
UV_OFF = KPE_OFF + QK_ROPE_DIM
GATE_OFF = UV_OFF + 2 * SGU_WIDTH
D_IN = GATE_OFF + N_BRANCH * D_MODEL
LANES = 128
HALF_ROPE = QK_ROPE_DIM // 2
LOG2_E = 1.4426950408889634
QK_LOG2_SCALE = (QK_NOPE_DIM + QK_ROPE_DIM) ** -0.5 * LOG2_E
BF16_SUBLANES = 16

VMEM_LIMIT_BYTES = 60 * 1024 * 1024

INPROJ_ROWS = 512
SGU_ROWS = 512
MERGE_ROWS = 256
ROW_SUB = 256
ATTN_Q_ROWS = 256
ATTN_HEADS_PER_STEP = 2
FFN_ROWS = 1024
FFN_COLS = 512
FFN_COL_SUB = 256
PREP_STEPS = 4

F32 = jnp.float32
BF16 = jnp.bfloat16


def _rms(x, g):
    return x * lax.rsqrt(jnp.mean(x * x, axis=-1, keepdims=True) + RMS_EPS) * g


def _dot(a, b):
    return jnp.dot(a, b, preferred_element_type=F32)


def _dot_t(a, b_t):
    return lax.dot_general(a, b_t, (((1,), (1,)), ((), ())), preferred_element_type=F32)


def _resident(shape):
    return pl.BlockSpec(shape, lambda *_: (0,) * len(shape), pipeline_mode=pl.Buffered(1))


def _params(n_axes):
    return pltpu.CompilerParams(
        dimension_semantics=("arbitrary",) * n_axes, vmem_limit_bytes=VMEM_LIMIT_BYTES)


def _cast_block_specs(n_rows, n_cols, row0, n_steps, linear_step):
    share = 1 if (n_rows // n_steps) % BF16_SUBLANES == 0 else 2
    blk = n_rows * share // n_steps
    assert blk * n_steps == n_rows * share and blk % BF16_SUBLANES == 0 and row0 % blk == 0
    first = row0 // blk
    return (pl.BlockSpec((blk, n_cols), lambda *idx: (first + linear_step(*idx) // share, 0)),
            pl.BlockSpec((blk, n_cols), lambda *idx: (linear_step(*idx) // share, 0)))


def _cast_rows(srcs, dsts):
    for src, dst in zip(srcs, dsts):
        dst[...] = src[...].astype(dst.dtype)


def _rope_table_rows(pos_ref, freq_ref, cs_ref):
    ang = pos_ref[...] * freq_ref[...]
    cos, sin = jnp.cos(ang), jnp.sin(ang)
    n_rows = ang.shape[0]
    per_row = LANES // HALF_ROPE
    lane = lax.broadcasted_iota(jnp.int32, ang.shape, 1)

    def lanes_from(x, src, dst):
        shift = (dst - src) % LANES
        return pltpu.roll(x, shift, 1) if shift else x

    for k in range(per_row):
        src = k * HALF_ROPE
        row = jnp.where(
            lane < HALF_ROPE, lanes_from(cos, src, 0),
            jnp.where(lane < 2 * HALF_ROPE, lanes_from(cos, src, HALF_ROPE),
                      jnp.where(lane < 3 * HALF_ROPE, -lanes_from(sin, src, 2 * HALF_ROPE),
                                lanes_from(sin, src, 3 * HALF_ROPE))))
        cs_ref[pl.ds(k, n_rows, stride=per_row), :] = row


def _rope_table_inputs(positions):
    n_tok = positions.size
    per_row = LANES // HALF_ROPE
    inv_freq = ROPE_THETA ** (-jnp.arange(0, QK_ROPE_DIM, 2, dtype=F32) / QK_ROPE_DIM)
    pos_rep = jnp.repeat(positions.astype(F32).reshape(n_tok // per_row, per_row), HALF_ROPE, axis=1)
    return pos_rep, jnp.tile(inv_freq, per_row).reshape(1, LANES)


def _rope_dup(x, cs):
    y = x * cs
    return y + pltpu.roll(y, LANES // 2, 1)


def _rope_pair(p, cos4, sin4):
    lane = lax.broadcasted_iota(jnp.int32, p.shape, 1)
    first_half = lane % QK_ROPE_DIM < HALF_ROPE
    partner = jnp.where(first_half, pltpu.roll(p, LANES - HALF_ROPE, 1), pltpu.roll(p, HALF_ROPE, 1))
    return p * cos4 + partner * sin4


def _prep_kernel(pos_ref, freq_ref, wlat_ref, wuq_ref, wukv_ref,
                 cs_ref, olat_ref, ouq_ref, oukv_ref):
    _rope_table_rows(pos_ref, freq_ref, cs_ref)
    olat_ref[0:KPE_OFF] = wlat_ref[0:KPE_OFF].astype(BF16)
    x1 = wlat_ref[KPE_OFF:KPE_OFF + HALF_ROPE].astype(BF16)
    x2 = wlat_ref[KPE_OFF + HALF_ROPE:UV_OFF].astype(BF16)
    for k, part in enumerate((x1, x2, x2, x1)):
        olat_ref[KPE_OFF + k * HALF_ROPE:KPE_OFF + (k + 1) * HALF_ROPE] = part
    oukv_ref[...] = wukv_ref[...].astype(BF16)
    half = LANES // 2
    nope_cols = N_HEADS * QK_NOPE_DIM
    lane = lax.broadcasted_iota(jnp.int32, (wuq_ref.shape[0], LANES), 1)
    for pair in range(N_HEADS // 2):
        t0, t1, t2 = (wuq_ref[:, (3 * pair + k) * LANES:(3 * pair + k + 1) * LANES] for k in range(3))
        nope_odd = jnp.where(lane < half, pltpu.roll(t1, half, 1), pltpu.roll(t2, half, 1))
        for h, nope in ((2 * pair, t0), (2 * pair + 1, nope_odd)):
            ouq_ref[:, h * LANES:(h + 1) * LANES] = nope.astype(BF16)
        ouq_ref[:, nope_cols + pair * LANES:nope_cols + (pair + 1) * LANES] = (
            jnp.where(lane < half, t1, t2).astype(BF16))


def _prep_inproj(positions, w_in_t, w_uq, w_ukv):
    n_chunks = PREP_STEPS
    d = w_in_t.shape[1]
    r_q, r_kv = w_uq.shape[0], w_ukv.shape[0]
    lat_rows = UV_OFF + LANES - QK_ROPE_DIM
    uq_cols = N_HEADS * QK_NOPE_DIM + (N_HEADS // 2) * LANES
    pos_rep, freq = _rope_table_inputs(positions)
    n_tok = positions.size
    return pl.pallas_call(
        _prep_kernel,
        grid=(n_chunks,),
        in_specs=[
            pl.BlockSpec((pos_rep.shape[0] // n_chunks, LANES), lambda c: (c, 0)),
            _resident((1, LANES)),
            pl.BlockSpec((UV_OFF, d // n_chunks), lambda c: (0, c)),
            pl.BlockSpec((r_q // n_chunks, w_uq.shape[1]), lambda c: (c, 0)),
            pl.BlockSpec((r_kv // n_chunks, w_ukv.shape[1]), lambda c: (c, 0)),
        ],
        out_specs=[
            pl.BlockSpec((n_tok // n_chunks, LANES), lambda c: (c, 0)),
            pl.BlockSpec((lat_rows, d // n_chunks), lambda c: (0, c)),
            pl.BlockSpec((r_q // n_chunks, uq_cols), lambda c: (c, 0)),
            pl.BlockSpec((r_kv // n_chunks, w_ukv.shape[1]), lambda c: (c, 0)),
        ],
        out_shape=[
            jax.ShapeDtypeStruct((n_tok, LANES), F32),
            jax.ShapeDtypeStruct((lat_rows, d), BF16),
            jax.ShapeDtypeStruct((r_q, uq_cols), BF16),
            jax.ShapeDtypeStruct(w_ukv.shape, BF16),
        ],
        compiler_params=_params(1),
        name="prep_inproj",
    )(pos_rep, freq, w_in_t, w_uq, w_ukv)


def _inproj_kernel(x_ref, g_ref, cs_ref, wlat_ref, qg_ref, kvg_ref, wuq_ref, wukv_ref,
                   a_ref, qn_ref, qpe_ref, kn_ref, v_ref, kpe_ref, *, sub):
    def latents(r0):
        a = _rms(x_ref[0, r0:r0 + sub, :], g_ref[...]).astype(BF16)
        a_ref[0, r0:r0 + sub, :] = a
        return _dot_t(a, wlat_ref[...])

    tm = x_ref.shape[1]
    z_all = jnp.concatenate([latents(r0) for r0 in range(0, tm, sub)], axis=0)
    for rows, z in ((slice(0, tm), z_all),):
        qn = (_rms(z[:, :Q_LORA_RANK], qg_ref[...]) * QK_LOG2_SCALE).astype(BF16)
        kvn = _rms(z[:, Q_LORA_RANK:Q_LORA_RANK + KV_LORA_RANK], kvg_ref[...]).astype(BF16)
        cs = cs_ref[0, rows, :]
        kpe = _rope_dup(z[:, Q_LORA_RANK + KV_LORA_RANK:], cs)
        lane = lax.broadcasted_iota(jnp.int32, kpe.shape, 1)
        low = lane < QK_ROPE_DIM
        kpe_ref[0, rows, 0:LANES] = jnp.where(low, kpe, 0.0).astype(BF16)
        kpe_ref[0, rows, LANES:2 * LANES] = jnp.where(low, 0.0, kpe).astype(BF16)
        cs_swapped = pltpu.roll(cs, LANES // 2, 1)
        cos4 = jnp.where(low, cs, cs_swapped)
        sin4 = jnp.where(low, cs_swapped, cs)

        heads_per_dot = 4
        width = heads_per_dot * LANES
        nope_cols = N_HEADS * QK_NOPE_DIM
        for hg in range(N_HEADS // heads_per_dot):
            c0 = hg * width
            q_nope = _dot(qn, wuq_ref[:, c0:c0 + width])
            pe0 = nope_cols + c0 // 2
            q_pe = _dot(qn, wuq_ref[:, pe0:pe0 + width // 2])
            kv0 = 2 * c0
            kv_a = _dot(kvn, wukv_ref[:, kv0:kv0 + width])
            kv_b = _dot(kvn, wukv_ref[:, kv0 + width:kv0 + 2 * width])
            for pp in range(heads_per_dot // 2):
                pair = hg * (heads_per_dot // 2) + pp
                qpe_ref[0, pair, rows, :] = _rope_pair(
                    q_pe[:, pp * LANES:(pp + 1) * LANES], cos4, sin4).astype(BF16)
            for hh in range(heads_per_dot):
                h = hg * heads_per_dot + hh
                sl = slice(hh * LANES, (hh + 1) * LANES)
                qn_ref[0, h, rows, :] = q_nope[:, sl].astype(BF16)
                kv = kv_a if hh < heads_per_dot // 2 else kv_b
                k0 = (hh % (heads_per_dot // 2)) * 2 * LANES
                kn_ref[0, h, rows, :] = kv[:, k0:k0 + LANES].astype(BF16)
                v_ref[0, h, rows, :] = kv[:, k0 + LANES:k0 + 2 * LANES].astype(BF16)


def _inproj(x, norm_g, cs, w_lat_t, q_g, kv_g, w_uq, w_ukv, tm, sub):
    B, S, D = x.shape
    row = lambda b, i: (b, i, 0)
    head = lambda b, i: (b, 0, i, 0)
    return pl.pallas_call(
        functools.partial(_inproj_kernel, sub=sub),
        grid=(B, S // tm),
        in_specs=[
            pl.BlockSpec((1, tm, D), row),
            _resident((1, D)),
            pl.BlockSpec((1, tm, LANES), row),
            _resident(w_lat_t.shape),
            _resident((1, Q_LORA_RANK)),
            _resident((1, KV_LORA_RANK)),
            _resident(w_uq.shape),
            _resident(w_ukv.shape),
        ],
        out_specs=[
            pl.BlockSpec((1, tm, D), row),
            pl.BlockSpec((1, N_HEADS, tm, LANES), head),
            pl.BlockSpec((1, N_HEADS // 2, tm, LANES), head),
            pl.BlockSpec((1, N_HEADS, tm, LANES), head),
            pl.BlockSpec((1, N_HEADS, tm, LANES), head),
            pl.BlockSpec((1, tm, 2 * LANES), row),
        ],
        out_shape=[
            jax.ShapeDtypeStruct((B, S, D), BF16),
            jax.ShapeDtypeStruct((B, N_HEADS, S, LANES), BF16),
            jax.ShapeDtypeStruct((B, N_HEADS // 2, S, LANES), BF16),
            jax.ShapeDtypeStruct((B, N_HEADS, S, LANES), BF16),
            jax.ShapeDtypeStruct((B, N_HEADS, S, LANES), BF16),
            jax.ShapeDtypeStruct((B, S, 2 * LANES), BF16),
        ],
        compiler_params=_params(2),
        name="inproj",
    )(x, norm_g, cs, w_lat_t, q_g, kv_g, w_uq, w_ukv)


def _sgu_kernel(a_ref, wuv_ref, sg_ref, ws_ref, bfull_ref, wos_ref, wg1_ref, bg1_ref, cast_ref,
                m_ref, cast_out_ref, *, sub):
    _cast_rows([cast_ref], [cast_out_ref])
    n_chunks = sub // CHUNK
    t_idx = lax.broadcasted_iota(jnp.int32, (CHUNK, CHUNK), 0)
    s_idx = lax.broadcasted_iota(jnp.int32, (CHUNK, CHUNK), 1)
    causal = t_idx >= s_idx
    ws = [jnp.where(causal, ws_ref[g], 0.0).astype(BF16) for g in range(SGU_GROUPS)]
    bfull = bfull_ref[...]
    def gating_unit(uv_raw):
        uv = jax.nn.gelu(uv_raw)
        u = uv[:, :SGU_WIDTH]
        vn = _rms(uv[:, SGU_WIDTH:], sg_ref[...]).astype(BF16)
        mixed_cols = []
        for g in range(SGU_GROUPS):
            gs = slice(g * SGU_GROUP_DIM, (g + 1) * SGU_GROUP_DIM)
            rhs = jnp.concatenate(
                [vn[c * CHUNK:(c + 1) * CHUNK, gs] for c in range(n_chunks)], axis=1)
            mixed_cols.append(_dot(ws[g], rhs))
        rows = []
        for c in range(n_chunks):
            cs = slice(c * SGU_GROUP_DIM, (c + 1) * SGU_GROUP_DIM)
            mixed = jnp.concatenate([mixed_cols[g][:, cs] for g in range(SGU_GROUPS)], axis=1)
            rows.append(u[c * CHUNK:(c + 1) * CHUNK] * (mixed + bfull))
        return jnp.concatenate(rows, axis=0).astype(BF16)

    a = a_ref[...]
    uv_raw = _dot_t(a, wuv_ref[...])
    gate_raw = _dot_t(a, wg1_ref[...])
    for r0 in range(0, a_ref.shape[0], sub):
        rows = slice(r0, r0 + sub)
        y_sgu = _dot(gating_unit(uv_raw[rows]), wos_ref[...])
        m_ref[rows, :] = jax.nn.sigmoid(gate_raw[rows] + bg1_ref[...]) * y_sgu


def _sgu_branch(a, w_uv, sgu_g, w_s, b_full, w_o_sgu, w_g1, b_g1, cast_w, tm, sub):
    T, D = a.shape
    row = lambda i: (i, 0)
    cast_in, cast_out = _cast_block_specs(*cast_w.shape, 0, T // tm, lambda i: i)
    return pl.pallas_call(
        functools.partial(_sgu_kernel, sub=sub),
        grid=(T // tm,),
        in_specs=[
            pl.BlockSpec((tm, D), row),
            _resident(w_uv.shape),
            _resident(sgu_g.shape),
            _resident(w_s.shape),
            _resident(b_full.shape),
            _resident(w_o_sgu.shape),
            _resident(w_g1.shape),
            _resident(b_g1.shape),
            cast_in,
        ],
        out_specs=[pl.BlockSpec((tm, D), row), cast_out],
        out_shape=[jax.ShapeDtypeStruct((T, D), F32), jax.ShapeDtypeStruct(cast_w.shape, BF16)],
        compiler_params=_params(1),
        name="sgu_branch",
    )(a, w_uv, sgu_g, w_s, b_full, w_o_sgu, w_g1, b_g1, cast_w)


def _attn_kernel(qn_ref, qpe_ref, kn_ref, kpe_ref, v_ref, *rest, tq, n_cast):
    cast_in, (o_ref,), cast_out, (kf_ref, vf_ref) = (
        rest[:n_cast], rest[n_cast:n_cast + 1], rest[n_cast + 1:2 * n_cast + 1], rest[2 * n_cast + 1:])
    _cast_rows(cast_in, cast_out)
    seq = qn_ref.shape[2]
    row = lax.broadcasted_iota(jnp.int32, (tq, tq), 0)
    col = lax.broadcasted_iota(jnp.int32, (tq, tq), 1)
    causal = row >= col
    neg = jnp.finfo(F32).min
    heads = qn_ref.shape[1]
    for hh in range(heads):
        parity = hh % 2
        kf_ref[hh, :, 0:LANES] = kn_ref[0, hh]
        kf_ref[hh, :, LANES:2 * LANES] = kpe_ref[0, :, parity * LANES:(parity + 1) * LANES]
        vf_ref[hh, :, 0:LANES] = v_ref[0, hh]
        vf_ref[hh, :, LANES:2 * LANES] = jnp.ones((seq, LANES), vf_ref.dtype)

    nt = (((1,), (1,)), ((), ()))
    def scores(pair, hh):
        q0 = pair * 2 * tq
        k1, k2 = q0 + tq, q0 + 2 * tq
        q = jnp.concatenate([qn_ref[0, hh, q0:k2, :], qpe_ref[0, hh // 2, q0:k2, :]], axis=1)
        return (lax.dot_general(q, kf_ref[hh, 0:k1, :], nt, preferred_element_type=F32),
                lax.dot_general(q[tq:], kf_ref[hh, k1:k2, :], nt, preferred_element_type=F32))

    def finish(pair, hh, s_main, s_ext):
        q0 = pair * 2 * tq
        k1, k2 = q0 + tq, q0 + 2 * tq
        top = s_main[0:tq]
        top_diag = jnp.where(causal, top[:, q0:k1], neg)
        top = jnp.concatenate([top[:, 0:q0], top_diag], axis=1) if pair else top_diag
        bot = s_main[tq:]
        ext = jnp.where(causal, s_ext, neg)
        m_top = jnp.max(top, axis=-1, keepdims=True)
        m_bot = jnp.maximum(jnp.max(bot, axis=-1, keepdims=True),
                            jnp.max(ext, axis=-1, keepdims=True))
        p_main = jnp.concatenate([jnp.exp2(top - m_top), jnp.exp2(bot - m_bot)], axis=0)
        acc = _dot(p_main.astype(BF16), vf_ref[hh, 0:k1, :])
        acc_bot = acc[tq:] + _dot(jnp.exp2(ext - m_bot).astype(BF16), vf_ref[hh, k1:k2, :])
        lanes = slice(hh * LANES, (hh + 1) * LANES)
        o_ref[0, q0:k1, lanes] = (acc[0:tq, 0:LANES] / acc[0:tq, LANES:]).astype(o_ref.dtype)
        o_ref[0, k1:k2, lanes] = (acc_bot[:, 0:LANES] / acc_bot[:, LANES:]).astype(o_ref.dtype)

    work = [(pair, hh) for pair in reversed(range(seq // (2 * tq))) for hh in range(heads)]
    lead = 2
    pending = [scores(*w) for w in work[:lead]]
    for idx, (pair, hh) in enumerate(work):
        if idx + lead < len(work):
            pending.append(scores(*work[idx + lead]))
        finish(pair, hh, *pending.pop(0))


def _attention(q_nope, q_pe, k_nope, k_pe, v, casts, tq, hb):
    B, H, S, _ = q_nope.shape
    assert hb % 2 == 0, "head pairs share a rope tile"
    groups = H // hb
    specs = [_cast_block_specs(n, w.shape[1], r0, B * groups, lambda b, g: b * groups + g)
             for w, r0, n in casts]
    outs = pl.pallas_call(
        functools.partial(_attn_kernel, tq=tq, n_cast=len(casts)),
        grid=(B, groups),
        in_specs=[
            pl.BlockSpec((1, hb, S, LANES), lambda b, g: (b, g, 0, 0)),
            pl.BlockSpec((1, hb // 2, S, LANES), lambda b, g: (b, g, 0, 0)),
            pl.BlockSpec((1, hb, S, LANES), lambda b, g: (b, g, 0, 0)),
            pl.BlockSpec((1, S, 2 * LANES), lambda b, g: (b, 0, 0)),
            pl.BlockSpec((1, hb, S, LANES), lambda b, g: (b, g, 0, 0)),
        ] + [s_in for s_in, _ in specs],
        out_specs=[pl.BlockSpec((1, S, hb * LANES), lambda b, g: (b, 0, g))]
        + [s_out for _, s_out in specs],
        out_shape=[jax.ShapeDtypeStruct((B, S, H * V_HEAD_DIM), BF16)]
        + [jax.ShapeDtypeStruct((n, w.shape[1]), BF16) for w, _, n in casts],
        scratch_shapes=[pltpu.VMEM((hb, S, 2 * LANES), BF16), pltpu.VMEM((hb, S, 2 * LANES), BF16)],
        compiler_params=_params(2),
        name="mla_attention",
    )(q_nope, q_pe, k_nope, k_pe, v, *[w for w, _, _ in casts])
    return outs[0], outs[1:]


def _merge_kernel(attn_ref, a_ref, m_ref, x_ref, woa_ref, wg0_ref, bg0_ref, wout_ref, fg_ref,
                  cast_ref, h_ref, f_ref, cast_out_ref):
    _cast_rows([cast_ref], [cast_out_ref])
    y_attn = _dot(attn_ref[...], woa_ref[...])
    gate = jax.nn.sigmoid(_dot_t(a_ref[...], wg0_ref[...]) + bg0_ref[...])
    merged = (gate * y_attn + m_ref[...]).astype(BF16)
    h = x_ref[...] + _dot(merged, wout_ref[...])
    h_ref[...] = h
    f_ref[...] = _rms(h, fg_ref[...]).astype(BF16)


def _merge(attn, a, m_sgu, x, w_o_attn, w_g0, b_g0, w_out, ffn_g, cast_w, tm):
    T, D = x.shape
    row = lambda i: (i, 0)
    tile = pl.BlockSpec((tm, D), row)
    cast_in, cast_out = _cast_block_specs(*cast_w.shape, 0, T // tm, lambda i: i)
    return pl.pallas_call(
        _merge_kernel,
        grid=(T // tm,),
        in_specs=[tile, tile, tile, tile,
                  _resident(w_o_attn.shape), _resident(w_g0.shape), _resident(b_g0.shape),
                  _resident(w_out.shape), _resident(ffn_g.shape), cast_in],
        out_specs=[tile, tile, cast_out],
        out_shape=[jax.ShapeDtypeStruct((T, D), F32), jax.ShapeDtypeStruct((T, D), BF16),
                   jax.ShapeDtypeStruct(cast_w.shape, BF16)],
        compiler_params=_params(1),
        name="merge_outproj",
    )(attn, a, m_sgu, x, w_o_attn, w_g0, b_g0, w_out, ffn_g, cast_w)


def _ffn_kernel(f_ref, h_ref, wg_ref, wu_ref, wd_ref, ng_ref, o_ref, *, sub):
    j = pl.program_id(1)

    f = f_ref[...]
    starts = list(range(0, wg_ref.shape[1], sub))
    projected = [(_dot(f, wg_ref[:, c0:c0 + sub]), _dot(f, wu_ref[:, c0:c0 + sub]))
                 for c0 in starts]
    act = jnp.concatenate(
        [(jax.nn.silu(gate) * up).astype(BF16) for gate, up in projected], axis=1)
    o_ref[...] = jnp.where(j == 0, h_ref[...], o_ref[...]) + _dot(act, wd_ref[...])

    @pl.when(j == pl.num_programs(1) - 1)
    def _():
        o_ref[...] = _rms(o_ref[...], ng_ref[...])


def _ffn(f, h, w_gate, w_up, w_down, final_g, tm, tf, sub):
    T, D = h.shape
    d_ff = w_gate.shape[1]
    row = lambda i, j: (i, 0)
    n_i, n_j = T // tm, d_ff // tf
    h_row = lambda i, j: (jnp.minimum(i + (j >= n_j // 2).astype(jnp.int32), n_i - 1), 0)
    return pl.pallas_call(
        functools.partial(_ffn_kernel, sub=sub),
        grid=(n_i, n_j),
        in_specs=[
            pl.BlockSpec((tm, D), row),
            pl.BlockSpec((tm, D), h_row),
            pl.BlockSpec((D, tf), lambda i, j: (0, j)),
            pl.BlockSpec((D, tf), lambda i, j: (0, j)),
            pl.BlockSpec((tf, D), lambda i, j: (j, 0)),
            _resident((1, D)),
        ],
        out_specs=pl.BlockSpec((tm, D), row),
        out_shape=jax.ShapeDtypeStruct((T, D), F32),
        compiler_params=_params(2),
        name="swiglu_ffn",
    )(f, h, w_gate, w_up, w_down, final_g)


def kernel(x, positions, norm_mix_g, w_in, b_gate, q_norm_g, w_uq, kv_norm_g, w_ukv, w_o_attn,
           sgu_norm_g, w_sgu, b_sgu, w_o_sgu, w_out, norm_ffn_g, w_gate_ffn, w_up_ffn,
           w_down_ffn, norm_final_g):
    B, S, D = x.shape
    T = B * S
    depth = w_in.shape[0]
    assert depth == 1, "the final norm is fused into the FFN epilogue of a single layer"
    assert w_in.shape[1:] == (D, D_IN)

    row_vec = lambda v: v.reshape(1, -1).astype(F32)

    h = x
    out = None
    for l in range(depth):
        w_in_t = jnp.swapaxes(w_in[l], 0, 1)
        cs, w_lat_t, w_uq_p, w_ukv_p = _prep_inproj(positions, w_in_t, w_uq[l], w_ukv[l])
        cs = cs.reshape(B, S, LANES)
        b_full = jnp.repeat(b_sgu[l].T, SGU_GROUP_DIM, axis=1).astype(F32)

        a, q_nope, q_pe, k_nope, v, k_pe = _inproj(
            h, row_vec(norm_mix_g[l]), cs, w_lat_t, row_vec(q_norm_g[l]), row_vec(kv_norm_g[l]),
            w_uq_p, w_ukv_p, tm=INPROJ_ROWS, sub=ROW_SUB)
        a2 = a.reshape(T, D)
        whole = lambda w: (w, 0, w.shape[0])
        attn, (w_uv_t, w_g0_t, w_g1_t, w_os, w_oa, w_o, w_uf) = _attention(
            q_nope, q_pe, k_nope, k_pe, v,
            [(w_in_t, UV_OFF, 2 * SGU_WIDTH), (w_in_t, GATE_OFF, D), (w_in_t, GATE_OFF + D, D),
             whole(w_o_sgu[l]), whole(w_o_attn[l]), whole(w_out[l]), whole(w_up_ffn[l])],
            tq=ATTN_Q_ROWS, hb=ATTN_HEADS_PER_STEP)
        m_sgu, w_gf = _sgu_branch(
            a2, w_uv_t, row_vec(sgu_norm_g[l]), w_sgu[l], b_full, w_os, w_g1_t,
            row_vec(b_gate[l, D:]), w_gate_ffn[l], tm=SGU_ROWS, sub=ROW_SUB)
        h_mid, f, w_df = _merge(
            attn.reshape(T, D), a2, m_sgu, h.reshape(T, D), w_oa, w_g0_t,
            row_vec(b_gate[l, :D]), w_o, row_vec(norm_ffn_g[l]), w_down_ffn[l], tm=MERGE_ROWS)
        out = _ffn(f, h_mid, w_gf, w_uf, w_df, row_vec(norm_final_g), tm=FFN_ROWS,
                   tf=FFN_COLS, sub=FFN_COL_SUB)
        h = out.reshape(B, S, D)
    return h
```

```python
import functools

import jax
import jax.numpy as jnp
from jax import lax
from jax.experimental import pallas as pl
from jax.experimental.pallas import tpu as pltpu

D_MODEL = 2048
N_HEADS = 16
QK_NOPE_DIM = 128
QK_ROPE_DIM = 64
V_HEAD_DIM = 128
Q_LORA_RANK = 512
KV_LORA_RANK = 512
ROPE_THETA = 10000.0
SGU_GROUPS = 8
SGU_GROUP_DIM = 128
SGU_WIDTH = SGU_GROUPS * SGU_GROUP_DIM
CHUNK = 128
N_BRANCH = 2
RMS_EPS = 1e-6
KPE_OFF = Q_LORA_RANK + KV_LORA_RANK
UV_OFF = KPE_OFF + QK_ROPE_DIM
GATE_OFF = UV_OFF + 2 * SGU_WIDTH
D_IN = GATE_OFF + N_BRANCH * D_MODEL
LANES = 128
HALF_ROPE = QK_ROPE_DIM // 2
LOG2_E = 1.4426950408889634
QK_LOG2_SCALE = (QK_NOPE_DIM + QK_ROPE_DIM) ** -0.5 * LOG2_E
BF16_SUBLANES = 16

VMEM_LIMIT_BYTES = 60 * 1024 * 1024

INPROJ_ROWS = 512
SGU_ROWS = 512
MERGE_ROWS = 256
ROW_SUB = 256
ATTN_Q_ROWS = 256
ATTN_HEADS_PER_STEP = 2
FFN_ROWS = 1024
FFN_COLS = 512
FFN_COL_SUB = 256
PREP_STEPS = 4

F32 = jnp.float32
BF16 = jnp.bfloat16


def _rms(x, g):
    return x * lax.rsqrt(jnp.mean(x * x, axis=-1, keepdims=True) + RMS_EPS) * g


def _dot(a, b):
    return jnp.dot(a, b, preferred_element_type=F32)


def _dot_t(a, b_t):
    return lax.dot_general(a, b_t, (((1,), (1,)), ((), ())), preferred_element_type=F32)


def _resident(shape):
    return pl.BlockSpec(shape, lambda *_: (0,) * len(shape), pipeline_mode=pl.Buffered(1))


def _resident_cols(n_cols, block):
    return pl.BlockSpec((1, n_cols), lambda *_: (0, block), pipeline_mode=pl.Buffered(1))


def _params(n_axes):
    return pltpu.CompilerParams(
        dimension_semantics=("arbitrary",) * n_axes, vmem_limit_bytes=VMEM_LIMIT_BYTES)


def _cast_block_specs(n_rows, n_cols, row0, n_steps, linear_step):
    share = 1 if (n_rows // n_steps) % BF16_SUBLANES == 0 else 2
    blk = n_rows * share // n_steps
    assert blk * n_steps == n_rows * share and blk % BF16_SUBLANES == 0 and row0 % blk == 0
    first = row0 // blk
    return (pl.BlockSpec((blk, n_cols), lambda *idx: (first + linear_step(*idx) // share, 0)),
            pl.BlockSpec((blk, n_cols), lambda *idx: (linear_step(*idx) // share, 0)))


def _cast_rows(srcs, dsts):
    for src, dst in zip(srcs, dsts):
        dst[...] = src[...].astype(dst.dtype)


def _rope_table_rows(pos_ref, freq_ref, cs_ref):
    ang = pos_ref[...] * freq_ref[...]
    cos, sin = jnp.cos(ang), jnp.sin(ang)
    n_rows = ang.shape[0]
    per_row = LANES // HALF_ROPE
    lane = lax.broadcasted_iota(jnp.int32, ang.shape, 1)

    def lanes_from(x, src, dst):
        shift = (dst - src) % LANES
        return pltpu.roll(x, shift, 1) if shift else x

    for k in range(per_row):
        src = k * HALF_ROPE
        row = jnp.where(
            lane < HALF_ROPE, lanes_from(cos, src, 0),
            jnp.where(lane < 2 * HALF_ROPE, lanes_from(cos, src, HALF_ROPE),
                      jnp.where(lane < 3 * HALF_ROPE, -lanes_from(sin, src, 2 * HALF_ROPE),
                                lanes_from(sin, src, 3 * HALF_ROPE))))
        cs_ref[pl.ds(k, n_rows, stride=per_row), :] = row


def _rope_table_inputs(positions):
    n_tok = positions.size
    per_row = LANES // HALF_ROPE
    inv_freq = ROPE_THETA ** (-jnp.arange(0, QK_ROPE_DIM, 2, dtype=F32) / QK_ROPE_DIM)
    pos_rep = jnp.repeat(positions.astype(F32).reshape(n_tok // per_row, per_row), HALF_ROPE, axis=1)
    return pos_rep, jnp.tile(inv_freq, per_row).reshape(1, LANES)


def _rope_dup(x, cs):
    y = x * cs
    return y + pltpu.roll(y, LANES // 2, 1)


def _rope_pair(p, cos4, sin4):
    lane = lax.broadcasted_iota(jnp.int32, p.shape, 1)
    first_half = lane % QK_ROPE_DIM < HALF_ROPE
    partner = jnp.where(first_half, pltpu.roll(p, LANES - HALF_ROPE, 1), pltpu.roll(p, HALF_ROPE, 1))
    return p * cos4 + partner * sin4


def _prep_kernel(pos_ref, freq_ref, wlat_ref, wuq_ref, wukv_ref,
                 cs_ref, olat_ref, ouq_ref, oukv_ref):
    _rope_table_rows(pos_ref, freq_ref, cs_ref)
    olat_ref[0:KPE_OFF] = wlat_ref[0:KPE_OFF].astype(BF16)
    x1 = wlat_ref[KPE_OFF:KPE_OFF + HALF_ROPE].astype(BF16)
    x2 = wlat_ref[KPE_OFF + HALF_ROPE:UV_OFF].astype(BF16)
    for k, part in enumerate((x1, x2, x2, x1)):
        olat_ref[KPE_OFF + k * HALF_ROPE:KPE_OFF + (k + 1) * HALF_ROPE] = part
    oukv_ref[...] = wukv_ref[...].astype(BF16)
    half = LANES // 2
    nope_cols = N_HEADS * QK_NOPE_DIM
    lane = lax.broadcasted_iota(jnp.int32, (wuq_ref.shape[0], LANES), 1)
    for pair in range(N_HEADS // 2):
        t0, t1, t2 = (wuq_ref[:, (3 * pair + k) * LANES:(3 * pair + k + 1) * LANES] for k in range(3))
        nope_odd = jnp.where(lane < half, pltpu.roll(t1, half, 1), pltpu.roll(t2, half, 1))
        for h, nope in ((2 * pair, t0), (2 * pair + 1, nope_odd)):
            ouq_ref[:, h * LANES:(h + 1) * LANES] = nope.astype(BF16)
        ouq_ref[:, nope_cols + pair * LANES:nope_cols + (pair + 1) * LANES] = (
            jnp.where(lane < half, t1, t2).astype(BF16))


def _prep_inproj(positions, w_in_t, w_uq, w_ukv):
    n_chunks = PREP_STEPS
    d = w_in_t.shape[1]
    r_q, r_kv = w_uq.shape[0], w_ukv.shape[0]
    lat_rows = UV_OFF + LANES - QK_ROPE_DIM
    uq_cols = N_HEADS * QK_NOPE_DIM + (N_HEADS // 2) * LANES
    pos_rep, freq = _rope_table_inputs(positions)
    n_tok = positions.size
    return pl.pallas_call(
        _prep_kernel,
        grid=(n_chunks,),
        in_specs=[
            pl.BlockSpec((pos_rep.shape[0] // n_chunks, LANES), lambda c: (c, 0)),
            _resident((1, LANES)),
            pl.BlockSpec((UV_OFF, d // n_chunks), lambda c: (0, c)),
            pl.BlockSpec((r_q // n_chunks, w_uq.shape[1]), lambda c: (c, 0)),
            pl.BlockSpec((r_kv // n_chunks, w_ukv.shape[1]), lambda c: (c, 0)),
        ],
        out_specs=[
            pl.BlockSpec((n_tok // n_chunks, LANES), lambda c: (c, 0)),
            pl.BlockSpec((lat_rows, d // n_chunks), lambda c: (0, c)),
            pl.BlockSpec((r_q // n_chunks, uq_cols), lambda c: (c, 0)),
            pl.BlockSpec((r_kv // n_chunks, w_ukv.shape[1]), lambda c: (c, 0)),
        ],
        out_shape=[
            jax.ShapeDtypeStruct((n_tok, LANES), F32),
            jax.ShapeDtypeStruct((lat_rows, d), BF16),
            jax.ShapeDtypeStruct((r_q, uq_cols), BF16),
            jax.ShapeDtypeStruct(w_ukv.shape, BF16),
        ],
        compiler_params=_params(1),
        name="prep_inproj",
    )(pos_rep, freq, w_in_t, w_uq, w_ukv)


def _inproj_kernel(x_ref, g_ref, cs_ref, wlat_ref, qg_ref, kvg_ref, wuq_ref, wukv_ref,
                   a_ref, qn_ref, qpe_ref, kn_ref, v_ref, kpe_ref, *, sub):
    def latents(r0):
        a = _rms(x_ref[0, r0:r0 + sub, :], g_ref[...]).astype(BF16)
        a_ref[0, r0:r0 + sub, :] = a
        return _dot_t(a, wlat_ref[...])

    starts = list(range(0, x_ref.shape[1], sub))
    for r0, z in zip(starts, [latents(r0) for r0 in starts]):
        rows = slice(r0, r0 + sub)
        qn = (_rms(z[:, :Q_LORA_RANK], qg_ref[...]) * QK_LOG2_SCALE).astype(BF16)
        kvn = _rms(z[:, Q_LORA_RANK:Q_LORA_RANK + KV_LORA_RANK], kvg_ref[...]).astype(BF16)
        cs = cs_ref[0, rows, :]
        kpe = _rope_dup(z[:, Q_LORA_RANK + KV_LORA_RANK:], cs)
        lane = lax.broadcasted_iota(jnp.int32, kpe.shape, 1)
        low = lane < QK_ROPE_DIM
        kpe_ref[0, rows, 0:LANES] = jnp.where(low, kpe, 0.0).astype(BF16)
        kpe_ref[0, rows, LANES:2 * LANES] = jnp.where(low, 0.0, kpe).astype(BF16)
        cs_swapped = pltpu.roll(cs, LANES // 2, 1)
        cos4 = jnp.where(low, cs, cs_swapped)
        sin4 = jnp.where(low, cs_swapped, cs)

        heads_per_dot = 4
        width = heads_per_dot * LANES
        nope_cols = N_HEADS * QK_NOPE_DIM
        for hg in range(N_HEADS // heads_per_dot):
            c0 = hg * width
            q_nope = _dot(qn, wuq_ref[:, c0:c0 + width])
            pe0 = nope_cols + c0 // 2
            q_pe = _dot(qn, wuq_ref[:, pe0:pe0 + width // 2])
            kv0 = 2 * c0
            kv_a = _dot(kvn, wukv_ref[:, kv0:kv0 + width])
            kv_b = _dot(kvn, wukv_ref[:, kv0 + width:kv0 + 2 * width])
            for pp in range(heads_per_dot // 2):
                pair = hg * (heads_per_dot // 2) + pp
                qpe_ref[0, pair, rows, :] = _rope_pair(
                    q_pe[:, pp * LANES:(pp + 1) * LANES], cos4, sin4).astype(BF16)
            for hh in range(heads_per_dot):
                h = hg * heads_per_dot + hh
                sl = slice(hh * LANES, (hh + 1) * LANES)
                qn_ref[0, h, rows, :] = q_nope[:, sl].astype(BF16)
                kv = kv_a if hh < heads_per_dot // 2 else kv_b
                k0 = (hh % (heads_per_dot // 2)) * 2 * LANES
                kn_ref[0, h, rows, :] = kv[:, k0:k0 + LANES].astype(BF16)
                v_ref[0, h, rows, :] = kv[:, k0 + LANES:k0 + 2 * LANES].astype(BF16)


def _inproj(x, norm_g, cs, w_lat_t, q_g, kv_g, w_uq, w_ukv, tm, sub):
    B, S, D = x.shape
    row = lambda b, i: (b, i, 0)
    head = lambda b, i: (b, 0, i, 0)
    return pl.pallas_call(
        functools.partial(_inproj_kernel, sub=sub),
        grid=(B, S // tm),
        in_specs=[
            pl.BlockSpec((1, tm, D), row),
            _resident((1, D)),
            pl.BlockSpec((1, tm, LANES), row),
            _resident(w_lat_t.shape),
            _resident((1, Q_LORA_RANK)),
            _resident((1, KV_LORA_RANK)),
            _resident(w_uq.shape),
            _resident(w_ukv.shape),
        ],
        out_specs=[
            pl.BlockSpec((1, tm, D), row),
            pl.BlockSpec((1, N_HEADS, tm, LANES), head),
            pl.BlockSpec((1, N_HEADS // 2, tm, LANES), head),
            pl.BlockSpec((1, N_HEADS, tm, LANES), head),
            pl.BlockSpec((1, N_HEADS, tm, LANES), head),
            pl.BlockSpec((1, tm, 2 * LANES), row),
        ],
        out_shape=[
            jax.ShapeDtypeStruct((B, S, D), BF16),
            jax.ShapeDtypeStruct((B, N_HEADS, S, LANES), BF16),
            jax.ShapeDtypeStruct((B, N_HEADS // 2, S, LANES), BF16),
            jax.ShapeDtypeStruct((B, N_HEADS, S, LANES), BF16),
            jax.ShapeDtypeStruct((B, N_HEADS, S, LANES), BF16),
            jax.ShapeDtypeStruct((B, S, 2 * LANES), BF16),
        ],
        compiler_params=_params(2),
        name="inproj",
    )(x, norm_g, cs, w_lat_t, q_g, kv_g, w_uq, w_ukv)


def _sgu_kernel(a_ref, wuv_ref, sg_ref, ws_ref, bfull_ref, wos_ref, wg1_ref, bg1_ref, cast_ref,
                m_ref, cast_out_ref, *, sub):
    _cast_rows([cast_ref], [cast_out_ref])
    n_chunks = sub // CHUNK
    t_idx = lax.broadcasted_iota(jnp.int32, (CHUNK, CHUNK), 0)
    s_idx = lax.broadcasted_iota(jnp.int32, (CHUNK, CHUNK), 1)
    causal = t_idx >= s_idx
    ws = [jnp.where(causal, ws_ref[g], 0.0).astype(BF16) for g in range(SGU_GROUPS)]
    bfull = jnp.concatenate(
        [jnp.broadcast_to(bfull_ref[:, g:g + 1], (CHUNK, SGU_GROUP_DIM)) for g in range(SGU_GROUPS)],
        axis=1)
    def gating_unit(uv_raw):
        uv = jax.nn.gelu(uv_raw)
        u = uv[:, :SGU_WIDTH]
        vn = _rms(uv[:, SGU_WIDTH:], sg_ref[...]).astype(BF16)
        mixed_cols = []
        for g in range(SGU_GROUPS):
            gs = slice(g * SGU_GROUP_DIM, (g + 1) * SGU_GROUP_DIM)
            rhs = jnp.concatenate(
                [vn[c * CHUNK:(c + 1) * CHUNK, gs] for c in range(n_chunks)], axis=1)
            mixed_cols.append(_dot(ws[g], rhs))
        rows = []
        for c in range(n_chunks):
            cs = slice(c * SGU_GROUP_DIM, (c + 1) * SGU_GROUP_DIM)
            mixed = jnp.concatenate([mixed_cols[g][:, cs] for g in range(SGU_GROUPS)], axis=1)
            rows.append(u[c * CHUNK:(c + 1) * CHUNK] * (mixed + bfull))
        return jnp.concatenate(rows, axis=0).astype(BF16)

    a = a_ref[...]
    uv_raw = _dot_t(a, wuv_ref[...])
    gate_raw = _dot_t(a, wg1_ref[...])
    for r0 in range(0, a_ref.shape[0], sub):
        rows = slice(r0, r0 + sub)
        y_sgu = _dot(gating_unit(uv_raw[rows]), wos_ref[...])
        m_ref[rows, :] = jax.nn.sigmoid(gate_raw[rows] + bg1_ref[...]) * y_sgu


def _sgu_branch(a, w_uv, sgu_g, w_s, b_full, w_o_sgu, w_g1, b_g1, cast_w, tm, sub):
    T, D = a.shape
    row = lambda i: (i, 0)
    cast_in, cast_out = _cast_block_specs(*cast_w.shape, 0, T // tm, lambda i: i)
    return pl.pallas_call(
        functools.partial(_sgu_kernel, sub=sub),
        grid=(T // tm,),
        in_specs=[
            pl.BlockSpec((tm, D), row),
            _resident(w_uv.shape),
            _resident(sgu_g.shape),
            _resident(w_s.shape),
            _resident(b_full.shape),
            _resident(w_o_sgu.shape),
            _resident(w_g1.shape),
            _resident_cols(D, 1),
            cast_in,
        ],
        out_specs=[pl.BlockSpec((tm, D), row), cast_out],
        out_shape=[jax.ShapeDtypeStruct((T, D), F32), jax.ShapeDtypeStruct(cast_w.shape, BF16)],
        compiler_params=_params(1),
        name="sgu_branch",
    )(a, w_uv, sgu_g, w_s, b_full, w_o_sgu, w_g1, b_g1, cast_w)


def _attn_kernel(qn_ref, qpe_ref, kn_ref, kpe_ref, v_ref, *rest, tq, n_cast):
    cast_in, (o_ref,), cast_out, (kf_ref, vf_ref) = (
        rest[:n_cast], rest[n_cast:n_cast + 1], rest[n_cast + 1:2 * n_cast + 1], rest[2 * n_cast + 1:])
    _cast_rows(cast_in, cast_out)
    seq = qn_ref.shape[2]
    row = lax.broadcasted_iota(jnp.int32, (tq, tq), 0)
    col = lax.broadcasted_iota(jnp.int32, (tq, tq), 1)
    causal = row >= col
    neg = jnp.finfo(F32).min
    heads = qn_ref.shape[1]
    for hh in range(heads):
        parity = hh % 2
        kf_ref[hh, :, 0:LANES] = kn_ref[0, hh]
        kf_ref[hh, :, LANES:2 * LANES] = kpe_ref[0, :, parity * LANES:(parity + 1) * LANES]
        vf_ref[hh, :, 0:LANES] = v_ref[0, hh]
        vf_ref[hh, :, LANES:2 * LANES] = jnp.ones((seq, LANES), vf_ref.dtype)

    nt = (((1,), (1,)), ((), ()))
    def scores(pair, hh):
        q0 = pair * 2 * tq
        k1, k2 = q0 + tq, q0 + 2 * tq
        q = jnp.concatenate([qn_ref[0, hh, q0:k2, :], qpe_ref[0, hh // 2, q0:k2, :]], axis=1)
        return (lax.dot_general(q, kf_ref[hh, 0:k1, :], nt, preferred_element_type=F32),
                lax.dot_general(q[tq:], kf_ref[hh, k1:k2, :], nt, preferred_element_type=F32))

    def finish(pair, hh, s_main, s_ext):
        q0 = pair * 2 * tq
        k1, k2 = q0 + tq, q0 + 2 * tq
        top = s_main[0:tq]
        top_diag = jnp.where(causal, top[:, q0:k1], neg)
        top = jnp.concatenate([top[:, 0:q0], top_diag], axis=1) if pair else top_diag
        bot = s_main[tq:]
        ext = jnp.where(causal, s_ext, neg)
        m_top = jnp.max(top, axis=-1, keepdims=True)
        m_bot = jnp.maximum(jnp.max(bot, axis=-1, keepdims=True),
                            jnp.max(ext, axis=-1, keepdims=True))
        p_main = jnp.concatenate([jnp.exp2(top - m_top), jnp.exp2(bot - m_bot)], axis=0)
        acc = _dot(p_main.astype(BF16), vf_ref[hh, 0:k1, :])
        acc_bot = acc[tq:] + _dot(jnp.exp2(ext - m_bot).astype(BF16), vf_ref[hh, k1:k2, :])
        lanes = slice(hh * LANES, (hh + 1) * LANES)
        o_ref[0, q0:k1, lanes] = (acc[0:tq, 0:LANES] / acc[0:tq, LANES:]).astype(o_ref.dtype)
        o_ref[0, k1:k2, lanes] = (acc_bot[:, 0:LANES] / acc_bot[:, LANES:]).astype(o_ref.dtype)

    work = [(pair, hh) for pair in reversed(range(seq // (2 * tq))) for hh in range(heads)]
    lead = 2
    pending = [scores(*w) for w in work[:lead]]
    for idx, (pair, hh) in enumerate(work):
        if idx + lead < len(work):
            pending.append(scores(*work[idx + lead]))
        finish(pair, hh, *pending.pop(0))


def _attention(q_nope, q_pe, k_nope, k_pe, v, casts, tq, hb):
    B, H, S, _ = q_nope.shape
    assert hb % 2 == 0, "head pairs share a rope tile"
    groups = H // hb
    specs = [_cast_block_specs(n, w.shape[1], r0, B * groups, lambda b, g: b * groups + g)
             for w, r0, n in casts]
    outs = pl.pallas_call(
        functools.partial(_attn_kernel, tq=tq, n_cast=len(casts)),
        grid=(B, groups),
        in_specs=[
            pl.BlockSpec((1, hb, S, LANES), lambda b, g: (b, g, 0, 0)),
            pl.BlockSpec((1, hb // 2, S, LANES), lambda b, g: (b, g, 0, 0)),
            pl.BlockSpec((1, hb, S, LANES), lambda b, g: (b, g, 0, 0)),
            pl.BlockSpec((1, S, 2 * LANES), lambda b, g: (b, 0, 0)),
            pl.BlockSpec((1, hb, S, LANES), lambda b, g: (b, g, 0, 0)),
        ] + [s_in for s_in, _ in specs],
        out_specs=[pl.BlockSpec((1, S, hb * LANES), lambda b, g: (b, 0, g))]
        + [s_out for _, s_out in specs],
        out_shape=[jax.ShapeDtypeStruct((B, S, H * V_HEAD_DIM), BF16)]
        + [jax.ShapeDtypeStruct((n, w.shape[1]), BF16) for w, _, n in casts],
        scratch_shapes=[pltpu.VMEM((hb, S, 2 * LANES), BF16), pltpu.VMEM((hb, S, 2 * LANES), BF16)],
        compiler_params=_params(2),
        name="mla_attention",
    )(q_nope, q_pe, k_nope, k_pe, v, *[w for w, _, _ in casts])
    return outs[0], outs[1:]


def _merge_kernel(attn_ref, a_ref, m_ref, x_ref, woa_ref, wg0_ref, bg0_ref, wout_ref, fg_ref,
                  cast_ref, h_ref, f_ref, cast_out_ref):
    _cast_rows([cast_ref], [cast_out_ref])
    y_attn = _dot(attn_ref[...], woa_ref[...])
    gate = jax.nn.sigmoid(_dot_t(a_ref[...], wg0_ref[...]) + bg0_ref[...])
    merged = (gate * y_attn + m_ref[...]).astype(BF16)
    h = x_ref[...] + _dot(merged, wout_ref[...])
    h_ref[...] = h
    f_ref[...] = _rms(h, fg_ref[...]).astype(BF16)


def _merge(attn, a, m_sgu, x, w_o_attn, w_g0, b_g0, w_out, ffn_g, cast_w, tm):
    T, D = x.shape
    row = lambda i: (i, 0)
    tile = pl.BlockSpec((tm, D), row)
    cast_in, cast_out = _cast_block_specs(*cast_w.shape, 0, T // tm, lambda i: i)
    return pl.pallas_call(
        _merge_kernel,
        grid=(T // tm,),
        in_specs=[tile, tile, tile, tile,
                  _resident(w_o_attn.shape), _resident(w_g0.shape), _resident_cols(D, 0),
                  _resident(w_out.shape), _resident(ffn_g.shape), cast_in],
        out_specs=[tile, tile, cast_out],
        out_shape=[jax.ShapeDtypeStruct((T, D), F32), jax.ShapeDtypeStruct((T, D), BF16),
                   jax.ShapeDtypeStruct(cast_w.shape, BF16)],
        compiler_params=_params(1),
        name="merge_outproj",
    )(attn, a, m_sgu, x, w_o_attn, w_g0, b_g0, w_out, ffn_g, cast_w)


def _ffn_kernel(f_ref, h_ref, wg_ref, wu_ref, wd_ref, ng_ref, o_ref, *, sub):
    j = pl.program_id(1)

    f = f_ref[...]
    starts = list(range(0, wg_ref.shape[1], sub))
    projected = [(_dot(f, wg_ref[:, c0:c0 + sub]), _dot(f, wu_ref[:, c0:c0 + sub]))
                 for c0 in starts]
    act = jnp.concatenate(
        [(jax.nn.silu(gate) * up).astype(BF16) for gate, up in projected], axis=1)
    o_ref[...] = jnp.where(j == 0, h_ref[...], o_ref[...]) + _dot(act, wd_ref[...])

    @pl.when(j == pl.num_programs(1) - 1)
    def _():
        o_ref[...] = _rms(o_ref[...], ng_ref[...])


def _ffn(f, h, w_gate, w_up, w_down, final_g, tm, tf, sub):
    T, D = h.shape
    d_ff = w_gate.shape[1]
    row = lambda i, j: (i, 0)
    n_i, n_j = T // tm, d_ff // tf
    h_row = lambda i, j: (jnp.minimum(i + (j >= n_j // 2).astype(jnp.int32), n_i - 1), 0)
    return pl.pallas_call(
        functools.partial(_ffn_kernel, sub=sub),
        grid=(n_i, n_j),
        in_specs=[
            pl.BlockSpec((tm, D), row),
            pl.BlockSpec((tm, D), h_row),
            pl.BlockSpec((D, tf), lambda i, j: (0, j)),
            pl.BlockSpec((D, tf), lambda i, j: (0, j)),
            pl.BlockSpec((tf, D), lambda i, j: (j, 0)),
            _resident((1, D)),
        ],
        out_specs=pl.BlockSpec((tm, D), row),
        out_shape=jax.ShapeDtypeStruct((T, D), F32),
        compiler_params=_params(2),
        name="swiglu_ffn",
    )(f, h, w_gate, w_up, w_down, final_g)


def kernel(x, positions, norm_mix_g, w_in, b_gate, q_norm_g, w_uq, kv_norm_g, w_ukv, w_o_attn,
           sgu_norm_g, w_sgu, b_sgu, w_o_sgu, w_out, norm_ffn_g, w_gate_ffn, w_up_ffn,
           w_down_ffn, norm_final_g):
    B, S, D = x.shape
    T = B * S
    depth = w_in.shape[0]
    assert depth == 1, "the final norm is fused into the FFN epilogue of a single layer"
    assert w_in.shape[1:] == (D, D_IN)

    row_vec = lambda v: v.reshape(1, -1).astype(F32)

    h = x
    out = None
    for l in range(depth):
        w_in_t = jnp.swapaxes(w_in[l], 0, 1)
        cs, w_lat_t, w_uq_p, w_ukv_p = _prep_inproj(positions, w_in_t, w_uq[l], w_ukv[l])
        cs = cs.reshape(B, S, LANES)
        b_full = b_sgu[l].T.astype(F32)
        b_gates = row_vec(b_gate[l])

        a, q_nope, q_pe, k_nope, v, k_pe = _inproj(
            h, row_vec(norm_mix_g[l]), cs, w_lat_t, row_vec(q_norm_g[l]), row_vec(kv_norm_g[l]),
            w_uq_p, w_ukv_p, tm=INPROJ_ROWS, sub=ROW_SUB)
        a2 = a.reshape(T, D)
        whole = lambda w: (w, 0, w.shape[0])
        attn, (w_uv_t, w_g0_t, w_g1_t, w_os, w_oa, w_o, w_uf) = _attention(
            q_nope, q_pe, k_nope, k_pe, v,
            [(w_in_t, UV_OFF, 2 * SGU_WIDTH), (w_in_t, GATE_OFF, D), (w_in_t, GATE_OFF + D, D),
             whole(w_o_sgu[l]), whole(w_o_attn[l]), whole(w_out[l]), whole(w_up_ffn[l])],
            tq=ATTN_Q_ROWS, hb=ATTN_HEADS_PER_STEP)
        m_sgu, w_gf = _sgu_branch(
            a2, w_uv_t, row_vec(sgu_norm_g[l]), w_sgu[l], b_full, w_os, w_g1_t,
            b_gates, w_gate_ffn[l], tm=SGU_ROWS, sub=ROW_SUB)
        h_mid, f, w_df = _merge(
            attn.reshape(T, D), a2, m_sgu, h.reshape(T, D), w_oa, w_g0_t,
            b_gates, w_o, row_vec(norm_ffn_g[l]), w_down_ffn[l], tm=MERGE_ROWS)
        out = _ffn(f, h_mid, w_gf, w_uf, w_df, row_vec(norm_final_g), tm=FFN_ROWS,
                   tf=FFN_COLS, sub=FFN_COL_SUB)
        h = out.reshape(B, S, D)
    return h
```

```python
import functools

import jax
import jax.numpy as jnp
from jax import lax
from jax.experimental import pallas as pl
from jax.experimental.pallas import tpu as pltpu

D_MODEL = 2048
N_HEADS = 16
QK_NOPE_DIM = 128
QK_ROPE_DIM = 64
V_HEAD_DIM = 128
Q_LORA_RANK = 512
KV_LORA_RANK = 512
ROPE_THETA = 10000.0
SGU_GROUPS = 8
SGU_GROUP_DIM = 128
SGU_WIDTH = SGU_GROUPS * SGU_GROUP_DIM
CHUNK = 128
N_BRANCH = 2
RMS_EPS = 1e-6
KPE_OFF = Q_LORA_RANK + KV_LORA_RANK
UV_OFF = KPE_OFF + QK_ROPE_DIM
GATE_OFF = UV_OFF + 2 * SGU_WIDTH
D_IN = GATE_OFF + N_BRANCH * D_MODEL
LANES = 128
HALF_ROPE = QK_ROPE_DIM // 2
LOG2_E = 1.4426950408889634
QK_LOG2_SCALE = (QK_NOPE_DIM + QK_ROPE_DIM) ** -0.5 * LOG2_E
BF16_SUBLANES = 16

VMEM_LIMIT_BYTES = 60 * 1024 * 1024

INPROJ_ROWS = 512
SGU_ROWS = 512
MERGE_ROWS = 256
ROW_SUB = 256
ATTN_Q_ROWS = 256
ATTN_HEADS_PER_STEP = 2
FFN_ROWS = 1024
FFN_COLS = 512
FFN_COL_SUB = 256
PREP_STEPS = 8
PREP_RING_SLOTS = 4

F32 = jnp.float32
BF16 = jnp.bfloat16


def _rms(x, g):
    return x * lax.rsqrt(jnp.mean(x * x, axis=-1, keepdims=True) + RMS_EPS) * g


def _dot(a, b):
    return jnp.dot(a, b, preferred_element_type=F32)


def _dot_t(a, b_t):
    return lax.dot_general(a, b_t, (((1,), (1,)), ((), ())), preferred_element_type=F32)


def _resident(shape):
    return pl.BlockSpec(shape, lambda *_: (0,) * len(shape), pipeline_mode=pl.Buffered(1))


def _resident_cols(n_cols, block):
    return pl.BlockSpec((1, n_cols), lambda *_: (0, block), pipeline_mode=pl.Buffered(1))


def _params(n_axes):
    return pltpu.CompilerParams(
        dimension_semantics=("arbitrary",) * n_axes, vmem_limit_bytes=VMEM_LIMIT_BYTES)


def _cast_block_specs(n_rows, n_cols, row0, n_steps, linear_step):
    share = 1 if (n_rows // n_steps) % BF16_SUBLANES == 0 else 2
    blk = n_rows * share // n_steps
    assert blk * n_steps == n_rows * share and blk % BF16_SUBLANES == 0 and row0 % blk == 0
    first = row0 // blk
    return (pl.BlockSpec((blk, n_cols), lambda *idx: (first + linear_step(*idx) // share, 0)),
            pl.BlockSpec((blk, n_cols), lambda *idx: (linear_step(*idx) // share, 0)))


def _cast_rows(srcs, dsts):
    for src, dst in zip(srcs, dsts):
        dst[...] = src[...].astype(dst.dtype)


def _rope_table_rows(pos_ref, freq_ref, cs_ref):
    ang = pos_ref[...] * freq_ref[...]
    cos, sin = jnp.cos(ang), jnp.sin(ang)
    n_rows = ang.shape[0]
    per_row = LANES // HALF_ROPE
    lane = lax.broadcasted_iota(jnp.int32, ang.shape, 1)

    def lanes_from(x, src, dst):
        shift = (dst - src) % LANES
        return pltpu.roll(x, shift, 1) if shift else x

    for k in range(per_row):
        src = k * HALF_ROPE
        row = jnp.where(
            lane < HALF_ROPE, lanes_from(cos, src, 0),
            jnp.where(lane < 2 * HALF_ROPE, lanes_from(cos, src, HALF_ROPE),
                      jnp.where(lane < 3 * HALF_ROPE, -lanes_from(sin, src, 2 * HALF_ROPE),
                                lanes_from(sin, src, 3 * HALF_ROPE))))
        cs_ref[pl.ds(k, n_rows, stride=per_row), :] = row


def _rope_table_inputs(positions):
    n_tok = positions.size
    per_row = LANES // HALF_ROPE
    inv_freq = ROPE_THETA ** (-jnp.arange(0, QK_ROPE_DIM, 2, dtype=F32) / QK_ROPE_DIM)
    pos_rep = jnp.repeat(positions.astype(F32).reshape(n_tok // per_row, per_row), HALF_ROPE, axis=1)
    return pos_rep, jnp.tile(inv_freq, per_row).reshape(1, LANES)


def _rope_dup(x, cs):
    y = x * cs
    return y + pltpu.roll(y, LANES // 2, 1)


def _rope_pair(p, cos4, sin4):
    lane = lax.broadcasted_iota(jnp.int32, p.shape, 1)
    first_half = lane % QK_ROPE_DIM < HALF_ROPE
    partner = jnp.where(first_half, pltpu.roll(p, LANES - HALF_ROPE, 1), pltpu.roll(p, HALF_ROPE, 1))
    return p * cos4 + partner * sin4


def _prep_kernel(pos_ref, freq_ref, wlat_ref, wuq_ref, wukv_ref,
                 cs_ref, olat_ref, ouq_ref, oukv_ref):
    _rope_table_rows(pos_ref, freq_ref, cs_ref)
    olat_ref[0:KPE_OFF] = wlat_ref[0:KPE_OFF].astype(BF16)
    x1 = wlat_ref[KPE_OFF:KPE_OFF + HALF_ROPE].astype(BF16)
    x2 = wlat_ref[KPE_OFF + HALF_ROPE:UV_OFF].astype(BF16)
    for k, part in enumerate((x1, x2, x2, x1)):
        olat_ref[KPE_OFF + k * HALF_ROPE:KPE_OFF + (k + 1) * HALF_ROPE] = part
    oukv_ref[...] = wukv_ref[...].astype(BF16)
    half = LANES // 2
    nope_cols = N_HEADS * QK_NOPE_DIM
    lane = lax.broadcasted_iota(jnp.int32, (wuq_ref.shape[0], LANES), 1)
    for pair in range(N_HEADS // 2):
        t0, t1, t2 = (wuq_ref[:, (3 * pair + k) * LANES:(3 * pair + k + 1) * LANES] for k in range(3))
        nope_odd = jnp.where(lane < half, pltpu.roll(t1, half, 1), pltpu.roll(t2, half, 1))
        for h, nope in ((2 * pair, t0), (2 * pair + 1, nope_odd)):
            ouq_ref[:, h * LANES:(h + 1) * LANES] = nope.astype(BF16)
        ouq_ref[:, nope_cols + pair * LANES:nope_cols + (pair + 1) * LANES] = (
            jnp.where(lane < half, t1, t2).astype(BF16))


def _inproj_kernel(pos_hbm, freq_ref, win_hbm, wuq_hbm, wukv_hbm, x_ref, g_ref, qg_ref, kvg_ref,
                   a_ref, qn_ref, qpe_ref, kn_ref, v_ref, kpe_ref,
                   cs_scr, wlat_scr, wuq_scr, wukv_scr, pos_st, wlat_st, wuq_st, wukv_st, sems,
                   *, sub, prep_steps):
    s = pl.program_id(0)
    tm = x_ref.shape[1]
    n_slots = pos_st.shape[0]
    n_pos = pos_st.shape[1]
    n_tab = cs_scr.shape[0] // prep_steps
    n_lat = wlat_scr.shape[1] // prep_steps
    n_q = wuq_scr.shape[0] // prep_steps
    n_kv = wukv_scr.shape[0] // prep_steps

    def chunk_copies(c):
        slot = c % n_slots
        pairs = (
            (pos_hbm.at[c * n_pos:(c + 1) * n_pos], pos_st.at[slot]),
            (win_hbm.at[0:UV_OFF, c * n_lat:(c + 1) * n_lat], wlat_st.at[slot]),
            (wuq_hbm.at[c * n_q:(c + 1) * n_q], wuq_st.at[slot]),
            (wukv_hbm.at[c * n_kv:(c + 1) * n_kv], wukv_st.at[slot]),
        )
        return [pltpu.make_async_copy(src, dst, sems.at[k, slot])
                for k, (src, dst) in enumerate(pairs)]

    for c in range(prep_steps):
        @pl.when(s == c)
        def _(c=c):
            if c == 0:
                for ahead in range(min(n_slots, prep_steps)):
                    for copy in chunk_copies(ahead):
                        copy.start()
            for copy in chunk_copies(c):
                copy.wait()
            slot = c % n_slots
            _prep_kernel(pos_st.at[slot], freq_ref, wlat_st.at[slot], wuq_st.at[slot],
                         wukv_st.at[slot],
                         cs_scr.at[c * n_tab:(c + 1) * n_tab],
                         wlat_scr.at[:, c * n_lat:(c + 1) * n_lat],
                         wuq_scr.at[c * n_q:(c + 1) * n_q],
                         wukv_scr.at[c * n_kv:(c + 1) * n_kv])
            if c + n_slots < prep_steps:
                for copy in chunk_copies(c + n_slots):
                    copy.start()

    @pl.when(s >= prep_steps)
    def _():
        row0 = pl.multiple_of((s - prep_steps) * tm, tm)
        _inproj_tile(x_ref, g_ref, cs_scr.at[pl.ds(row0, tm)], wlat_scr, qg_ref, kvg_ref,
                     wuq_scr, wukv_scr, a_ref, qn_ref, qpe_ref, kn_ref, v_ref, kpe_ref, sub=sub)


def _inproj_tile(x_ref, g_ref, cs_ref, wlat_ref, qg_ref, kvg_ref, wuq_ref, wukv_ref,
                 a_ref, qn_ref, qpe_ref, kn_ref, v_ref, kpe_ref, *, sub):
    def latents(r0):
        a = _rms(x_ref[0, r0:r0 + sub, :], g_ref[...]).astype(BF16)
        a_ref[0, r0:r0 + sub, :] = a
        return _dot_t(a, wlat_ref[...])

    starts = list(range(0, x_ref.shape[1], sub))
    for r0, z in zip(starts, [latents(r0) for r0 in starts]):
        rows = slice(r0, r0 + sub)
        qn = (_rms(z[:, :Q_LORA_RANK], qg_ref[...]) * QK_LOG2_SCALE).astype(BF16)
        kvn = _rms(z[:, Q_LORA_RANK:Q_LORA_RANK + KV_LORA_RANK], kvg_ref[...]).astype(BF16)
        cs = cs_ref[rows, :]
        kpe = _rope_dup(z[:, Q_LORA_RANK + KV_LORA_RANK:], cs)
        lane = lax.broadcasted_iota(jnp.int32, kpe.shape, 1)
        low = lane < QK_ROPE_DIM
        kpe_ref[0, rows, 0:LANES] = jnp.where(low, kpe, 0.0).astype(BF16)
        kpe_ref[0, rows, LANES:2 * LANES] = jnp.where(low, 0.0, kpe).astype(BF16)
        cs_swapped = pltpu.roll(cs, LANES // 2, 1)
        cos4 = jnp.where(low, cs, cs_swapped)
        sin4 = jnp.where(low, cs_swapped, cs)

        heads_per_dot = 4
        width = heads_per_dot * LANES
        nope_cols = N_HEADS * QK_NOPE_DIM
        for hg in range(N_HEADS // heads_per_dot):
            c0 = hg * width
            q_nope = _dot(qn, wuq_ref[:, c0:c0 + width])
            pe0 = nope_cols + c0 // 2
            q_pe = _dot(qn, wuq_ref[:, pe0:pe0 + width // 2])
            kv0 = 2 * c0
            kv_a = _dot(kvn, wukv_ref[:, kv0:kv0 + width])
            kv_b = _dot(kvn, wukv_ref[:, kv0 + width:kv0 + 2 * width])
            for pp in range(heads_per_dot // 2):
                pair = hg * (heads_per_dot // 2) + pp
                qpe_ref[0, pair, rows, :] = _rope_pair(
                    q_pe[:, pp * LANES:(pp + 1) * LANES], cos4, sin4).astype(BF16)
            for hh in range(heads_per_dot):
                h = hg * heads_per_dot + hh
                sl = slice(hh * LANES, (hh + 1) * LANES)
                qn_ref[0, h, rows, :] = q_nope[:, sl].astype(BF16)
                kv = kv_a if hh < heads_per_dot // 2 else kv_b
                k0 = (hh % (heads_per_dot // 2)) * 2 * LANES
                kn_ref[0, h, rows, :] = kv[:, k0:k0 + LANES].astype(BF16)
                v_ref[0, h, rows, :] = kv[:, k0 + LANES:k0 + 2 * LANES].astype(BF16)


def _inproj(x, norm_g, positions, w_in_t, q_g, kv_g, w_uq, w_ukv, tm, sub):
    B, S, D = x.shape
    P = PREP_STEPS
    tiles = S // tm
    r_q, r_kv = w_uq.shape[0], w_ukv.shape[0]
    lat_rows = UV_OFF + LANES - QK_ROPE_DIM
    uq_cols = N_HEADS * QK_NOPE_DIM + (N_HEADS // 2) * LANES
    pos_rep, freq = _rope_table_inputs(positions)
    tile = lambda s: jnp.maximum(s - P, 0)
    row = lambda s: (tile(s) // tiles, tile(s) % tiles, 0)
    head = lambda s: (tile(s) // tiles, 0, tile(s) % tiles, 0)
    in_hbm = pl.BlockSpec(memory_space=pl.ANY)
    slots = PREP_RING_SLOTS
    return pl.pallas_call(
        functools.partial(_inproj_kernel, sub=sub, prep_steps=P),
        grid=(P + B * tiles,),
        in_specs=[
            in_hbm,
            _resident((1, LANES)),
            in_hbm,
            in_hbm,
            in_hbm,
            pl.BlockSpec((1, tm, D), row),
            _resident((1, D)),
            _resident((1, Q_LORA_RANK)),
            _resident((1, KV_LORA_RANK)),
        ],
        out_specs=[
            pl.BlockSpec((1, tm, D), row),
            pl.BlockSpec((1, N_HEADS, tm, LANES), head),
            pl.BlockSpec((1, N_HEADS // 2, tm, LANES), head),
            pl.BlockSpec((1, N_HEADS, tm, LANES), head),
            pl.BlockSpec((1, N_HEADS, tm, LANES), head),
            pl.BlockSpec((1, tm, 2 * LANES), row),
        ],
        out_shape=[
            jax.ShapeDtypeStruct((B, S, D), BF16),
            jax.ShapeDtypeStruct((B, N_HEADS, S, LANES), BF16),
            jax.ShapeDtypeStruct((B, N_HEADS // 2, S, LANES), BF16),
            jax.ShapeDtypeStruct((B, N_HEADS, S, LANES), BF16),
            jax.ShapeDtypeStruct((B, N_HEADS, S, LANES), BF16),
            jax.ShapeDtypeStruct((B, S, 2 * LANES), BF16),
        ],
        scratch_shapes=[
            pltpu.VMEM((positions.size, LANES), F32),
            pltpu.VMEM((lat_rows, D), BF16),
            pltpu.VMEM((r_q, uq_cols), BF16),
            pltpu.VMEM(w_ukv.shape, BF16),
            pltpu.VMEM((slots, pos_rep.shape[0] // P, LANES), F32),
            pltpu.VMEM((slots, UV_OFF, D // P), F32),
            pltpu.VMEM((slots, r_q // P, w_uq.shape[1]), F32),
            pltpu.VMEM((slots, r_kv // P, w_ukv.shape[1]), F32),
            pltpu.SemaphoreType.DMA((4, slots)),
        ],
        compiler_params=_params(1),
        name="inproj",
    )(pos_rep, freq, w_in_t, w_uq, w_ukv, x, norm_g, q_g, kv_g)


def _sgu_kernel(a_ref, wuv_ref, sg_ref, ws_ref, bfull_ref, wos_ref, wg1_ref, bg1_ref, cast_ref,
                m_ref, cast_out_ref, *, sub):
    _cast_rows([cast_ref], [cast_out_ref])
    n_chunks = sub // CHUNK
    t_idx = lax.broadcasted_iota(jnp.int32, (CHUNK, CHUNK), 0)
    s_idx = lax.broadcasted_iota(jnp.int32, (CHUNK, CHUNK), 1)
    causal = t_idx >= s_idx
    ws = [jnp.where(causal, ws_ref[g], 0.0).astype(BF16) for g in range(SGU_GROUPS)]
    b_t = bfull_ref[...].T
    bfull = jnp.concatenate(
        [jnp.broadcast_to(b_t[:, g:g + 1], (CHUNK, SGU_GROUP_DIM)) for g in range(SGU_GROUPS)],
        axis=1)
    def gating_unit(uv_raw):
        uv = jax.nn.gelu(uv_raw)
        u = uv[:, :SGU_WIDTH]
        vn = _rms(uv[:, SGU_WIDTH:], sg_ref[...]).astype(BF16)
        mixed_cols = []
        for g in range(SGU_GROUPS):
            gs = slice(g * SGU_GROUP_DIM, (g + 1) * SGU_GROUP_DIM)
            rhs = jnp.concatenate(
                [vn[c * CHUNK:(c + 1) * CHUNK, gs] for c in range(n_chunks)], axis=1)
            mixed_cols.append(_dot(ws[g], rhs))
        rows = []
        for c in range(n_chunks):
            cs = slice(c * SGU_GROUP_DIM, (c + 1) * SGU_GROUP_DIM)
            mixed = jnp.concatenate([mixed_cols[g][:, cs] for g in range(SGU_GROUPS)], axis=1)
            rows.append(u[c * CHUNK:(c + 1) * CHUNK] * (mixed + bfull))
        return jnp.concatenate(rows, axis=0).astype(BF16)

    a = a_ref[...]
    uv_raw = _dot_t(a, wuv_ref[...])
    gate_raw = _dot_t(a, wg1_ref[...])
    for r0 in range(0, a_ref.shape[0], sub):
        rows = slice(r0, r0 + sub)
        y_sgu = _dot(gating_unit(uv_raw[rows]), wos_ref[...])
        m_ref[rows, :] = jax.nn.sigmoid(gate_raw[rows] + bg1_ref[...]) * y_sgu


def _sgu_branch(a, w_uv, sgu_g, w_s, b_full, w_o_sgu, w_g1, b_g1, cast_w, tm, sub):
    T, D = a.shape
    row = lambda i: (i, 0)
    cast_in, cast_out = _cast_block_specs(*cast_w.shape, 0, T // tm, lambda i: i)
    return pl.pallas_call(
        functools.partial(_sgu_kernel, sub=sub),
        grid=(T // tm,),
        in_specs=[
            pl.BlockSpec((tm, D), row),
            _resident(w_uv.shape),
            _resident(sgu_g.shape),
            _resident(w_s.shape),
            _resident(b_full.shape),
            _resident(w_o_sgu.shape),
            _resident(w_g1.shape),
            _resident_cols(D, 1),
            cast_in,
        ],
        out_specs=[pl.BlockSpec((tm, D), row), cast_out],
        out_shape=[jax.ShapeDtypeStruct((T, D), F32), jax.ShapeDtypeStruct(cast_w.shape, BF16)],
        compiler_params=_params(1),
        name="sgu_branch",
    )(a, w_uv, sgu_g, w_s, b_full, w_o_sgu, w_g1, b_g1, cast_w)


def _attn_kernel(qn_ref, qpe_ref, kn_ref, kpe_ref, v_ref, *rest, tq, n_cast):
    cast_in, (o_ref,), cast_out, (kf_ref, vf_ref) = (
        rest[:n_cast], rest[n_cast:n_cast + 1], rest[n_cast + 1:2 * n_cast + 1], rest[2 * n_cast + 1:])
    _cast_rows(cast_in, cast_out)
    seq = qn_ref.shape[2]
    row = lax.broadcasted_iota(jnp.int32, (tq, tq), 0)
    col = lax.broadcasted_iota(jnp.int32, (tq, tq), 1)
    causal = row >= col
    neg = jnp.finfo(F32).min
    heads = qn_ref.shape[1]
    for hh in range(heads):
        parity = hh % 2
        kf_ref[hh, :, 0:LANES] = kn_ref[0, hh]
        kf_ref[hh, :, LANES:2 * LANES] = kpe_ref[0, :, parity * LANES:(parity + 1) * LANES]
        vf_ref[hh, :, 0:LANES] = v_ref[0, hh]
        vf_ref[hh, :, LANES:2 * LANES] = jnp.ones((seq, LANES), vf_ref.dtype)

    nt = (((1,), (1,)), ((), ()))
    def scores(pair, hh):
        q0 = pair * 2 * tq
        k1, k2 = q0 + tq, q0 + 2 * tq
        q = jnp.concatenate([qn_ref[0, hh, q0:k2, :], qpe_ref[0, hh // 2, q0:k2, :]], axis=1)
        return (lax.dot_general(q, kf_ref[hh, 0:k1, :], nt, preferred_element_type=F32),
                lax.dot_general(q[tq:], kf_ref[hh, k1:k2, :], nt, preferred_element_type=F32))

    def finish(pair, hh, s_main, s_ext):
        q0 = pair * 2 * tq
        k1, k2 = q0 + tq, q0 + 2 * tq
        top = s_main[0:tq]
        top_diag = jnp.where(causal, top[:, q0:k1], neg)
        top = jnp.concatenate([top[:, 0:q0], top_diag], axis=1) if pair else top_diag
        bot = s_main[tq:]
        ext = jnp.where(causal, s_ext, neg)
        m_top = jnp.max(top, axis=-1, keepdims=True)
        m_bot = jnp.maximum(jnp.max(bot, axis=-1, keepdims=True),
                            jnp.max(ext, axis=-1, keepdims=True))
        p_main = jnp.concatenate([jnp.exp2(top - m_top), jnp.exp2(bot - m_bot)], axis=0)
        acc = _dot(p_main.astype(BF16), vf_ref[hh, 0:k1, :])
        acc_bot = acc[tq:] + _dot(jnp.exp2(ext - m_bot).astype(BF16), vf_ref[hh, k1:k2, :])
        lanes = slice(hh * LANES, (hh + 1) * LANES)
        o_ref[0, q0:k1, lanes] = (acc[0:tq, 0:LANES] / acc[0:tq, LANES:]).astype(o_ref.dtype)
        o_ref[0, k1:k2, lanes] = (acc_bot[:, 0:LANES] / acc_bot[:, LANES:]).astype(o_ref.dtype)

    work = [(pair, hh) for pair in reversed(range(seq // (2 * tq))) for hh in range(heads)]
    lead = 2
    pending = [scores(*w) for w in work[:lead]]
    for idx, (pair, hh) in enumerate(work):
        if idx + lead < len(work):
            pending.append(scores(*work[idx + lead]))
        finish(pair, hh, *pending.pop(0))


def _attention(q_nope, q_pe, k_nope, k_pe, v, casts, tq, hb):
    B, H, S, _ = q_nope.shape
    assert hb % 2 == 0, "head pairs share a rope tile"
    groups = H // hb
    specs = [_cast_block_specs(n, w.shape[1], r0, B * groups, lambda b, g: b * groups + g)
             for w, r0, n in casts]
    outs = pl.pallas_call(
        functools.partial(_attn_kernel, tq=tq, n_cast=len(casts)),
        grid=(B, groups),
        in_specs=[
            pl.BlockSpec((1, hb, S, LANES), lambda b, g: (b, g, 0, 0)),
            pl.BlockSpec((1, hb // 2, S, LANES), lambda b, g: (b, g, 0, 0)),
            pl.BlockSpec((1, hb, S, LANES), lambda b, g: (b, g, 0, 0)),
            pl.BlockSpec((1, S, 2 * LANES), lambda b, g: (b, 0, 0)),
            pl.BlockSpec((1, hb, S, LANES), lambda b, g: (b, g, 0, 0)),
        ] + [s_in for s_in, _ in specs],
        out_specs=[pl.BlockSpec((1, S, hb * LANES), lambda b, g: (b, 0, g))]
        + [s_out for _, s_out in specs],
        out_shape=[jax.ShapeDtypeStruct((B, S, H * V_HEAD_DIM), BF16)]
        + [jax.ShapeDtypeStruct((n, w.shape[1]), BF16) for w, _, n in casts],
        scratch_shapes=[pltpu.VMEM((hb, S, 2 * LANES), BF16), pltpu.VMEM((hb, S, 2 * LANES), BF16)],
        compiler_params=_params(2),
        name="mla_attention",
    )(q_nope, q_pe, k_nope, k_pe, v, *[w for w, _, _ in casts])
    return outs[0], outs[1:]


def _merge_kernel(attn_ref, a_ref, m_ref, x_ref, woa_ref, wg0_ref, bg0_ref, wout_ref, fg_ref,
                  cast_ref, h_ref, f_ref, cast_out_ref):
    _cast_rows([cast_ref], [cast_out_ref])
    y_attn = _dot(attn_ref[...], woa_ref[...])
    gate = jax.nn.sigmoid(_dot_t(a_ref[...], wg0_ref[...]) + bg0_ref[...])
    merged = (gate * y_attn + m_ref[...]).astype(BF16)
    h = x_ref[...] + _dot(merged, wout_ref[...])
    h_ref[...] = h
    f_ref[...] = _rms(h, fg_ref[...]).astype(BF16)


def _merge(attn, a, m_sgu, x, w_o_attn, w_g0, b_g0, w_out, ffn_g, cast_w, tm):
    T, D = x.shape
    row = lambda i: (i, 0)
    tile = pl.BlockSpec((tm, D), row)
    cast_in, cast_out = _cast_block_specs(*cast_w.shape, 0, T // tm, lambda i: i)
    return pl.pallas_call(
        _merge_kernel,
        grid=(T // tm,),
        in_specs=[tile, tile, tile, tile,
                  _resident(w_o_attn.shape), _resident(w_g0.shape), _resident_cols(D, 0),
                  _resident(w_out.shape), _resident(ffn_g.shape), cast_in],
        out_specs=[tile, tile, cast_out],
        out_shape=[jax.ShapeDtypeStruct((T, D), F32), jax.ShapeDtypeStruct((T, D), BF16),
                   jax.ShapeDtypeStruct(cast_w.shape, BF16)],
        compiler_params=_params(1),
        name="merge_outproj",
    )(attn, a, m_sgu, x, w_o_attn, w_g0, b_g0, w_out, ffn_g, cast_w)


def _ffn_kernel(f_ref, h_ref, wg_ref, wu_ref, wd_ref, ng_ref, o_ref, *, sub):
    j = pl.program_id(1)

    f = f_ref[...]
    starts = list(range(0, wg_ref.shape[1], sub))
    projected = [(_dot(f, wg_ref[:, c0:c0 + sub]), _dot(f, wu_ref[:, c0:c0 + sub]))
                 for c0 in starts]
    act = jnp.concatenate(
        [(jax.nn.silu(gate) * up).astype(BF16) for gate, up in projected], axis=1)
    o_ref[...] = jnp.where(j == 0, h_ref[...], o_ref[...]) + _dot(act, wd_ref[...])

    @pl.when(j == pl.num_programs(1) - 1)
    def _():
        o_ref[...] = _rms(o_ref[...], ng_ref[...])


def _ffn(f, h, w_gate, w_up, w_down, final_g, tm, tf, sub):
    T, D = h.shape
    d_ff = w_gate.shape[1]
    row = lambda i, j: (i, 0)
    n_i, n_j = T // tm, d_ff // tf
    h_row = lambda i, j: (jnp.minimum(i + (j >= n_j // 2).astype(jnp.int32), n_i - 1), 0)
    return pl.pallas_call(
        functools.partial(_ffn_kernel, sub=sub),
        grid=(n_i, n_j),
        in_specs=[
            pl.BlockSpec((tm, D), row),
            pl.BlockSpec((tm, D), h_row),
            pl.BlockSpec((D, tf), lambda i, j: (0, j)),
            pl.BlockSpec((D, tf), lambda i, j: (0, j)),
            pl.BlockSpec((tf, D), lambda i, j: (j, 0)),
            _resident((1, D)),
        ],
        out_specs=pl.BlockSpec((tm, D), row),
        out_shape=jax.ShapeDtypeStruct((T, D), F32),
        compiler_params=_params(2),
        name="swiglu_ffn",
    )(f, h, w_gate, w_up, w_down, final_g)


def kernel(x, positions, norm_mix_g, w_in, b_gate, q_norm_g, w_uq, kv_norm_g, w_ukv, w_o_attn,
           sgu_norm_g, w_sgu, b_sgu, w_o_sgu, w_out, norm_ffn_g, w_gate_ffn, w_up_ffn,
           w_down_ffn, norm_final_g):
    B, S, D = x.shape
    T = B * S
    depth = w_in.shape[0]
    assert depth == 1, "the final norm is fused into the FFN epilogue of a single layer"
    assert w_in.shape[1:] == (D, D_IN)

    row_vec = lambda v: v.reshape(1, -1).astype(F32)

    h = x
    out = None
    for l in range(depth):
        w_in_t = jnp.swapaxes(w_in[l], 0, 1)
        b_full = b_sgu[l].astype(F32)
        b_gates = row_vec(b_gate[l])

        a, q_nope, q_pe, k_nope, v, k_pe = _inproj(
            h, row_vec(norm_mix_g[l]), positions, w_in_t, row_vec(q_norm_g[l]),
            row_vec(kv_norm_g[l]), w_uq[l], w_ukv[l], tm=INPROJ_ROWS, sub=ROW_SUB)
        a2 = a.reshape(T, D)
        whole = lambda w: (w, 0, w.shape[0])
        attn, (w_uv_t, w_g0_t, w_g1_t, w_os, w_oa, w_o, w_uf) = _attention(
            q_nope, q_pe, k_nope, k_pe, v,
            [(w_in_t, UV_OFF, 2 * SGU_WIDTH), (w_in_t, GATE_OFF, D), (w_in_t, GATE_OFF + D, D),
             whole(w_o_sgu[l]), whole(w_o_attn[l]), whole(w_out[l]), whole(w_up_ffn[l])],
            tq=ATTN_Q_ROWS, hb=ATTN_HEADS_PER_STEP)
        m_sgu, w_gf = _sgu_branch(
            a2, w_uv_t, row_vec(sgu_norm_g[l]), w_sgu[l], b_full, w_os, w_g1_t,
            b_gates, w_gate_ffn[l], tm=SGU_ROWS, sub=ROW_SUB)
        h_mid, f, w_df = _merge(
            attn.reshape(T, D), a2, m_sgu, h.reshape(T, D), w_oa, w_g0_t,
            b_gates, w_o, row_vec(norm_ffn_g[l]), w_down_ffn[l], tm=MERGE_ROWS)
        out = _ffn(f, h_mid, w_gf, w_uf, w_df, row_vec(norm_final_g), tm=FFN_ROWS,
                   tf=FFN_COLS, sub=FFN_COL_SUB)
        h = out.reshape(B, S, D)
    return h
```
